```python
import math
import jax, jax.numpy as jnp
from jax import lax
import numpy as np

D_MODEL = 1024
BATCH = 2
SEQ = 8192
DEPTH = 2

N_MIXERS = 2
DN_ALPHA = (2 * DEPTH) ** 0.25
DN_BETA = (8 * DEPTH) ** -0.25
LN_EPS = 1e-5
RMS_EPS = 1e-6

NSA_HEADS = 16
NSA_HEAD_DIM = D_MODEL // NSA_HEADS
NSA_KV_GROUPS = 4
NSA_HPG = NSA_HEADS // NSA_KV_GROUPS
NSA_KV_DIM = NSA_KV_GROUPS * NSA_HEAD_DIM
CMP_LEN = 32
CMP_STRIDE = 16
CMP_HIDDEN = 256
SLC_LEN = 64
N_SEL = 16
WINDOW = 512
NSA_QBLOCK = 64
NSA_IN_DIM = D_MODEL + 6 * NSA_KV_DIM + 3 * NSA_HEADS

GLA_HEADS = 4
GLA_KEY_DIM = D_MODEL // 2
GLA_VAL_DIM = D_MODEL
GLA_DK = GLA_KEY_DIM // GLA_HEADS
GLA_DV = GLA_VAL_DIM // GLA_HEADS
GLA_GATE_RANK = 16
GLA_GATE_NORM = 16.0
GLA_CHUNK = 64
GLA_IN_DIM = 2 * GLA_KEY_DIM + 2 * GLA_VAL_DIM + GLA_GATE_RANK

FFN_DENSE = 2816
N_EXPERTS = 8
TOP_K = 2
FFN_EXPERT = 3584

kernel_name = "hybrid_nsa_gla_moe_deepnorm"


def layer_norm(x, g, b):
    xf = x.astype(jnp.float32)
    mu = xf.mean(-1, keepdims=True)
    var = jnp.square(xf - mu).mean(-1, keepdims=True)
    return ((xf - mu) * lax.rsqrt(var + LN_EPS) * g + b).astype(x.dtype)


def alibi_slopes(n):
    return 2.0 ** (-8.0 * (jnp.arange(n, dtype=jnp.float32) + 1.0) / n)


def masked_softmax(s, mask):
    s = jnp.where(mask, s.astype(jnp.float32), -1e30)
    m = s.max(-1, keepdims=True)
    e = jnp.exp(s - m) * mask
    return e / jnp.maximum(e.sum(-1, keepdims=True), 1e-30)


def compress_blocks(blk, pe, w1, w2):
    blk = blk + pe[None, None, :, None, :]
    b, nc, l, g, dk = blk.shape
    flat = blk.transpose(0, 1, 3, 2, 4).reshape(b, nc, g, l * dk)
    return jax.nn.gelu(flat @ w1) @ w2


def nsa_mixer(x, w_in, pe_k, pe_v, wk1, wk2, wv1, wv2, w_o):
    B, T, _ = x.shape
    G, HPG, DK = NSA_KV_GROUPS, NSA_HEAD_DIM and NSA_HPG, NSA_HEAD_DIM
    proj = x @ w_in
    splits = np.cumsum([D_MODEL] + [NSA_KV_DIM] * 6).tolist()
    q, kc, vc, ks, vs, kw, vw, gl = jnp.split(proj, splits, axis=-1)
    q = (q * NSA_HEAD_DIM ** -0.5).reshape(B, T, G, HPG, DK)
    kc, vc, ks, vs, kw, vw = [a.reshape(B, T, G, DK) for a in (kc, vc, ks, vs, kw, vw)]
    gates = jax.nn.sigmoid(gl.astype(jnp.float32)).reshape(B, T, G, HPG, 3)

    n_cmp = (T - CMP_LEN) // CMP_STRIDE + 1
    cmp_idx = np.arange(n_cmp)[:, None] * CMP_STRIDE + np.arange(CMP_LEN)[None, :]
    k_cmp = compress_blocks(kc[:, cmp_idx], pe_k, wk1, wk2)
    v_cmp = compress_blocks(vc[:, cmp_idx], pe_v, wv1, wv2)
    cmp_end = jnp.asarray(cmp_idx[:, -1])

    n_slc = T // SLC_LEN
    n_sel = min(N_SEL, n_slc)
    cs = np.arange(n_cmp) * CMP_STRIDE
    ce = cs + CMP_LEN - 1
    bs = np.arange(n_slc) * SLC_LEN
    be = bs + SLC_LEN - 1
    overlap = jnp.asarray(((cs[:, None] <= be[None]) & (ce[:, None] >= bs[None])).astype(np.float32))
    ks_blk = ks.reshape(B, n_slc, SLC_LEN, G, DK).transpose(0, 3, 1, 2, 4)
    vs_blk = vs.reshape(B, n_slc, SLC_LEN, G, DK).transpose(0, 3, 1, 2, 4)
    slc_start = jnp.arange(n_slc) * SLC_LEN
    jidx = jnp.arange(n_slc)[None, :]

    pad = ((0, 0), (WINDOW, 0), (0, 0), (0, 0))
    kw_pad = jnp.pad(kw, pad)
    vw_pad = jnp.pad(vw, pad)

    slopes = alibi_slopes(NSA_HEADS).reshape(G, HPG)
    bi = jnp.arange(B)[:, None, None, None]
    gi = jnp.arange(G)[None, :, None, None]

    def block(c):
        t0 = c * NSA_QBLOCK
        qb = lax.dynamic_slice_in_dim(q, t0, NSA_QBLOCK, axis=1)
        gb = lax.dynamic_slice_in_dim(gates, t0, NSA_QBLOCK, axis=1)
        tq = t0 + jnp.arange(NSA_QBLOCK)

        dist = tq[:, None] - cmp_end[None, :]
        s = jnp.einsum('bqghd,bngd->bghqn', qb, k_cmp) - slopes[None, :, :, None, None] * dist
        p_cmp = masked_softmax(s, dist >= 0)
        o_cmp = jnp.einsum('bghqn,bngd->bqghd', p_cmp, v_cmp.astype(jnp.float32))

        imp = jnp.einsum('bghqn,nj->bgqj', p_cmp, overlap)
        valid = slc_start[None, :] <= tq[:, None]
        cur = (tq // SLC_LEN)[:, None]
        forced = valid & ((jidx == 0) | (jidx == cur) | (jidx == cur - 1))
        score = jnp.where(forced, 1e9, jnp.where(valid, imp, -1e9))
        _, sel = lax.top_k(score, n_sel)
        kg = ks_blk[bi, gi, sel].reshape(B, G, NSA_QBLOCK, n_sel * SLC_LEN, DK)
        vg = vs_blk[bi, gi, sel].reshape(B, G, NSA_QBLOCK, n_sel * SLC_LEN, DK)
        pos = (sel[..., None] * SLC_LEN + jnp.arange(SLC_LEN)).reshape(B, G, NSA_QBLOCK, n_sel * SLC_LEN)
        dist = tq[None, None, :, None] - pos
        s = jnp.einsum('bqghd,bgqkd->bghqk', qb, kg) - slopes[None, :, :, None, None] * dist[:, :, None]
        p = masked_softmax(s, (dist >= 0)[:, :, None])
        o_slc = jnp.einsum('bghqk,bgqkd->bqghd', p, vg.astype(jnp.float32))

        kwb = lax.dynamic_slice_in_dim(kw_pad, t0, NSA_QBLOCK + WINDOW, axis=1)
        vwb = lax.dynamic_slice_in_dim(vw_pad, t0, NSA_QBLOCK + WINDOW, axis=1)
        kpos = t0 - WINDOW + jnp.arange(NSA_QBLOCK + WINDOW)
        dist = tq[:, None] - kpos[None, :]
        wmask = (dist >= 0) & (dist < WINDOW) & (kpos[None, :] >= 0)
        s = jnp.einsum('bqghd,bkgd->bghqk', qb, kwb) - slopes[None, :, :, None, None] * dist
        p = masked_softmax(s, wmask)
        o_win = jnp.einsum('bghqk,bkgd->bqghd', p, vwb.astype(jnp.float32))

        o = gb[..., 0:1] * o_cmp + gb[..., 1:2] * o_slc + gb[..., 2:3] * o_win
        return o.reshape(B, NSA_QBLOCK, NSA_HEADS * NSA_HEAD_DIM).astype(x.dtype)

    out = lax.map(block, jnp.arange(T // NSA_QBLOCK))
    out = out.transpose(1, 0, 2, 3).reshape(B, T, NSA_HEADS * NSA_HEAD_DIM)
    return out @ w_o


def gla_mixer(x, w_in, w_gate2, b_gate2, head_norm_g, w_o):
    B, T, _ = x.shape
    H, C = GLA_HEADS, GLA_CHUNK
    N = T // C
    proj = x @ w_in
    splits = np.cumsum([GLA_KEY_DIM, GLA_KEY_DIM, GLA_VAL_DIM, GLA_VAL_DIM]).tolist()
    q, k, v, g, a_low = jnp.split(proj, splits, axis=-1)
    log_a = jax.nn.log_sigmoid((a_low @ w_gate2 + b_gate2).astype(jnp.float32)) / GLA_GATE_NORM

    def to_chunks(a, d):
        return a.astype(jnp.float32).reshape(B, N, C, H, d).transpose(1, 0, 3, 2, 4)

    qc = to_chunks(q * GLA_DK ** -0.5, GLA_DK)
    kc = to_chunks(k, GLA_DK)
    vc = to_chunks(v, GLA_DV)
    ac = to_chunks(log_a, GLA_DK)
    causal = jnp.tril(jnp.ones((C, C), dtype=bool))
    ref = C // 2

    def step(S, inp):
        qi, ki, vi, ai = inp
        b = jnp.cumsum(ai, axis=2)
        b_ref = b[:, :, ref:ref + 1]
        b_last = b[:, :, -1:]
        A = jnp.einsum('bhid,bhjd->bhij', qi * jnp.exp(b - b_ref), ki * jnp.exp(b_ref - b))
        A = jnp.where(causal, A, 0.0)
        o = jnp.einsum('bhij,bhje->bhie', A, vi) + jnp.einsum('bhid,bhde->bhie', qi * jnp.exp(b), S)
        S = S * jnp.exp(b_last)[:, :, 0, :, None] + jnp.einsum('bhjd,bhje->bhde', ki * jnp.exp(b_last - b), vi)
        return S, o

    S0 = jnp.zeros((B, H, GLA_DK, GLA_DV), jnp.float32)
    _, o = lax.scan(step, S0, (qc, kc, vc, ac))
    o = o.transpose(1, 0, 3, 2, 4).reshape(B, T, H, GLA_DV)
    o = o * lax.rsqrt(jnp.mean(jnp.square(o), -1, keepdims=True) + RMS_EPS)
    o = o.reshape(B, T, GLA_VAL_DIM) * head_norm_g * jax.nn.silu(g.astype(jnp.float32))
    return o.astype(x.dtype) @ w_o


def swiglu(x, w_gu, w_down):
    a, b = jnp.split(x @ w_gu, 2, axis=-1)
    return (jax.nn.silu(a) * b) @ w_down


def moe_ffn(x, w_router, w_gu, w_down):
    logits = (x @ w_router).astype(jnp.float32)
    top_val, top_idx = lax.top_k(logits, TOP_K)
    w = jax.nn.softmax(top_val, axis=-1)
    gate = jnp.sum(jax.nn.one_hot(top_idx, N_EXPERTS, dtype=jnp.float32) * w[..., None], axis=-2)
    y = jnp.zeros(x.shape, jnp.float32)
    for e in range(N_EXPERTS):
        y = y + gate[..., e:e + 1] * swiglu(x, w_gu[e], w_down[e])
    return y.astype(x.dtype)


def setup_inputs(seed: int = 0) -> dict:
    key = jax.random.key(seed)
    ks = jax.random.split(key, 32)
    nrm = lambda k, shape, scale: jax.random.normal(k, shape, jnp.float32) * scale
    gain = lambda k: 1.0 + nrm(k, (D_MODEL,), 0.02)
    bias = lambda k: nrm(k, (D_MODEL,), 0.02)
    L = CMP_LEN * NSA_HEAD_DIM
    return {
        "x": nrm(ks[0], (BATCH, SEQ, D_MODEL), 1.0),
        "l0_w_in": nrm(ks[1], (D_MODEL, NSA_IN_DIM), D_MODEL ** -0.5),
        "l0_cmp_pe_k": nrm(ks[2], (CMP_LEN, NSA_HEAD_DIM), 0.02),
        "l0_cmp_pe_v": nrm(ks[3], (CMP_LEN, NSA_HEAD_DIM), 0.02),
        "l0_cmp_wk1": nrm(ks[4], (L, CMP_HIDDEN), L ** -0.5),
        "l0_cmp_wk2": nrm(ks[5], (CMP_HIDDEN, NSA_HEAD_DIM), CMP_HIDDEN ** -0.5),
        "l0_cmp_wv1": nrm(ks[6], (L, CMP_HIDDEN), L ** -0.5),
        "l0_cmp_wv2": nrm(ks[7], (CMP_HIDDEN, NSA_HEAD_DIM), CMP_HIDDEN ** -0.5),
        "l0_w_o": nrm(ks[8], (D_MODEL, D_MODEL), DN_BETA * D_MODEL ** -0.5),
        "l0_ln1_g": gain(ks[9]),
        "l0_ln1_b": bias(ks[10]),
        "l0_ffn_w_gu": nrm(ks[11], (D_MODEL, 2 * FFN_DENSE), D_MODEL ** -0.5),
        "l0_ffn_w_down": nrm(ks[12], (FFN_DENSE, D_MODEL), DN_BETA * FFN_DENSE ** -0.5),
        "l0_ln2_g": gain(ks[13]),
        "l0_ln2_b": bias(ks[14]),
        "l1_w_in": nrm(ks[15], (D_MODEL, GLA_IN_DIM), D_MODEL ** -0.5),
        "l1_w_gate2": nrm(ks[16], (GLA_GATE_RANK, GLA_KEY_DIM), GLA_GATE_RANK ** -0.5),
        "l1_b_gate2": nrm(ks[17], (GLA_KEY_DIM,), 0.1),
        "l1_head_norm_g": 1.0 + nrm(ks[18], (GLA_VAL_DIM,), 0.02),
        "l1_w_o": nrm(ks[19], (GLA_VAL_DIM, D_MODEL), DN_BETA * GLA_VAL_DIM ** -0.5),
        "l1_ln1_g": gain(ks[20]),
        "l1_ln1_b": bias(ks[21]),
        "l1_router": nrm(ks[22], (D_MODEL, N_EXPERTS), D_MODEL ** -0.5),
        "l1_moe_w_gu": nrm(ks[23], (N_EXPERTS, D_MODEL, 2 * FFN_EXPERT), D_MODEL ** -0.5),
        "l1_moe_w_down": nrm(ks[24], (N_EXPERTS, FFN_EXPERT, D_MODEL), DN_BETA * FFN_EXPERT ** -0.5),
        "l1_ln2_g": gain(ks[25]),
        "l1_ln2_b": bias(ks[26]),
    }


def reference(x, l0_w_in, l0_cmp_pe_k, l0_cmp_pe_v, l0_cmp_wk1, l0_cmp_wk2, l0_cmp_wv1, l0_cmp_wv2,
              l0_w_o, l0_ln1_g, l0_ln1_b, l0_ffn_w_gu, l0_ffn_w_down, l0_ln2_g, l0_ln2_b,
              l1_w_in, l1_w_gate2, l1_b_gate2, l1_head_norm_g, l1_w_o, l1_ln1_g, l1_ln1_b,
              l1_router, l1_moe_w_gu, l1_moe_w_down, l1_ln2_g, l1_ln2_b):
    token_mixers = [
        lambda h: nsa_mixer(h, l0_w_in, l0_cmp_pe_k, l0_cmp_pe_v, l0_cmp_wk1, l0_cmp_wk2,
                            l0_cmp_wv1, l0_cmp_wv2, l0_w_o),
        lambda h: gla_mixer(h, l1_w_in, l1_w_gate2, l1_b_gate2, l1_head_norm_g, l1_w_o),
    ]
    channel_mixers = [
        lambda h: swiglu(h, l0_ffn_w_gu, l0_ffn_w_down),
        lambda h: moe_ffn(h, l1_router, l1_moe_w_gu, l1_moe_w_down),
    ]
    norms1 = [(l0_ln1_g, l0_ln1_b), (l1_ln1_g, l1_ln1_b)]
    norms2 = [(l0_ln2_g, l0_ln2_b), (l1_ln2_g, l1_ln2_b)]
    for i in range(DEPTH):
        x = layer_norm(DN_ALPHA * x + token_mixers[i % N_MIXERS](x), *norms1[i])
        x = layer_norm(DN_ALPHA * x + channel_mixers[i][(0)] if False else DN_ALPHA * x + channel_mixers[i](x), *norms2[i])
    return x
```

```python
import functools

import numpy as np
import jax
import jax.numpy as jnp
from jax import lax
from jax.experimental import pallas as pl
from jax.experimental.pallas import tpu as pltpu

F32 = jnp.float32
MXU_DTYPE = jnp.bfloat16

D_MODEL = 1024
DEPTH = 2
DN_ALPHA = (2 * DEPTH) ** 0.25
LN_EPS = 1e-5
RMS_EPS = 1e-6

NSA_HEADS = 16
NSA_HEAD_DIM = 64
NSA_KV_GROUPS = 4
NSA_HPG = NSA_HEADS // NSA_KV_GROUPS
NSA_KV_DIM = NSA_KV_GROUPS * NSA_HEAD_DIM
CMP_LEN = 32
CMP_STRIDE = 16
CMP_HIDDEN = 256
SLC_LEN = 64
N_SEL = 16
WINDOW = 512
QTILE = 128
KBLK = 128
NCOL = NSA_HPG * QTILE

GLA_HEADS = 4
GLA_KEY_DIM = D_MODEL // 2
GLA_VAL_DIM = D_MODEL
GLA_DK = GLA_KEY_DIM // GLA_HEADS
GLA_DV = GLA_VAL_DIM // GLA_HEADS
GLA_GATE_RANK = 16
GLA_GATE_NORM = 16.0
GLA_CHUNK = 64
GLA_TSTEP = 512

FFN_DENSE = 2816
N_EXPERTS = 8
TOP_K = 2
FFN_EXPERT = 3584
MOE_TM = 256

LANE = 128
NEG = -1e30
VMEM_LIMIT = 56 * 1024 * 1024


def _params(*sem):
    return pltpu.CompilerParams(dimension_semantics=sem, vmem_limit_bytes=VMEM_LIMIT)


def _mx(a):
    return a.astype(MXU_DTYPE)


def _dot(a, b):
    return jnp.dot(a, b, preferred_element_type=F32)


def _dot_nt(a, b):
    return lax.dot_general(a, b, (((1,), (1,)), ((), ())), preferred_element_type=F32)


def _mm_kernel(x_ref, w_ref, o_ref):
    o_ref[...] = _dot(_mx(x_ref[...]), w_ref[...]).astype(o_ref.dtype)


def _matmul(x, w, tm, tn, out_dtype=F32):
    m, k = x.shape
    n = w.shape[1]
    return pl.pallas_call(
        _mm_kernel,
        grid=(m // tm, n // tn),
        in_specs=[pl.BlockSpec((tm, k), lambda i, j: (i, 0)),
                  pl.BlockSpec((k, tn), lambda i, j: (0, j))],
        out_specs=pl.BlockSpec((tm, tn), lambda i, j: (i, j)),
        out_shape=jax.ShapeDtypeStruct((m, n), out_dtype),
        compiler_params=_params("parallel", "parallel"),
        name="matmul",
    )(x, w)


def _layer_norm_rows(h, g, b):
    mu = jnp.mean(h, -1, keepdims=True)
    d = h - mu
    var = jnp.mean(d * d, -1, keepdims=True)
    return d * lax.rsqrt(var + LN_EPS) * g + b


def _mm_ln_kernel(x_ref, w_ref, r_ref, g_ref, b_ref, o_ref):
    y = _dot(_mx(x_ref[...]), w_ref[...])
    o_ref[...] = _layer_norm_rows(DN_ALPHA * r_ref[...] + y, g_ref[...], b_ref[...])


def _matmul_res_ln(x, w, res, g, b, tm):
    m, k = x.shape
    n = w.shape[1]
    return pl.pallas_call(
        _mm_ln_kernel,
        grid=(m // tm,),
        in_specs=[pl.BlockSpec((tm, k), lambda i: (i, 0)),
                  pl.BlockSpec((k, n), lambda i: (0, 0)),
                  pl.BlockSpec((tm, n), lambda i: (i, 0)),
                  pl.BlockSpec((1, n), lambda i: (0, 0)),
                  pl.BlockSpec((1, n), lambda i: (0, 0))],
        out_specs=pl.BlockSpec((tm, n), lambda i: (i, 0)),
        out_shape=jax.ShapeDtypeStruct((m, n), F32),
        compiler_params=_params("parallel"),
        name="matmul_res_ln",
    )(x, w, res, g.reshape(1, n), b.reshape(1, n))


def _swiglu_up_kernel(x_ref, wg_ref, wu_ref, o_ref):
    x = _mx(x_ref[...])
    a = _dot(x, wg_ref[...])
    b = _dot(x, wu_ref[...])
    o_ref[...] = (a * jax.nn.sigmoid(a) * b).astype(o_ref.dtype)


def _swiglu_up(x, w_gu, tm, tn):
    m, k = x.shape
    f = w_gu.shape[1] // 2
    nj = f // tn
    return pl.pallas_call(
        _swiglu_up_kernel,
        grid=(m // tm, nj),
        in_specs=[pl.BlockSpec((tm, k), lambda i, j: (i, 0)),
                  pl.BlockSpec((k, tn), lambda i, j: (0, j)),
                  pl.BlockSpec((k, tn), lambda i, j: (0, j + nj))],
        out_specs=pl.BlockSpec((tm, tn), lambda i, j: (i, j)),
        out_shape=jax.ShapeDtypeStruct((m, f), MXU_DTYPE),
        compiler_params=_params("parallel", "parallel"),
        name="swiglu_up",
    )(x, w_gu, w_gu)


def _cmp_up_kernel(c_ref, pea_ref, peb_ref, wa_ref, wb_ref, p_ref, q_ref):
    c = c_ref[...]
    p_ref[...] = _dot(_mx(c + pea_ref[...]), wa_ref[...])
    q_ref[...] = _dot(_mx(c + peb_ref[...]), wb_ref[...])


def _cmp_down_kernel(p_ref, q_ref, w_ref, o_ref):
    h = jax.nn.gelu(p_ref[...] + q_ref[...])
    o_ref[...] = _dot(_mx(h), w_ref[...])


def _compress(a, pe, w1, w2):
    bsz, t, g, dk = a.shape
    nch = t // CMP_STRIDE
    half = CMP_STRIDE * dk
    chunks = a.reshape(bsz, nch, CMP_STRIDE, g, dk).transpose(0, 1, 3, 2, 4).reshape(bsz * nch * g, half)
    rows = chunks.shape[0]
    tm = min(512, rows)
    pe_a = pe[:CMP_STRIDE].reshape(1, half)
    pe_b = pe[CMP_STRIDE:].reshape(1, half)
    w1m = _mx(w1)
    p, q = pl.pallas_call(
        _cmp_up_kernel,
        grid=(rows // tm,),
        in_specs=[pl.BlockSpec((tm, half), lambda i: (i, 0)),
                  pl.BlockSpec((1, half), lambda i: (0, 0)),
                  pl.BlockSpec((1, half), lambda i: (0, 0)),
                  pl.BlockSpec((half, CMP_HIDDEN), lambda i: (0, 0)),
                  pl.BlockSpec((half, CMP_HIDDEN), lambda i: (0, 0))],
        out_specs=[pl.BlockSpec((tm, CMP_HIDDEN), lambda i: (i, 0)),
                   pl.BlockSpec((tm, CMP_HIDDEN), lambda i: (i, 0))],
        out_shape=[jax.ShapeDtypeStruct((rows, CMP_HIDDEN), F32)] * 2,
        compiler_params=_params("parallel"),
        name="cmp_up",
    )(chunks, pe_a, pe_b, w1m[:half], w1m[half:])
    q = q.reshape(bsz, nch, g, CMP_HIDDEN)
    q = jnp.concatenate([q[:, 1:], jnp.zeros_like(q[:, :1])], axis=1).reshape(rows, CMP_HIDDEN)
    w2p = jnp.pad(_mx(w2), ((0, 0), (0, LANE - dk)))
    out = pl.pallas_call(
        _cmp_down_kernel,
        grid=(rows // tm,),
        in_specs=[pl.BlockSpec((tm, CMP_HIDDEN), lambda i: (i, 0)),
                  pl.BlockSpec((tm, CMP_HIDDEN), lambda i: (i, 0)),
                  pl.BlockSpec((CMP_HIDDEN, LANE), lambda i: (0, 0))],
        out_specs=pl.BlockSpec((tm, LANE), lambda i: (i, 0)),
        out_shape=jax.ShapeDtypeStruct((rows, LANE), F32),
        compiler_params=_params("parallel"),
        name="cmp_down",
    )(p, q, w2p)
    out = out[:, :dk].reshape(bsz, nch, g, dk)
    valid = (jnp.arange(nch) < nch - 1)[None, :, None, None]
    return jnp.where(valid, out, 0.0)


def _nsa_cmp_kernel(qt_ref, kc_ref, vct_ref, ovl_ref, slope_ref, ocmp_ref, sel_ref, flag_ref, sc_ref):
    c = pl.program_id(2)
    ncp = kc_ref.shape[2]
    ns = ovl_ref.shape[0]
    qt = qt_ref[0, 0, 0]
    s = _dot(kc_ref[0, 0], qt)
    n_i = lax.broadcasted_iota(jnp.int32, (ncp, NCOL), 0)
    col = lax.broadcasted_iota(jnp.int32, (ncp, NCOL), 1)
    tq = c * QTILE + (col & (QTILE - 1))
    dist = tq - (n_i * CMP_STRIDE + (CMP_LEN - 1))
    mask = dist >= 0
    s = jnp.where(mask, s - slope_ref[0] * dist.astype(F32), NEG)
    m = jnp.max(s, axis=0, keepdims=True)
    e = jnp.where(mask, jnp.exp(s - m), 0.0)
    l = jnp.sum(e, axis=0, keepdims=True)
    p = e * (1.0 / jnp.maximum(l, 1e-30))
    ocmp_ref[0, 0, 0] = _dot(vct_ref[0, 0], _mx(p))

    psum = p[:, 0:QTILE]
    for h in range(1, NSA_HPG):
        psum = psum + p[:, h * QTILE:(h + 1) * QTILE]
    hi = _mx(psum)
    lo = _mx(psum - hi.astype(F32))
    ovl = ovl_ref[...]
    imp = _dot(ovl, hi) + _dot(ovl, lo)

    j_i = lax.broadcasted_iota(jnp.int32, (ns, QTILE), 0)
    tq2 = c * QTILE + lax.broadcasted_iota(jnp.int32, (ns, QTILE), 1)
    valid = j_i * SLC_LEN <= tq2
    cur = tq2 >> (SLC_LEN.bit_length() - 1)
    forced = valid & ((j_i == 0) | (j_i == cur) | (j_i == cur - 1))
    score = jnp.where(forced, 1e9, jnp.where(valid, imp, -1e9))
    sc_ref[...] = score

    def body(i, cnt):
        row = sc_ref[pl.ds(i, 1), :]
        before = (row > score) | ((row == score) & (i < j_i))
        return cnt + jnp.where(before, 1, 0)

    rank = lax.fori_loop(0, ns, body, jnp.zeros((ns, QTILE), jnp.int32), unroll=8)
    sel = jnp.where(rank < min(N_SEL, ns), 1.0, 0.0)
    sel_ref[0, 0, 0] = sel
    flag_ref[0, 0, 0] = _dot_nt(jnp.ones((8, QTILE), MXU_DTYPE), _mx(sel))


def _nsa_cmp_select(qt, kcmp, vcmp_t, ovl_t, slopes):
    bsz, g, nt, dk, _ = qt.shape
    ncp = kcmp.shape[2]
    ns = ovl_t.shape[0]
    return pl.pallas_call(
        _nsa_cmp_kernel,
        grid=(bsz, g, nt),
        in_specs=[pl.BlockSpec((1, 1, 1, dk, NCOL), lambda b, gg, c: (b, gg, c, 0, 0)),
                  pl.BlockSpec((1, 1, ncp, dk), lambda b, gg, c: (b, gg, 0, 0)),
                  pl.BlockSpec((1, 1, dk, ncp), lambda b, gg, c: (b, gg, 0, 0)),
                  pl.BlockSpec((ns, ncp), lambda b, gg, c: (0, 0)),
                  pl.BlockSpec((1, 1, NCOL), lambda b, gg, c: (gg, 0, 0))],
        out_specs=[pl.BlockSpec((1, 1, 1, dk, NCOL), lambda b, gg, c: (b, gg, c, 0, 0)),
                   pl.BlockSpec((1, 1, 1, ns, QTILE), lambda b, gg, c: (b, gg, c, 0, 0)),
                   pl.BlockSpec((1, 1, 1, 8, ns), lambda b, gg, c: (b, gg, c, 0, 0))],
        out_shape=[jax.ShapeDtypeStruct((bsz, g, nt, dk, NCOL), F32),
                   jax.ShapeDtypeStruct((bsz, g, nt, ns, QTILE), F32),
                   jax.ShapeDtypeStruct((bsz, g, nt, 8, ns), F32)],
        scratch_shapes=[pltpu.VMEM((ns, QTILE), F32)],
        compiler_params=_params("parallel", "parallel", "parallel"),
        name="nsa_cmp_select",
    )(qt, kcmp, vcmp_t, ovl_t, slopes)


def _nsa_attn_kernel(pf_ref, qt_ref, ks_ref, vst_ref, kw_ref, vwt_ref, sel_ref, ocmp_ref, gl_ref,
                     slope_ref, o_ref, m_sc, l_sc, acc_sc):
    b = pl.program_id(0)
    g = pl.program_id(1)
    c = pl.program_id(2)
    nt = pl.num_programs(2)
    npair = ks_ref.shape[2]
    qt = qt_ref[0, 0, 0]
    key_i = lax.broadcasted_iota(jnp.int32, (KBLK, QTILE), 0)
    tq = c * QTILE + lax.broadcasted_iota(jnp.int32, (KBLK, QTILE), 1)
    slope = slope_ref[0]

    def reset():
        m_sc[...] = jnp.full(m_sc.shape, NEG, F32)
        l_sc[...] = jnp.zeros(l_sc.shape, F32)
        acc_sc[...] = jnp.zeros(acc_sc.shape, F32)

    def attend(k_blk, vt_blk, kpos0, extra_mask):
        s = _dot(k_blk, qt)
        dist = tq - (kpos0 + key_i)
        mask = extra_mask(dist) & (dist >= 0)
        distf = dist.astype(F32)
        es = []
        for h in range(NSA_HPG):
            cs = slice(h * QTILE, (h + 1) * QTILE)
            sh = jnp.where(mask, s[:, cs] - slope[:, cs] * distf, NEG)
            m_old = m_sc[:, cs]
            m_new = jnp.maximum(m_old, jnp.max(sh, axis=0, keepdims=True))
            alpha = jnp.exp(m_old - m_new)
            e = jnp.where(mask, jnp.exp(sh - m_new), 0.0)
            l_sc[:, cs] = alpha * l_sc[:, cs] + jnp.sum(e, axis=0, keepdims=True)
            acc_sc[:, cs] = alpha * acc_sc[:, cs]
            m_sc[:, cs] = m_new
            es.append(_mx(e))
        acc_sc[...] += _dot(vt_blk, jnp.concatenate(es, axis=1))

    def result():
        return acc_sc[...] * (1.0 / jnp.maximum(l_sc[...], 1e-30))

    reset()
    pf_base = ((b * NSA_KV_GROUPS + g) * nt + c) * npair

    def sel_body(jj, carry):
        @pl.when(pf_ref[pf_base + jj] > 0)
        def _():
            rows = sel_ref[0, 0, 0, pl.ds(2 * jj, 2), :]
            picked = jnp.where(key_i < SLC_LEN, rows[0:1, :], rows[1:2, :]) > 0.5
            attend(ks_ref[0, 0, jj], vst_ref[0, 0, jj], jj * KBLK, lambda d: picked)
        return carry

    lax.fori_loop(0, c + 1, sel_body, 0)
    o_sel = result()

    reset()
    for t in range(WINDOW // KBLK + 1):
        jw = c - WINDOW // KBLK + t

        @pl.when(jw >= 0)
        def _():
            attend(kw_ref[0, 0, jw], vwt_ref[0, 0, jw], jw * KBLK, lambda d: d < WINDOW)

    o_win = result()

    gates = jax.nn.sigmoid(gl_ref[0, 0, 0])
    o_ref[0, 0, 0] = gates[0:1] * ocmp_ref[0, 0, 0] + gates[1:2] * o_sel + gates[2:3] * o_win


def _nsa_attend(pair_flags, qt, ks, vs_t, kw, vw_t, sel_t, ocmp_t, gl_t, slopes):
    bsz, g, nt, dk, _ = qt.shape
    npair = ks.shape[2]
    ns = sel_t.shape[3]
    tile = lambda b, gg, c, pf: (b, gg, c, 0, 0)
    whole = lambda b, gg, c, pf: (b, gg, 0, 0, 0)
    grid_spec = pltpu.PrefetchScalarGridSpec(
        num_scalar_prefetch=1,
        grid=(bsz, g, nt),
        in_specs=[pl.BlockSpec((1, 1, 1, dk, NCOL), tile),
                  pl.BlockSpec((1, 1, npair, KBLK, dk), whole),
                  pl.BlockSpec((1, 1, npair, dk, KBLK), whole),
                  pl.BlockSpec((1, 1, npair, KBLK, dk), whole),
                  pl.BlockSpec((1, 1, npair, dk, KBLK), whole),
                  pl.BlockSpec((1, 1, 1, ns, QTILE), tile),
                  pl.BlockSpec((1, 1, 1, dk, NCOL), tile),
                  pl.BlockSpec((1, 1, 1, 3, NCOL), tile),
                  pl.BlockSpec((1, 1, NCOL), lambda b, gg, c, pf: (gg, 0, 0))],
        out_specs=pl.BlockSpec((1, 1, 1, dk, NCOL), tile),
        scratch_shapes=[pltpu.VMEM((1, NCOL), F32), pltpu.VMEM((1, NCOL), F32),
                        pltpu.VMEM((dk, NCOL), F32)],
    )
    return pl.pallas_call(
        _nsa_attn_kernel,
        grid_spec=grid_spec,
        out_shape=jax.ShapeDtypeStruct((bsz, g, nt, dk, NCOL), F32),
        compiler_params=_params("parallel", "parallel", "parallel"),
        name="nsa_attend",
    )(pair_flags, qt, ks, vs_t, kw, vw_t, sel_t, ocmp_t, gl_t, slopes)


def _nsa_mixer(x2, bsz, t, w_in, pe_k, pe_v, wk1, wk2, wv1, wv2):
    g, hpg, dk = NSA_KV_GROUPS, NSA_HPG, NSA_HEAD_DIM
    nt = t // QTILE
    n_in = w_in.shape[1]
    n_pad = -n_in % LANE
    w_in_p = jnp.pad(_mx(w_in), ((0, 0), (0, n_pad)))
    tn = (n_in + n_pad) // 3 if (n_in + n_pad) % (3 * LANE) == 0 else LANE
    proj = _matmul(x2, w_in_p, 512, tn)
    splits = np.cumsum([D_MODEL] + [NSA_KV_DIM] * 6).tolist()
    q, kc, vc, ks, vs, kw, vw, gl = jnp.split(proj[:, :n_in], splits, axis=-1)

    qt = (q * dk ** -0.5).reshape(bsz, nt, QTILE, g, hpg, dk).transpose(0, 3, 1, 5, 4, 2)
    qt = _mx(qt).reshape(bsz, g, nt, dk, NCOL)
    gl_t = gl.reshape(bsz, nt, QTILE, g, hpg, 3).transpose(0, 3, 1, 5, 4, 2).reshape(bsz, g, nt, 3, NCOL)

    def key_blocks(a):
        return _mx(a).reshape(bsz, t // KBLK, KBLK, g, dk).transpose(0, 3, 1, 2, 4)

    def val_blocks_t(a):
        return _mx(a).reshape(bsz, t // KBLK, KBLK, g, dk).transpose(0, 3, 1, 4, 2)

    four = lambda a: a.reshape(bsz, t, g, dk)
    k_cmp = _compress(four(kc), pe_k, wk1, wk2)
    v_cmp = _compress(four(vc), pe_v, wv1, wv2)
    kcmp = _mx(k_cmp).transpose(0, 2, 1, 3)
    vcmp_t = _mx(v_cmp).transpose(0, 2, 3, 1)

    ncp = t // CMP_STRIDE
    ns = t // SLC_LEN
    cs = np.arange(ncp) * CMP_STRIDE
    ce = cs + CMP_LEN - 1
    bs = np.arange(ns) * SLC_LEN
    be = bs + SLC_LEN - 1
    ovl_t = ((cs[None, :] <= be[:, None]) & (ce[None, :] >= bs[:, None])).astype(np.float32)
    ovl_t[:, ncp - 1] = 0.0
    ovl_t = jnp.asarray(ovl_t, MXU_DTYPE)

    slopes = 2.0 ** (-8.0 * (jnp.arange(NSA_HEADS, dtype=F32) + 1.0) / NSA_HEADS)
    slopes = jnp.broadcast_to(slopes.reshape(g, 1, hpg, 1), (g, 1, hpg, QTILE)).reshape(g, 1, NCOL)

    ocmp_t, sel_t, counts = _nsa_cmp_select(qt, kcmp, vcmp_t, ovl_t, slopes)
    picked = counts[:, :, :, 0, :] > 0.5
    pair_flags = (picked[..., 0::2] | picked[..., 1::2]).astype(jnp.int32).reshape(-1)

    o_t = _nsa_attend(pair_flags, qt, key_blocks(four(ks)), val_blocks_t(four(vs)),
                      key_blocks(four(kw)), val_blocks_t(four(vw)), sel_t, ocmp_t, gl_t, slopes)
    o = o_t.reshape(bsz, g, nt, dk, hpg, QTILE).transpose(0, 2, 5, 1, 4, 3)
    return o.reshape(bsz * t, NSA_HEADS * dk)


def _log_sigmoid(z):
    return jnp.minimum(z, 0.0) - jnp.log1p(jnp.exp(-jnp.abs(z)))


def _gla_kernel(q_ref, k_ref, v_ref, g_ref, al_ref, w2_ref, b2_ref, hn_ref, o_ref, st_sc):
    @pl.when(pl.program_id(2) == 0)
    def _():
        st_sc[...] = jnp.zeros(st_sc.shape, F32)

    cc = GLA_CHUNK
    r_i = lax.broadcasted_iota(jnp.int32, (cc, cc), 0)
    c_i = lax.broadcasted_iota(jnp.int32, (cc, cc), 1)
    causal = r_i >= c_i
    tri = jnp.where(causal, 1.0, 0.0).astype(MXU_DTYPE)
    ref = cc // 2

    def chunk(ci, carry):
        rows = pl.ds(pl.multiple_of(ci * cc, cc), cc)
        q = q_ref[0, rows, :] * GLA_DK ** -0.5
        k = k_ref[0, rows, :]
        v = v_ref[0, rows, :]
        z = _dot(_mx(al_ref[0, rows, :]), w2_ref[...]) + b2_ref[...]
        log_a = _log_sigmoid(z) / GLA_GATE_NORM
        a1 = _mx(log_a)
        r1 = log_a - a1.astype(F32)
        a2 = _mx(r1)
        a3 = _mx(r1 - a2.astype(F32))
        bc = _dot(tri, a1) + _dot(tri, a2) + _dot(tri, a3)
        b_ref = bc[ref:ref + 1, :]
        b_last = bc[cc - 1:cc, :]
        a = _dot_nt(_mx(q * jnp.exp(bc - b_ref)), _mx(k * jnp.exp(b_ref - bc)))
        a = jnp.where(causal, a, 0.0)
        st = st_sc[...]
        o = _dot(_mx(a), _mx(v)) + _dot_nt(_mx(q * jnp.exp(bc)), _mx(st))
        kl = k * jnp.exp(b_last - bc)
        st_sc[...] = st * jnp.exp(b_last) + _dot(_mx(v.T), _mx(kl))
        o = o * lax.rsqrt(jnp.mean(o * o, -1, keepdims=True) + RMS_EPS)
        gg = g_ref[0, rows, :]
        o_ref[0, rows, :] = o * hn_ref[...] * (gg * jax.nn.sigmoid(gg))
        return carry

    lax.fori_loop(0, q_ref.shape[1] // cc, chunk, 0)


def _gla_mixer(x2, bsz, t, w_in, w_gate2, b_gate2, head_norm_g):
    n_main = 2 * GLA_KEY_DIM + 2 * GLA_VAL_DIM
    w_main = _mx(w_in[:, :n_main])
    w_low = jnp.pad(_mx(w_in[:, n_main:]), ((0, 0), (0, LANE - GLA_GATE_RANK)))
    proj = _matmul(x2, w_main, 512, 1024).reshape(bsz, t, n_main)
    a_low = _matmul(x2, w_low, 512, LANE).reshape(bsz, t, LANE)
    w2p = jnp.pad(_mx(w_gate2), ((0, LANE - GLA_GATE_RANK), (0, 0)))
    ts = min(GLA_TSTEP, t)
    kb = GLA_KEY_DIM // GLA_DK
    vb = GLA_VAL_DIM // GLA_DV
    out = pl.pallas_call(
        _gla_kernel,
        grid=(bsz, GLA_HEADS, t // ts),
        in_specs=[pl.BlockSpec((1, ts, GLA_DK), lambda b, h, s: (b, s, h)),
                  pl.BlockSpec((1, ts, GLA_DK), lambda b, h, s: (b, s, kb + h)),
                  pl.BlockSpec((1, ts, GLA_DV), lambda b, h, s: (b, s, (2 * GLA_KEY_DIM) // GLA_DV + h)),
                  pl.BlockSpec((1, ts, GLA_DV), lambda b, h, s: (b, s, (2 * GLA_KEY_DIM) // GLA_DV + vb + h)),
                  pl.BlockSpec((1, ts, LANE), lambda b, h, s: (b, s, 0)),
                  pl.BlockSpec((LANE, GLA_DK), lambda b, h, s: (0, h)),
                  pl.BlockSpec((1, GLA_DK), lambda b, h, s: (0, h)),
                  pl.BlockSpec((1, GLA_DV), lambda b, h, s: (0, h))],
        out_specs=pl.BlockSpec((1, ts, GLA_DV), lambda b, h, s: (b, s, h)),
        out_shape=jax.ShapeDtypeStruct((bsz, t, GLA_VAL_DIM), F32),
        scratch_shapes=[pltpu.VMEM((GLA_DV, GLA_DK), F32)],
        compiler_params=_params("parallel", "parallel", "arbitrary"),
        name="gla",
    )(proj, proj, proj, proj, a_low, w2p, b_gate2.reshape(1, -1), head_norm_g.reshape(1, -1))
    return out.reshape(bsz * t, GLA_VAL_DIM)


def _router_kernel(x_ref, wh_ref, wl_ref, o_ref):
    x = x_ref[...]
    xh = _mx(x)
    xl = _mx(x - xh.astype(F32))
    wh = wh_ref[...]
    logits = _dot(xh, wh) + _dot(xl, wh) + _dot(xh, wl_ref[...])
    lane = lax.broadcasted_iota(jnp.int32, logits.shape, 1)
    lg = jnp.where(lane < N_EXPERTS, logits, -jnp.inf)
    m1 = jnp.max(lg, axis=1, keepdims=True)
    i1 = jnp.min(jnp.where(lg == m1, lane, LANE), axis=1, keepdims=True)
    lg2 = jnp.where(lane == i1, -jnp.inf, lg)
    m2 = jnp.max(lg2, axis=1, keepdims=True)
    i2 = jnp.min(jnp.where(lg2 == m2, lane, LANE), axis=1, keepdims=True)
    e2 = jnp.exp(m2 - m1)
    den = 1.0 + e2
    w1 = 1.0 / den
    w2 = e2 / den
    out = jnp.where(lane == N_EXPERTS, i1.astype(F32), 0.0)
    out = jnp.where(lane == N_EXPERTS + 1, i2.astype(F32), out)
    out = jnp.where(lane == N_EXPERTS + 2, w1, out)
    out = jnp.where(lane == N_EXPERTS + 3, w2, out)
    o_ref[...] = out


def _moe_up_kernel(te_ref, x_ref, wg_ref, wu_ref, o_ref):
    x = x_ref[...]
    a = _dot(x, wg_ref[0])
    b = _dot(x, wu_ref[0])
    o_ref[...] = (a * jax.nn.sigmoid(a) * b).astype(o_ref.dtype)


def _moe_down_kernel(te_ref, h_ref, w_ref, rw_ref, o_ref):
    o_ref[...] = _dot(h_ref[...], w_ref[0]) * rw_ref[...]


def _add_ln_kernel(x_ref, y0_ref, y1_ref, g_ref, b_ref, o_ref):
    h = DN_ALPHA * x_ref[...] + (y0_ref[...] + y1_ref[...])
    o_ref[...] = _layer_norm_rows(h, g_ref[...], b_ref[...])


def _moe_layer(x2, w_router, w_gu, w_down, ln_g, ln_b):
    m, d = x2.shape
    tm = MOE_TM
    wr = jnp.pad(w_router, ((0, 0), (0, LANE - N_EXPERTS)))
    wr_hi = _mx(wr)
    wr_lo = _mx(wr - wr_hi.astype(F32))
    rt = pl.pallas_call(
        _router_kernel,
        grid=(m // 512,),
        in_specs=[pl.BlockSpec((512, d), lambda i: (i, 0)),
                  pl.BlockSpec((d, LANE), lambda i: (0, 0)),
                  pl.BlockSpec((d, LANE), lambda i: (0, 0))],
        out_specs=pl.BlockSpec((512, LANE), lambda i: (i, 0)),
        out_shape=jax.ShapeDtypeStruct((m, LANE), F32),
        compiler_params=_params("parallel"),
        name="moe_router",
    )(x2, wr_hi, wr_lo)
    top_idx = rt[:, N_EXPERTS:N_EXPERTS + 2].astype(jnp.int32)
    top_w = rt[:, N_EXPERTS + 2:N_EXPERTS + 4]

    n_asg = m * TOP_K
    e_flat = top_idx.reshape(-1)
    order = jnp.argsort(e_flat, stable=True)
    counts = jnp.bincount(e_flat, length=N_EXPERTS)
    tiles_per = (counts + tm - 1) // tm
    tile_end = jnp.cumsum(tiles_per)
    row_start = (tile_end - tiles_per) * tm
    grp_start = jnp.cumsum(counts) - counts
    e_sorted = e_flat[order]
    dest = row_start[e_sorted] + (jnp.arange(n_asg) - grp_start[e_sorted])
    n_rows = n_asg + N_EXPERTS * tm
    n_tiles = n_rows // tm
    row_token = jnp.zeros((n_rows,), jnp.int32).at[dest].set((order // TOP_K).astype(jnp.int32))
    row_w = jnp.zeros((n_rows,), F32).at[dest].set(top_w.reshape(-1)[order])
    pos = jnp.zeros((n_asg,), jnp.int32).at[order].set(dest.astype(jnp.int32)).reshape(m, TOP_K)
    tile_expert = jnp.minimum(jnp.searchsorted(tile_end, jnp.arange(n_tiles), side="right"),
                              N_EXPERTS - 1).astype(jnp.int32)

    xs = _mx(x2)[row_token]
    f = w_gu.shape[2] // 2
    tf = 512
    nf = f // tf
    w_gu_m = _mx(w_gu)
    h = pl.pallas_call(
        _moe_up_kernel,
        grid_spec=pltpu.PrefetchScalarGridSpec(
            num_scalar_prefetch=1,
            grid=(nf, n_tiles),
            in_specs=[pl.BlockSpec((tm, d), lambda j, i, te: (i, 0)),
                      pl.BlockSpec((1, d, tf), lambda j, i, te: (te[i], 0, j)),
                      pl.BlockSpec((1, d, tf), lambda j, i, te: (te[i], 0, j + nf))],
            out_specs=pl.BlockSpec((tm, tf), lambda j, i, te: (i, j)),
        ),
        out_shape=jax.ShapeDtypeStruct((n_rows, f), MXU_DTYPE),
        compiler_params=_params("parallel", "arbitrary"),
        name="moe_up",
    )(tile_expert, xs, w_gu_m, w_gu_m)
    ys = pl.pallas_call(
        _moe_down_kernel,
        grid_spec=pltpu.PrefetchScalarGridSpec(
            num_scalar_prefetch=1,
            grid=(n_tiles,),
            in_specs=[pl.BlockSpec((tm, f), lambda i, te: (i, 0)),
                      pl.BlockSpec((1, f, d), lambda i, te: (te[i], 0, 0)),
                      pl.BlockSpec((tm, 1), lambda i, te: (i, 0))],
            out_specs=pl.BlockSpec((tm, d), lambda i, te: (i, 0)),
        ),
        out_shape=jax.ShapeDtypeStruct((n_rows, d), F32),
        compiler_params=_params("arbitrary"),
        name="moe_down",
    )(tile_expert, h, _mx(w_down), row_w.reshape(n_rows, 1))

    lo_first = top_idx[:, 0:1] < top_idx[:, 1:2]
    p0 = jnp.where(lo_first[:, 0], pos[:, 0], pos[:, 1])
    p1 = jnp.where(lo_first[:, 0], pos[:, 1], pos[:, 0])
    y0 = ys[p0]
    y1 = ys[p1]
    return pl.pallas_call(
        _add_ln_kernel,
        grid=(m // 512,),
        in_specs=[pl.BlockSpec((512, d), lambda i: (i, 0))] * 3
        + [pl.BlockSpec((1, d), lambda i: (0, 0))] * 2,
        out_specs=pl.BlockSpec((512, d), lambda i: (i, 0)),
        out_shape=jax.ShapeDtypeStruct((m, d), F32),
        compiler_params=_params("parallel"),
        name="moe_add_ln",
    )(x2, y0, y1, ln_g.reshape(1, d), ln_b.reshape(1, d))


def kernel(x, l0_w_in, l0_cmp_pe_k, l0_cmp_pe_v, l0_cmp_wk1, l0_cmp_wk2, l0_cmp_wv1, l0_cmp_wv2, l0_w_o, l0_ln1_g, l0_ln1_b, l0_ffn_w_gu, l0_ffn_w_down, l0_ln2_g, l0_ln2_b, l1_w_in, l1_w_gate2, l1_b_gate2, l1_head_norm_g, l1_w_o, l1_ln1_g, l1_ln1_b, l1_router, l1_moe_w_gu, l1_moe_w_down, l1_ln2_g, l1_ln2_b):
    bsz, t, d = x.shape
    x2 = x.reshape(bsz * t, d)

    o = _nsa_mixer(x2, bsz, t, l0_w_in, l0_cmp_pe_k, l0_cmp_pe_v, l0_cmp_wk1, l0_cmp_wk2,
                   l0_cmp_wv1, l0_cmp_wv2)
    x2 = _matmul_res_ln(o, _mx(l0_w_o), x2, l0_ln1_g, l0_ln1_b, 512)
    hmid = _swiglu_up(x2, _mx(l0_ffn_w_gu), 512, FFN_DENSE // 2)
    x2 = _matmul_res_ln(hmid, _mx(l0_ffn_w_down), x2, l0_ln2_g, l0_ln2_b, 512)

    o = _gla_mixer(x2, bsz, t, l1_w_in, l1_w_gate2, l1_b_gate2, l1_head_norm_g)
    x2 = _matmul_res_ln(o, _mx(l1_w_o), x2, l1_ln1_g, l1_ln1_b, 512)
    x2 = _moe_layer(x2, l1_router, l1_moe_w_gu, l1_moe_w_down, l1_ln2_g, l1_ln2_b)
    return x2.reshape(bsz, t, d)
```

```python
import functools

import numpy as np
import jax
import jax.numpy as jnp
from jax import lax
from jax.experimental import pallas as pl
from jax.experimental.pallas import tpu as pltpu

F32 = jnp.float32
MXU_DTYPE = jnp.bfloat16

D_MODEL = 1024
DEPTH = 2
DN_ALPHA = (2 * DEPTH) ** 0.25
LN_EPS = 1e-5
RMS_EPS = 1e-6

NSA_HEADS = 16
NSA_HEAD_DIM = 64
NSA_KV_GROUPS = 4
NSA_HPG = NSA_HEADS // NSA_KV_GROUPS
NSA_KV_DIM = NSA_KV_GROUPS * NSA_HEAD_DIM
CMP_LEN = 32
CMP_STRIDE = 16
CMP_HIDDEN = 256
SLC_LEN = 64
N_SEL = 16
WINDOW = 512
QTILE = 128
KBLK = 128
NCOL = NSA_HPG * QTILE
CMP_CHUNK = 128
POS_FEATS = 16
SEL_UNROLL = 4
LOG2E = 1.4426950408889634

GLA_HEADS = 4
GLA_KEY_DIM = D_MODEL // 2
GLA_VAL_DIM = D_MODEL
GLA_DK = GLA_KEY_DIM // GLA_HEADS
GLA_DV = GLA_VAL_DIM // GLA_HEADS
GLA_GATE_RANK = 16
GLA_GATE_NORM = 16.0
GLA_CHUNK = 64
GLA_TSTEP = 512

FFN_DENSE = 2816
N_EXPERTS = 8
TOP_K = 2
FFN_EXPERT = 3584
MOE_TM = 256
MOE_TF = 896

LANE = 128
NEG = -1e30
VMEM_LIMIT = 56 * 1024 * 1024


def _params(*sem):
    return pltpu.CompilerParams(dimension_semantics=sem, vmem_limit_bytes=VMEM_LIMIT)


def _mx(a):
    return a.astype(MXU_DTYPE)


def _dot(a, b):
    return jnp.dot(a, b, preferred_element_type=F32)


def _dot_nt(a, b):
    return lax.dot_general(a, b, (((1,), (1,)), ((), ())), preferred_element_type=F32)


def _mm_kernel(x_ref, w_ref, o_ref):
    o_ref[...] = _dot(_mx(x_ref[...]), w_ref[...]).astype(o_ref.dtype)


def _matmul(x, w, tm, tn, out_dtype=F32):
    m, k = x.shape
    n = w.shape[1]
    return pl.pallas_call(
        _mm_kernel,
        grid=(m // tm, n // tn),
        in_specs=[pl.BlockSpec((tm, k), lambda i, j: (i, 0)),
                  pl.BlockSpec((k, tn), lambda i, j: (0, j))],
        out_specs=pl.BlockSpec((tm, tn), lambda i, j: (i, j)),
        out_shape=jax.ShapeDtypeStruct((m, n), out_dtype),
        compiler_params=_params("parallel", "parallel"),
        name="matmul",
    )(x, w)


def _layer_norm_rows(h, g, b):
    mu = jnp.mean(h, -1, keepdims=True)
    d = h - mu
    var = jnp.mean(d * d, -1, keepdims=True)
    return d * lax.rsqrt(var + LN_EPS) * g + b


def _mm_ln_kernel(x_ref, w_ref, r_ref, g_ref, b_ref, o_ref):
    y = _dot(_mx(x_ref[...]), w_ref[...])
    o_ref[...] = _layer_norm_rows(DN_ALPHA * r_ref[...] + y, g_ref[...], b_ref[...])


def _matmul_res_ln(x, w, res, g, b, tm):
    m, k = x.shape
    n = w.shape[1]
    return pl.pallas_call(
        _mm_ln_kernel,
        grid=(m // tm,),
        in_specs=[pl.BlockSpec((tm, k), lambda i: (i, 0)),
                  pl.BlockSpec((k, n), lambda i: (0, 0)),
                  pl.BlockSpec((tm, n), lambda i: (i, 0)),
                  pl.BlockSpec((1, n), lambda i: (0, 0)),
                  pl.BlockSpec((1, n), lambda i: (0, 0))],
        out_specs=pl.BlockSpec((tm, n), lambda i: (i, 0)),
        out_shape=jax.ShapeDtypeStruct((m, n), F32),
        compiler_params=_params("parallel"),
        name="matmul_res_ln",
    )(x, w, res, g.reshape(1, n), b.reshape(1, n))


def _swiglu_up_kernel(x_ref, wg_ref, wu_ref, o_ref):
    x = _mx(x_ref[...])
    a = _dot(x, wg_ref[...])
    b = _dot(x, wu_ref[...])
    o_ref[...] = (a * jax.nn.sigmoid(a) * b).astype(o_ref.dtype)


def _swiglu_up(x, w_gu, tm, tn):
    m, k = x.shape
    f = w_gu.shape[1] // 2
    nj = f // tn
    return pl.pallas_call(
        _swiglu_up_kernel,
        grid=(m // tm, nj),
        in_specs=[pl.BlockSpec((tm, k), lambda i, j: (i, 0)),
                  pl.BlockSpec((k, tn), lambda i, j: (0, j)),
                  pl.BlockSpec((k, tn), lambda i, j: (0, j + nj))],
        out_specs=pl.BlockSpec((tm, tn), lambda i, j: (i, j)),
        out_shape=jax.ShapeDtypeStruct((m, f), MXU_DTYPE),
        compiler_params=_params("parallel", "parallel"),
        name="swiglu_up",
    )(x, w_gu, w_gu)


def _cmp_up_kernel(c_ref, pea_ref, peb_ref, wa_ref, wb_ref, p_ref, q_ref):
    c = c_ref[...]
    p_ref[...] = _dot(_mx(c + pea_ref[...]), wa_ref[...])
    q_ref[...] = _dot(_mx(c + peb_ref[...]), wb_ref[...])


def _cmp_down_kernel(p_ref, q_ref, w_ref, o_ref):
    h = jax.nn.gelu(p_ref[...] + q_ref[...])
    o_ref[...] = _dot(_mx(h), w_ref[...])


def _compress(a, pe, w1, w2):
    bsz, t, g, dk = a.shape
    nch = t // CMP_STRIDE
    half = CMP_STRIDE * dk
    chunks = a.reshape(bsz, nch, CMP_STRIDE, g, dk).transpose(0, 1, 3, 2, 4).reshape(bsz * nch * g, half)
    rows = chunks.shape[0]
    tm = min(512, rows)
    pe_a = pe[:CMP_STRIDE].reshape(1, half)
    pe_b = pe[CMP_STRIDE:].reshape(1, half)
    w1m = _mx(w1)
    p, q = pl.pallas_call(
        _cmp_up_kernel,
        grid=(rows // tm,),
        in_specs=[pl.BlockSpec((tm, half), lambda i: (i, 0)),
                  pl.BlockSpec((1, half), lambda i: (0, 0)),
                  pl.BlockSpec((1, half), lambda i: (0, 0)),
                  pl.BlockSpec((half, CMP_HIDDEN), lambda i: (0, 0)),
                  pl.BlockSpec((half, CMP_HIDDEN), lambda i: (0, 0))],
        out_specs=[pl.BlockSpec((tm, CMP_HIDDEN), lambda i: (i, 0)),
                   pl.BlockSpec((tm, CMP_HIDDEN), lambda i: (i, 0))],
        out_shape=[jax.ShapeDtypeStruct((rows, CMP_HIDDEN), F32)] * 2,
        compiler_params=_params("parallel"),
        name="cmp_up",
    )(chunks, pe_a, pe_b, w1m[:half], w1m[half:])
    q = q.reshape(bsz, nch, g, CMP_HIDDEN)
    q = jnp.concatenate([q[:, 1:], jnp.zeros_like(q[:, :1])], axis=1).reshape(rows, CMP_HIDDEN)
    w2p = jnp.pad(_mx(w2), ((0, 0), (0, LANE - dk)))
    out = pl.pallas_call(
        _cmp_down_kernel,
        grid=(rows // tm,),
        in_specs=[pl.BlockSpec((tm, CMP_HIDDEN), lambda i: (i, 0)),
                  pl.BlockSpec((tm, CMP_HIDDEN), lambda i: (i, 0)),
                  pl.BlockSpec((CMP_HIDDEN, LANE), lambda i: (0, 0))],
        out_specs=pl.BlockSpec((tm, LANE), lambda i: (i, 0)),
        out_shape=jax.ShapeDtypeStruct((rows, LANE), F32),
        compiler_params=_params("parallel"),
        name="cmp_down",
    )(p, q, w2p)
    out = out[:, :dk].reshape(bsz, nch, g, dk)
    valid = (jnp.arange(nch) < nch - 1)[None, :, None, None]
    return jnp.where(valid, out, 0.0)


def _nsa_cmp_kernel(qt_ref, kc_ref, vct_ref, ovl_ref, slope_ref, ocmp_ref, sel_ref, flag_ref,
                    imp_ref, key_ref, cnt_ref):
    c = pl.program_id(2)
    ncp = kc_ref.shape[2]
    ns = ovl_ref.shape[0]
    qt = qt_ref[0, 0, 0, :kc_ref.shape[3], :]
    chunk = min(CMP_CHUNK, ncp)
    n_chunks = (c * (QTILE // CMP_STRIDE) + (QTILE - CMP_LEN) // CMP_STRIDE + chunk) // chunk

    def cmp_branch(rows):
        s = _dot(kc_ref[0, 0, :rows, :], qt)
        n_i = lax.broadcasted_iota(jnp.int32, (rows, NCOL), 0)
        col = lax.broadcasted_iota(jnp.int32, (rows, NCOL), 1)
        tq = c * QTILE + (col & (QTILE - 1))
        dist = tq - (n_i * CMP_STRIDE + (CMP_LEN - 1))
        s = jnp.where(dist >= 0, s - slope_ref[0] * dist.astype(F32), NEG)
        m = jnp.max(s, axis=0, keepdims=True)
        e = jnp.exp2(s - m)
        l = jnp.sum(e, axis=0, keepdims=True)
        r = jnp.where(tq[0:1, :] >= CMP_LEN - 1, 1.0 / jnp.maximum(l, 1e-30), 0.0)
        p = e * r
        ocmp_ref[0, 0, 0] = _dot(vct_ref[0, 0, :, :rows], _mx(p))
        psum = p[:, 0:QTILE]
        for h in range(1, NSA_HPG):
            psum = psum + p[:, h * QTILE:(h + 1) * QTILE]
        hi = _mx(psum)
        lo = _mx(psum - hi.astype(F32))
        ovl = ovl_ref[:, :rows]
        imp_ref[...] = _dot(ovl, hi) + _dot(ovl, lo)

    for k in range(1, ncp // chunk + 1):
        pl.when(n_chunks == k)(functools.partial(cmp_branch, k * chunk))

    j_i = lax.broadcasted_iota(jnp.int32, (ns, QTILE), 0)
    tq2 = c * QTILE + lax.broadcasted_iota(jnp.int32, (ns, QTILE), 1)
    valid = j_i * SLC_LEN <= tq2
    cur = tq2 >> (SLC_LEN.bit_length() - 1)
    forced = valid & ((j_i == 0) | (j_i == cur) | (j_i == cur - 1))
    score = jnp.where(forced, 1e9, jnp.where(valid, imp_ref[...], -1e9))
    bits = lax.bitcast_convert_type(score, jnp.int32)
    key_ref[...] = jnp.where(bits < 0, bits ^ 0x7FFFFFFF, bits)
    cnt_ref[...] = jnp.zeros(cnt_ref.shape, jnp.int32)

    sub = lax.broadcasted_iota(jnp.int32, (8, QTILE), 0)
    n_it = (c * (QTILE // SLC_LEN) + QTILE // SLC_LEN + 7) // 8

    def it_body(it, carry):
        base = pl.multiple_of(it * 8, 8)
        rows8 = key_ref[pl.ds(base, 8), :]
        rb = [jnp.broadcast_to(rows8[il:il + 1, :], (8, QTILE)) for il in range(8)]
        for jt in range(ns // 8):
            js = slice(jt * 8, (jt + 1) * 8)
            thr = key_ref[js, :] - jnp.where(jt > it, 1, 0)
            acc = cnt_ref[js, :]
            for il in range(8):
                acc = acc + jnp.where(rb[il] > thr, 1, 0)
            cnt_ref[js, :] = acc
        corr = jnp.zeros((8, QTILE), jnp.int32)
        for il in range(8):
            corr = corr + jnp.where((rb[il] == rows8) & (sub > il), 1, 0)
        cnt_ref[pl.ds(base, 8), :] += corr
        return carry

    lax.fori_loop(0, n_it, it_body, 0)
    sel = jnp.where(cnt_ref[...] < min(N_SEL, ns), 1.0, 0.0)
    sel_ref[0, 0, 0] = sel
    flag_ref[0, 0, 0] = _dot_nt(jnp.ones((8, QTILE), MXU_DTYPE), _mx(sel))


def _nsa_cmp_select(qt, kcmp, vcmp_t, ovl_t, slopes):
    bsz, g, nt, ka, _ = qt.shape
    ncp, dk = kcmp.shape[2:]
    ns = ovl_t.shape[0]
    return pl.pallas_call(
        _nsa_cmp_kernel,
        grid=(bsz, g, nt),
        in_specs=[pl.BlockSpec((1, 1, 1, ka, NCOL), lambda b, gg, c: (b, gg, c, 0, 0)),
                  pl.BlockSpec((1, 1, ncp, dk), lambda b, gg, c: (b, gg, 0, 0)),
                  pl.BlockSpec((1, 1, dk, ncp), lambda b, gg, c: (b, gg, 0, 0)),
                  pl.BlockSpec((ns, ncp), lambda b, gg, c: (0, 0)),
                  pl.BlockSpec((1, 1, NCOL), lambda b, gg, c: (gg, 0, 0))],
        out_specs=[pl.BlockSpec((1, 1, 1, dk, NCOL), lambda b, gg, c: (b, gg, c, 0, 0)),
                   pl.BlockSpec((1, 1, 1, ns, QTILE), lambda b, gg, c: (b, gg, c, 0, 0)),
                   pl.BlockSpec((1, 1, 1, 8, ns), lambda b, gg, c: (b, gg, c, 0, 0))],
        out_shape=[jax.ShapeDtypeStruct((bsz, g, nt, dk, NCOL), F32),
                   jax.ShapeDtypeStruct((bsz, g, nt, ns, QTILE), F32),
                   jax.ShapeDtypeStruct((bsz, g, nt, 8, ns), F32)],
        scratch_shapes=[pltpu.VMEM((ns, QTILE), F32), pltpu.VMEM((ns, QTILE), jnp.int32),
                        pltpu.VMEM((ns, QTILE), jnp.int32)],
        compiler_params=_params("parallel", "parallel", "parallel"),
        name="nsa_cmp_select",
    )(qt, kcmp, vcmp_t, ovl_t, slopes)


def _nsa_attn_kernel(lst_ref, cnt_ref, qt_ref, ks_ref, vst_ref, kw_ref, vwt_ref, sel_ref, ocmp_ref, gl_ref,
                     o_ref, m_sc, l_sc, acc_sc):
    b = pl.program_id(0)
    g = pl.program_id(1)
    c = pl.program_id(2)
    nt = pl.num_programs(2)
    npair = ks_ref.shape[2]
    dk = acc_sc.shape[0]
    qt = qt_ref[0, 0, 0]
    key_i = lax.broadcasted_iota(jnp.int32, (KBLK, QTILE), 0)
    tq = c * QTILE + lax.broadcasted_iota(jnp.int32, (KBLK, QTILE), 1)
    heads = [slice(h * QTILE, (h + 1) * QTILE) for h in range(NSA_HPG)]

    m_sc[...] = jnp.full(m_sc.shape, NEG, F32)
    l_sc[...] = jnp.zeros(l_sc.shape, F32)
    acc_sc[...] = jnp.zeros(acc_sc.shape, F32)
    step = (b * NSA_KV_GROUPS + g) * nt + c

    def group_body(r, carry):
        blocks, scores, masks = [], [], []
        for u in range(SEL_UNROLL):
            jj = lst_ref[step * npair + r * SEL_UNROLL + u]
            js = jnp.maximum(jj, 0)
            kbase = jnp.where(jj >= 0, js * KBLK, nt * QTILE)
            rows = sel_ref[0, 0, 0, pl.ds(2 * js, 2), :]
            picked = jnp.where(key_i < SLC_LEN, rows[0:1, :], rows[1:2, :]) > 0.5
            blocks.append(js)
            masks.append(picked & (kbase + key_i <= tq))
            scores.append(_dot(ks_ref[0, 0, js], qt))
        es = [[] for _ in range(SEL_UNROLL)]
        for cs in heads:
            sh = [jnp.where(masks[u], scores[u][:, cs], NEG) for u in range(SEL_UNROLL)]
            m_old = m_sc[:, cs]
            m_new = m_old
            for u in range(SEL_UNROLL):
                m_new = jnp.maximum(m_new, jnp.max(sh[u], axis=0, keepdims=True))
            alpha = jnp.exp2(m_old - m_new)
            l_new = alpha * l_sc[:, cs]
            for u in range(SEL_UNROLL):
                e = jnp.exp2(sh[u] - m_new)
                l_new = l_new + jnp.sum(e, axis=0, keepdims=True)
                es[u].append(_mx(e))
            l_sc[:, cs] = l_new
            acc_sc[:, cs] = alpha * acc_sc[:, cs]
            m_sc[:, cs] = m_new
        upd = _dot(vst_ref[0, 0, blocks[0]], jnp.concatenate(es[0], axis=1))
        for u in range(1, SEL_UNROLL):
            upd = upd + _dot(vst_ref[0, 0, blocks[u]], jnp.concatenate(es[u], axis=1))
        acc_sc[...] += upd
        return carry

    lax.fori_loop(0, (cnt_ref[step] + SEL_UNROLL - 1) // SEL_UNROLL, group_body, 0)
    o_sel = acc_sc[...] * (1.0 / jnp.maximum(l_sc[...], 1e-30))

    nwin = WINDOW // KBLK + 1
    kwin = kw_ref[0, 0, pl.ds(c, nwin)].reshape(nwin * KBLK, qt.shape[0])
    s = _dot(kwin, qt)
    row = lax.broadcasted_iota(jnp.int32, (nwin * KBLK, QTILE), 0)
    dist = lax.broadcasted_iota(jnp.int32, (nwin * KBLK, QTILE), 1) + WINDOW - row
    mask = (dist >= 0) & (dist < WINDOW) & (row >= (nwin - 1 - c) * KBLK)
    es, rs = [], []
    for cs in heads:
        sh = jnp.where(mask, s[:, cs], NEG)
        e = jnp.exp2(sh - jnp.max(sh, axis=0, keepdims=True))
        rs.append(1.0 / jnp.maximum(jnp.sum(e, axis=0, keepdims=True), 1e-30))
        es.append(_mx(e))
    e_all = jnp.concatenate(es, axis=1)
    acc = jnp.zeros((dk, NCOL), F32)
    for t in range(nwin):
        acc = acc + _dot(vwt_ref[0, 0, c + t], e_all[t * KBLK:(t + 1) * KBLK, :])
    o_win = acc * jnp.concatenate(rs, axis=1)

    gates = jax.nn.sigmoid(gl_ref[0, 0, 0])
    o_ref[0, 0, 0] = gates[0:1] * ocmp_ref[0, 0, 0] + gates[1:2] * o_sel + gates[2:3] * o_win


def _nsa_attend(blk_list, blk_count, qt, ks, vs_t, kw, vw_t, sel_t, ocmp_t, gl_t):
    bsz, g, nt, ka, _ = qt.shape
    npair = ks.shape[2]
    nwpad = kw.shape[2]
    dk = vs_t.shape[3]
    ns = sel_t.shape[3]
    tile = lambda b, gg, c, lst, cnt: (b, gg, c, 0, 0)
    whole = lambda b, gg, c, lst, cnt: (b, gg, 0, 0, 0)
    grid_spec = pltpu.PrefetchScalarGridSpec(
        num_scalar_prefetch=2,
        grid=(bsz, g, nt),
        in_specs=[pl.BlockSpec((1, 1, 1, ka, NCOL), tile),
                  pl.BlockSpec((1, 1, npair, KBLK, ka), whole),
                  pl.BlockSpec((1, 1, npair, dk, KBLK), whole),
                  pl.BlockSpec((1, 1, nwpad, KBLK, ka), whole),
                  pl.BlockSpec((1, 1, nwpad, dk, KBLK), whole),
                  pl.BlockSpec((1, 1, 1, ns, QTILE), tile),
                  pl.BlockSpec((1, 1, 1, dk, NCOL), tile),
                  pl.BlockSpec((1, 1, 1, 3, NCOL), tile)],
        out_specs=pl.BlockSpec((1, 1, 1, dk, NCOL), tile),
        scratch_shapes=[pltpu.VMEM((1, NCOL), F32), pltpu.VMEM((1, NCOL), F32),
                        pltpu.VMEM((dk, NCOL), F32)],
    )
    return pl.pallas_call(
        _nsa_attn_kernel,
        grid_spec=grid_spec,
        out_shape=jax.ShapeDtypeStruct((bsz, g, nt, dk, NCOL), F32),
        compiler_params=_params("parallel", "parallel", "parallel"),
        name="nsa_attend",
    )(blk_list, blk_count, qt, ks, vs_t, kw, vw_t, sel_t, ocmp_t, gl_t)


def _nsa_mixer(x2, bsz, t, w_in, pe_k, pe_v, wk1, wk2, wv1, wv2):
    g, hpg, dk = NSA_KV_GROUPS, NSA_HPG, NSA_HEAD_DIM
    nt = t // QTILE
    n_in = w_in.shape[1]
    n_pad = -n_in % LANE
    w_in_p = jnp.pad(_mx(w_in), ((0, 0), (0, n_pad)))
    tn = (n_in + n_pad) // 3 if (n_in + n_pad) % (3 * LANE) == 0 else LANE
    proj = _matmul(x2, w_in_p, 512, tn)
    splits = np.cumsum([D_MODEL] + [NSA_KV_DIM] * 6).tolist()
    q, kc, vc, ks, vs, kw, vw, gl = jnp.split(proj[:, :n_in], splits, axis=-1)

    slopes = 2.0 ** (-8.0 * (jnp.arange(NSA_HEADS, dtype=F32) + 1.0) / NSA_HEADS) * LOG2E
    slopes = jnp.broadcast_to(slopes.reshape(g, 1, hpg, 1), (g, 1, hpg, QTILE)).reshape(g, 1, NCOL)
    s1 = slopes.astype(jnp.bfloat16)
    s2 = (slopes - s1.astype(F32)).astype(jnp.bfloat16)
    s3 = (slopes - s1.astype(F32) - s2.astype(F32)).astype(jnp.bfloat16)
    q_feat = jnp.concatenate([s1 * SLC_LEN, s2 * SLC_LEN, s3 * SLC_LEN, s1, s2, s3]
                             + [jnp.zeros_like(s1)] * (POS_FEATS - 6), axis=1)
    kpos = jnp.arange(t, dtype=jnp.int32)
    ka_, kb_ = (kpos // SLC_LEN).astype(jnp.bfloat16), (kpos % SLC_LEN).astype(jnp.bfloat16)
    k_feat = jnp.stack([ka_, ka_, ka_, kb_, kb_, kb_] + [jnp.zeros_like(ka_)] * (POS_FEATS - 6), axis=1)
    k_feat = jnp.broadcast_to(k_feat.reshape(1, 1, t // KBLK, KBLK, POS_FEATS),
                              (bsz, g, t // KBLK, KBLK, POS_FEATS)).astype(MXU_DTYPE)

    qt = (q * (dk ** -0.5 * LOG2E)).reshape(bsz, nt, QTILE, g, hpg, dk).transpose(0, 3, 1, 5, 4, 2)
    qt = _mx(qt).reshape(bsz, g, nt, dk, NCOL)
    qt = jnp.concatenate([qt, jnp.broadcast_to(q_feat.astype(MXU_DTYPE)[None, :, None],
                                               (bsz, g, nt, POS_FEATS, NCOL))], axis=3)
    gl_t = gl.reshape(bsz, nt, QTILE, g, hpg, 3).transpose(0, 3, 1, 5, 4, 2).reshape(bsz, g, nt, 3, NCOL)

    def key_blocks(a):
        kb = _mx(a).reshape(bsz, t // KBLK, KBLK, g, dk).transpose(0, 3, 1, 2, 4)
        return jnp.concatenate([kb, k_feat], axis=-1)

    def val_blocks_t(a):
        return _mx(a).reshape(bsz, t // KBLK, KBLK, g, dk).transpose(0, 3, 1, 4, 2)

    def front_pad(a):
        return jnp.pad(a, ((0, 0), (0, 0), (WINDOW // KBLK, 0), (0, 0), (0, 0)))

    four = lambda a: a.reshape(bsz, t, g, dk)
    k_cmp = _compress(four(kc), pe_k, wk1, wk2)
    v_cmp = _compress(four(vc), pe_v, wv1, wv2)
    kcmp = _mx(k_cmp).transpose(0, 2, 1, 3)
    vcmp_t = _mx(v_cmp).transpose(0, 2, 3, 1)

    ncp = t // CMP_STRIDE
    ns = t // SLC_LEN
    cs = np.arange(ncp) * CMP_STRIDE
    ce = cs + CMP_LEN - 1
    bs = np.arange(ns) * SLC_LEN
    be = bs + SLC_LEN - 1
    ovl_t = ((cs[None, :] <= be[:, None]) & (ce[None, :] >= bs[:, None])).astype(np.float32)
    ovl_t[:, ncp - 1] = 0.0
    ovl_t = jnp.asarray(ovl_t, MXU_DTYPE)

    ocmp_t, sel_t, counts = _nsa_cmp_select(qt, kcmp, vcmp_t, ovl_t, slopes)
    npair = t // KBLK
    picked = counts[:, :, :, 0, :] > 0.5
    jj = jnp.arange(npair)
    own = jj[None, :] == jnp.arange(nt)[:, None]
    active = ((picked[..., 0::2] | picked[..., 1::2]) & (jj[None, :] < jnp.arange(nt)[:, None])) | own
    blk_count = jnp.sum(active, axis=-1).astype(jnp.int32)
    rank = jnp.cumsum(active, axis=-1) - 1
    hit = active[..., None, :] & (rank[..., None, :] == jj[:, None])
    blk_list = jnp.sum(jnp.where(hit, jj, 0), axis=-1)
    blk_list = jnp.where(jj < blk_count[..., None], blk_list, -1).astype(jnp.int32)

    o_t = _nsa_attend(blk_list.reshape(-1), blk_count.reshape(-1), qt,
                      key_blocks(four(ks)), val_blocks_t(four(vs)),
                      front_pad(key_blocks(four(kw))), front_pad(val_blocks_t(four(vw))),
                      sel_t, ocmp_t, gl_t)
    o = o_t.reshape(bsz, g, nt, dk, hpg, QTILE).transpose(0, 2, 5, 1, 4, 3)
    return o.reshape(bsz * t, NSA_HEADS * dk)


def _log_sigmoid(z):
    return jnp.minimum(z, 0.0) - jnp.log1p(jnp.exp(-jnp.abs(z)))


def _gla_kernel(q_ref, k_ref, v_ref, g_ref, al_ref, w2_ref, b2_ref, hn_ref, o_ref, st_sc):
    @pl.when(pl.program_id(2) == 0)
    def _():
        st_sc[...] = jnp.zeros(st_sc.shape, F32)

    cc = GLA_CHUNK
    r_i = lax.broadcasted_iota(jnp.int32, (cc, cc), 0)
    c_i = lax.broadcasted_iota(jnp.int32, (cc, cc), 1)
    causal = r_i >= c_i
    tri = jnp.where(causal, 1.0, 0.0).astype(MXU_DTYPE)
    ref = cc // 2

    def chunk(ci, carry):
        rows = pl.ds(pl.multiple_of(ci * cc, cc), cc)
        q = q_ref[0, rows, :] * GLA_DK ** -0.5
        k = k_ref[0, rows, :]
        v = v_ref[0, rows, :]
        z = _dot(_mx(al_ref[0, rows, :]), w2_ref[...]) + b2_ref[...]
        log_a = _log_sigmoid(z) / GLA_GATE_NORM
        a1 = _mx(log_a)
        r1 = log_a - a1.astype(F32)
        a2 = _mx(r1)
        a3 = _mx(r1 - a2.astype(F32))
        bc = _dot(tri, a1) + _dot(tri, a2) + _dot(tri, a3)
        b_ref = bc[ref:ref + 1, :]
        b_last = bc[cc - 1:cc, :]
        a = _dot_nt(_mx(q * jnp.exp(bc - b_ref)), _mx(k * jnp.exp(b_ref - bc)))
        a = jnp.where(causal, a, 0.0)
        st = st_sc[...]
        o = _dot(_mx(a), _mx(v)) + _dot_nt(_mx(q * jnp.exp(bc)), _mx(st))
        kl = k * jnp.exp(b_last - bc)
        st_sc[...] = st * jnp.exp(b_last) + _dot(_mx(v.T), _mx(kl))
        o = o * lax.rsqrt(jnp.mean(o * o, -1, keepdims=True) + RMS_EPS)
        gg = g_ref[0, rows, :]
        o_ref[0, rows, :] = o * hn_ref[...] * (gg * jax.nn.sigmoid(gg))
        return carry

    lax.fori_loop(0, q_ref.shape[1] // cc, chunk, 0)


def _gla_mixer(x2, bsz, t, w_in, w_gate2, b_gate2, head_norm_g):
    n_main = 2 * GLA_KEY_DIM + 2 * GLA_VAL_DIM
    w_main = _mx(w_in[:, :n_main])
    w_low = jnp.pad(_mx(w_in[:, n_main:]), ((0, 0), (0, LANE - GLA_GATE_RANK)))
    proj = _matmul(x2, w_main, 512, 1024).reshape(bsz, t, n_main)
    a_low = _matmul(x2, w_low, 512, LANE).reshape(bsz, t, LANE)
    w2p = jnp.pad(_mx(w_gate2), ((0, LANE - GLA_GATE_RANK), (0, 0)))
    ts = min(GLA_TSTEP, t)
    kb = GLA_KEY_DIM // GLA_DK
    vb = GLA_VAL_DIM // GLA_DV
    out = pl.pallas_call(
        _gla_kernel,
        grid=(bsz, GLA_HEADS, t // ts),
        in_specs=[pl.BlockSpec((1, ts, GLA_DK), lambda b, h, s: (b, s, h)),
                  pl.BlockSpec((1, ts, GLA_DK), lambda b, h, s: (b, s, kb + h)),
                  pl.BlockSpec((1, ts, GLA_DV), lambda b, h, s: (b, s, (2 * GLA_KEY_DIM) // GLA_DV + h)),
                  pl.BlockSpec((1, ts, GLA_DV), lambda b, h, s: (b, s, (2 * GLA_KEY_DIM) // GLA_DV + vb + h)),
                  pl.BlockSpec((1, ts, LANE), lambda b, h, s: (b, s, 0)),
                  pl.BlockSpec((LANE, GLA_DK), lambda b, h, s: (0, h)),
                  pl.BlockSpec((1, GLA_DK), lambda b, h, s: (0, h)),
                  pl.BlockSpec((1, GLA_DV), lambda b, h, s: (0, h))],
        out_specs=pl.BlockSpec((1, ts, GLA_DV), lambda b, h, s: (b, s, h)),
        out_shape=jax.ShapeDtypeStruct((bsz, t, GLA_VAL_DIM), F32),
        scratch_shapes=[pltpu.VMEM((GLA_DV, GLA_DK), F32)],
        compiler_params=_params("parallel", "parallel", "arbitrary"),
        name="gla",
    )(proj, proj, proj, proj, a_low, w2p, b_gate2.reshape(1, -1), head_norm_g.reshape(1, -1))
    return out.reshape(bsz * t, GLA_VAL_DIM)


def _router_kernel(x_ref, wh_ref, wl_ref, o_ref):
    x = x_ref[...]
    xh = _mx(x)
    xl = _mx(x - xh.astype(F32))
    wh = wh_ref[...]
    logits = _dot(xh, wh) + _dot(xl, wh) + _dot(xh, wl_ref[...])
    lane = lax.broadcasted_iota(jnp.int32, logits.shape, 1)
    lg = jnp.where(lane < N_EXPERTS, logits, -jnp.inf)
    m1 = jnp.max(lg, axis=1, keepdims=True)
    i1 = jnp.min(jnp.where(lg == m1, lane, LANE), axis=1, keepdims=True)
    lg2 = jnp.where(lane == i1, -jnp.inf, lg)
    m2 = jnp.max(lg2, axis=1, keepdims=True)
    i2 = jnp.min(jnp.where(lg2 == m2, lane, LANE), axis=1, keepdims=True)
    e2 = jnp.exp(m2 - m1)
    den = 1.0 + e2
    w1 = 1.0 / den
    w2 = e2 / den
    out = jnp.where(lane == N_EXPERTS, i1.astype(F32), 0.0)
    out = jnp.where(lane == N_EXPERTS + 1, i2.astype(F32), out)
    out = jnp.where(lane == N_EXPERTS + 2, w1, out)
    out = jnp.where(lane == N_EXPERTS + 3, w2, out)
    o_ref[...] = out


def _moe_up_kernel(te_ref, x_ref, wg_ref, wu_ref, o_ref, wg_sc, wu_sc):
    i = pl.program_id(1)

    @pl.when((i == 0) | (te_ref[i] != te_ref[jnp.maximum(i - 1, 0)]))
    def _():
        wg_sc[...] = _mx(wg_ref[0])
        wu_sc[...] = _mx(wu_ref[0])

    x = x_ref[...]
    a = _dot(x, wg_sc[...])
    b = _dot(x, wu_sc[...])
    o_ref[...] = (a * jax.nn.sigmoid(a) * b).astype(o_ref.dtype)


def _moe_down_kernel(te_ref, h_ref, w_ref, rw_ref, o_ref):
    o_ref[...] = _dot(h_ref[...], w_ref[0]) * rw_ref[...]


def _add_ln_kernel(x_ref, y0_ref, y1_ref, g_ref, b_ref, o_ref):
    h = DN_ALPHA * x_ref[...] + (y0_ref[...] + y1_ref[...])
    o_ref[...] = _layer_norm_rows(h, g_ref[...], b_ref[...])


def _take_rows(a, idx):
    if a.dtype.itemsize != 2:
        return a[idx]
    rows, cols = a.shape
    words = lax.bitcast_convert_type(a.reshape(rows, cols // 2, 2), jnp.uint32)
    return lax.bitcast_convert_type(words[idx], a.dtype).reshape(idx.shape[0], cols)


def _moe_layer(x2, w_router, w_gu, w_down, ln_g, ln_b):
    m, d = x2.shape
    tm = MOE_TM
    wr = jnp.pad(w_router, ((0, 0), (0, LANE - N_EXPERTS)))
    wr_hi = _mx(wr)
    wr_lo = _mx(wr - wr_hi.astype(F32))
    rt = pl.pallas_call(
        _router_kernel,
        grid=(m // 512,),
        in_specs=[pl.BlockSpec((512, d), lambda i: (i, 0)),
                  pl.BlockSpec((d, LANE), lambda i: (0, 0)),
                  pl.BlockSpec((d, LANE), lambda i: (0, 0))],
        out_specs=pl.BlockSpec((512, LANE), lambda i: (i, 0)),
        out_shape=jax.ShapeDtypeStruct((m, LANE), F32),
        compiler_params=_params("parallel"),
        name="moe_router",
    )(x2, wr_hi, wr_lo)
    top_idx = rt[:, N_EXPERTS:N_EXPERTS + 2].astype(jnp.int32)
    top_w = rt[:, N_EXPERTS + 2:N_EXPERTS + 4]

    n_asg = m * TOP_K
    n_rows = n_asg + N_EXPERTS * tm
    n_tiles = n_rows // tm
    e_flat = top_idx.reshape(-1)
    order = jnp.argsort(e_flat, stable=True).astype(jnp.int32)
    slot = jnp.argsort(order).astype(jnp.int32)
    counts = jnp.sum((e_flat[:, None] == jnp.arange(N_EXPERTS)[None, :]).astype(jnp.int32), axis=0)
    tiles_per = (counts + tm - 1) // tm
    tile_end = jnp.cumsum(tiles_per)
    row_start = (tile_end - tiles_per) * tm
    grp_start = jnp.cumsum(counts) - counts
    tile_expert = jnp.minimum(jnp.sum((jnp.arange(n_tiles)[:, None] >= tile_end[None, :]).astype(jnp.int32),
                                      axis=1), N_EXPERTS - 1).astype(jnp.int32)
    pos = (row_start[e_flat] + slot - grp_start[e_flat]).astype(jnp.int32).reshape(m, TOP_K)
    row_e = jnp.repeat(tile_expert, tm)
    row_off = jnp.arange(n_rows, dtype=jnp.int32) - row_start[row_e].astype(jnp.int32)
    row_live = row_off < counts[row_e]
    row_asg = order[jnp.clip(grp_start[row_e].astype(jnp.int32) + row_off, 0, n_asg - 1)]
    row_token = jnp.where(row_live, row_asg // TOP_K, 0)
    row_w = jnp.where(row_live, top_w.reshape(-1)[row_asg], 0.0)

    xs = _take_rows(_mx(x2), row_token)
    f = w_gu.shape[2] // 2
    tf = MOE_TF
    nf = f // tf
    h = pl.pallas_call(
        _moe_up_kernel,
        grid_spec=pltpu.PrefetchScalarGridSpec(
            num_scalar_prefetch=1,
            grid=(nf, n_tiles),
            in_specs=[pl.BlockSpec((tm, d), lambda j, i, te: (i, 0)),
                      pl.BlockSpec((1, d, tf), lambda j, i, te: (te[i], 0, j)),
                      pl.BlockSpec((1, d, tf), lambda j, i, te: (te[i], 0, j + nf))],
            out_specs=pl.BlockSpec((tm, tf), lambda j, i, te: (i, j)),
            scratch_shapes=[pltpu.VMEM((d, tf), MXU_DTYPE), pltpu.VMEM((d, tf), MXU_DTYPE)],
        ),
        out_shape=jax.ShapeDtypeStruct((n_rows, f), MXU_DTYPE),
        compiler_params=_params("arbitrary", "arbitrary"),
        name="moe_up",
    )(tile_expert, xs, w_gu, w_gu)
    ys = pl.pallas_call(
        _moe_down_kernel,
        grid_spec=pltpu.PrefetchScalarGridSpec(
            num_scalar_prefetch=1,
            grid=(n_tiles,),
            in_specs=[pl.BlockSpec((tm, f), lambda i, te: (i, 0)),
                      pl.BlockSpec((1, f, d), lambda i, te: (te[i], 0, 0)),
                      pl.BlockSpec((tm, 1), lambda i, te: (i, 0))],
            out_specs=pl.BlockSpec((tm, d), lambda i, te: (i, 0)),
        ),
        out_shape=jax.ShapeDtypeStruct((n_rows, d), F32),
        compiler_params=_params("arbitrary"),
        name="moe_down",
    )(tile_expert, h, _mx(w_down), row_w.reshape(n_rows, 1))

    lo_first = top_idx[:, 0:1] < top_idx[:, 1:2]
    p0 = jnp.where(lo_first[:, 0], pos[:, 0], pos[:, 1])
    p1 = jnp.where(lo_first[:, 0], pos[:, 1], pos[:, 0])
    y0 = ys[p0]
    y1 = ys[p1]
    return pl.pallas_call(
        _add_ln_kernel,
        grid=(m // 512,),
        in_specs=[pl.BlockSpec((512, d), lambda i: (i, 0))] * 3
        + [pl.BlockSpec((1, d), lambda i: (0, 0))] * 2,
        out_specs=pl.BlockSpec((512, d), lambda i: (i, 0)),
        out_shape=jax.ShapeDtypeStruct((m, d), F32),
        compiler_params=_params("parallel"),
        name="moe_add_ln",
    )(x2, y0, y1, ln_g.reshape(1, d), ln_b.reshape(1, d))


def kernel(x, l0_w_in, l0_cmp_pe_k, l0_cmp_pe_v, l0_cmp_wk1, l0_cmp_wk2, l0_cmp_wv1, l0_cmp_wv2, l0_w_o, l0_ln1_g, l0_ln1_b, l0_ffn_w_gu, l0_ffn_w_down, l0_ln2_g, l0_ln2_b, l1_w_in, l1_w_gate2, l1_b_gate2, l1_head_norm_g, l1_w_o, l1_ln1_g, l1_ln1_b, l1_router, l1_moe_w_gu, l1_moe_w_down, l1_ln2_g, l1_ln2_b):
    bsz, t, d = x.shape
    x2 = x.reshape(bsz * t, d)

    o = _nsa_mixer(x2, bsz, t, l0_w_in, l0_cmp_pe_k, l0_cmp_pe_v, l0_cmp_wk1, l0_cmp_wk2,
                   l0_cmp_wv1, l0_cmp_wv2)
    x2 = _matmul_res_ln(o, _mx(l0_w_o), x2, l0_ln1_g, l0_ln1_b, 512)
    hmid = _swiglu_up(x2, _mx(l0_ffn_w_gu), 512, FFN_DENSE // 2)
    x2 = _matmul_res_ln(hmid, _mx(l0_ffn_w_down), x2, l0_ln2_g, l0_ln2_b, 512)

    o = _gla_mixer(x2, bsz, t, l1_w_in, l1_w_gate2, l1_b_gate2, l1_head_norm_g)
    x2 = _matmul_res_ln(o, _mx(l1_w_o), x2, l1_ln1_g, l1_ln1_b, 512)
    x2 = _moe_layer(x2, l1_router, l1_moe_w_gu, l1_moe_w_down, l1_ln2_g, l1_ln2_b)
    return x2.reshape(bsz, t, d)
```

```python
import functools

import numpy as np
import jax
import jax.numpy as jnp
from jax import lax
from jax.experimental import pallas as pl
from jax.experimental.pallas import tpu as pltpu

F32 = jnp.float32
MXU_DTYPE = jnp.bfloat16

D_MODEL = 1024
DEPTH = 2
DN_ALPHA = (2 * DEPTH) ** 0.25
LN_EPS = 1e-5
RMS_EPS = 1e-6

NSA_HEADS = 16
NSA_HEAD_DIM = 64
NSA_KV_GROUPS = 4
NSA_HPG = NSA_HEADS // NSA_KV_GROUPS
NSA_KV_DIM = NSA_KV_GROUPS * NSA_HEAD_DIM
CMP_LEN = 32
CMP_STRIDE = 16
CMP_HIDDEN = 256
SLC_LEN = 64
N_SEL = 16
WINDOW = 512
QTILE = 128
KBLK = 128
NCOL = NSA_HPG * QTILE
CMP_CHUNK = 128
POS_FEATS = 16
SEL_UNROLL = 4
LOG2E = 1.4426950408889634

GLA_HEADS = 4
GLA_KEY_DIM = D_MODEL // 2
GLA_VAL_DIM = D_MODEL
GLA_DK = GLA_KEY_DIM // GLA_HEADS
GLA_DV = GLA_VAL_DIM // GLA_HEADS
GLA_GATE_RANK = 16
GLA_GATE_NORM = 16.0
GLA_CHUNK = 64
GLA_TSTEP = 256

FFN_DENSE = 2816
N_EXPERTS = 8
TOP_K = 2
FFN_EXPERT = 3584
MOE_TM = 256
MOE_TF = 896

LANE = 128
NEG = -1e30
VMEM_LIMIT = 56 * 1024 * 1024


def _params(*sem):
    return pltpu.CompilerParams(dimension_semantics=sem, vmem_limit_bytes=VMEM_LIMIT)


def _mx(a):
    return a.astype(MXU_DTYPE)


def _dot(a, b):
    return jnp.dot(a, b, preferred_element_type=F32)


def _dot_nt(a, b):
    return lax.dot_general(a, b, (((1,), (1,)), ((), ())), preferred_element_type=F32)


def _mm_kernel(x_ref, w_ref, o_ref):
    o_ref[...] = _dot(_mx(x_ref[...]), w_ref[...]).astype(o_ref.dtype)


def _matmul(x, w, tm, tn, out_dtype=F32):
    m, k = x.shape
    n = w.shape[1]
    return pl.pallas_call(
        _mm_kernel,
        grid=(m // tm, n // tn),
        in_specs=[pl.BlockSpec((tm, k), lambda i, j: (i, 0)),
                  pl.BlockSpec((k, tn), lambda i, j: (0, j))],
        out_specs=pl.BlockSpec((tm, tn), lambda i, j: (i, j)),
        out_shape=jax.ShapeDtypeStruct((m, n), out_dtype),
        compiler_params=_params("parallel", "parallel"),
        name="matmul",
    )(x, w)


def _layer_norm_rows(h, g, b):
    mu = jnp.mean(h, -1, keepdims=True)
    d = h - mu
    var = jnp.mean(d * d, -1, keepdims=True)
    return d * lax.rsqrt(var + LN_EPS) * g + b


def _mm_ln_kernel(x_ref, w_ref, r_ref, g_ref, b_ref, o_ref):
    y = _dot(_mx(x_ref[...]), w_ref[...])
    o_ref[...] = _layer_norm_rows(DN_ALPHA * r_ref[...] + y, g_ref[...], b_ref[...])


def _matmul_res_ln(x, w, res, g, b, tm):
    m, k = x.shape
    n = w.shape[1]
    return pl.pallas_call(
        _mm_ln_kernel,
        grid=(m // tm,),
        in_specs=[pl.BlockSpec((tm, k), lambda i: (i, 0)),
                  pl.BlockSpec((k, n), lambda i: (0, 0)),
                  pl.BlockSpec((tm, n), lambda i: (i, 0)),
                  pl.BlockSpec((1, n), lambda i: (0, 0)),
                  pl.BlockSpec((1, n), lambda i: (0, 0))],
        out_specs=pl.BlockSpec((tm, n), lambda i: (i, 0)),
        out_shape=jax.ShapeDtypeStruct((m, n), F32),
        compiler_params=_params("parallel"),
        name="matmul_res_ln",
    )(x, w, res, g.reshape(1, n), b.reshape(1, n))


def _swiglu_up_kernel(x_ref, wg_ref, wu_ref, o_ref):
    x = _mx(x_ref[...])
    a = _dot(x, wg_ref[...])
    b = _dot(x, wu_ref[...])
    o_ref[...] = (a * jax.nn.sigmoid(a) * b).astype(o_ref.dtype)


def _swiglu_up(x, w_gu, tm, tn):
    m, k = x.shape
    f = w_gu.shape[1] // 2
    nj = f // tn
    return pl.pallas_call(
        _swiglu_up_kernel,
        grid=(m // tm, nj),
        in_specs=[pl.BlockSpec((tm, k), lambda i, j: (i, 0)),
                  pl.BlockSpec((k, tn), lambda i, j: (0, j)),
                  pl.BlockSpec((k, tn), lambda i, j: (0, j + nj))],
        out_specs=pl.BlockSpec((tm, tn), lambda i, j: (i, j)),
        out_shape=jax.ShapeDtypeStruct((m, f), MXU_DTYPE),
        compiler_params=_params("parallel", "parallel"),
        name="swiglu_up",
    )(x, w_gu, w_gu)


def _cmp_up_kernel(c_ref, pea_ref, peb_ref, wa_ref, wb_ref, p_ref, q_ref):
    c = c_ref[...]
    p_ref[...] = _dot(_mx(c + pea_ref[...]), wa_ref[...])
    q_ref[...] = _dot(_mx(c + peb_ref[...]), wb_ref[...])


def _cmp_down_kernel(p_ref, q_ref, w_ref, o_ref):
    h = jax.nn.gelu(p_ref[...] + q_ref[...])
    o_ref[...] = _dot(_mx(h), w_ref[...])


def _compress(a, pe, w1, w2):
    bsz, t, g, dk = a.shape
    nch = t // CMP_STRIDE
    half = CMP_STRIDE * dk
    chunks = a.reshape(bsz, nch, CMP_STRIDE, g, dk).transpose(0, 1, 3, 2, 4).reshape(bsz * nch * g, half)
    rows = chunks.shape[0]
    tm = min(512, rows)
    pe_a = pe[:CMP_STRIDE].reshape(1, half)
    pe_b = pe[CMP_STRIDE:].reshape(1, half)
    w1m = _mx(w1)
    p, q = pl.pallas_call(
        _cmp_up_kernel,
        grid=(rows // tm,),
        in_specs=[pl.BlockSpec((tm, half), lambda i: (i, 0)),
                  pl.BlockSpec((1, half), lambda i: (0, 0)),
                  pl.BlockSpec((1, half), lambda i: (0, 0)),
                  pl.BlockSpec((half, CMP_HIDDEN), lambda i: (0, 0)),
                  pl.BlockSpec((half, CMP_HIDDEN), lambda i: (0, 0))],
        out_specs=[pl.BlockSpec((tm, CMP_HIDDEN), lambda i: (i, 0)),
                   pl.BlockSpec((tm, CMP_HIDDEN), lambda i: (i, 0))],
        out_shape=[jax.ShapeDtypeStruct((rows, CMP_HIDDEN), F32)] * 2,
        compiler_params=_params("parallel"),
        name="cmp_up",
    )(chunks, pe_a, pe_b, w1m[:half], w1m[half:])
    q = q.reshape(bsz, nch, g, CMP_HIDDEN)
    q = jnp.concatenate([q[:, 1:], jnp.zeros_like(q[:, :1])], axis=1).reshape(rows, CMP_HIDDEN)
    w2p = jnp.pad(_mx(w2), ((0, 0), (0, LANE - dk)))
    out = pl.pallas_call(
        _cmp_down_kernel,
        grid=(rows // tm,),
        in_specs=[pl.BlockSpec((tm, CMP_HIDDEN), lambda i: (i, 0)),
                  pl.BlockSpec((tm, CMP_HIDDEN), lambda i: (i, 0)),
                  pl.BlockSpec((CMP_HIDDEN, LANE), lambda i: (0, 0))],
        out_specs=pl.BlockSpec((tm, LANE), lambda i: (i, 0)),
        out_shape=jax.ShapeDtypeStruct((rows, LANE), F32),
        compiler_params=_params("parallel"),
        name="cmp_down",
    )(p, q, w2p)
    out = out[:, :dk].reshape(bsz, nch, g, dk)
    valid = (jnp.arange(nch) < nch - 1)[None, :, None, None]
    return jnp.where(valid, out, 0.0)


def _nsa_cmp_kernel(qt_ref, kc_ref, vct_ref, ovl_ref, slope_ref, ocmp_ref, sel_ref, flag_ref,
                    imp_ref, key_ref, cnt_ref):
    c = pl.program_id(2)
    ncp = kc_ref.shape[2]
    ns = ovl_ref.shape[0]
    qt = qt_ref[0, 0, 0, :kc_ref.shape[3], :]
    chunk = min(CMP_CHUNK, ncp)
    n_chunks = (c * (QTILE // CMP_STRIDE) + (QTILE - CMP_LEN) // CMP_STRIDE + chunk) // chunk

    def cmp_branch(rows):
        s = _dot(kc_ref[0, 0, :rows, :], qt)
        n_i = lax.broadcasted_iota(jnp.int32, (rows, NCOL), 0)
        col = lax.broadcasted_iota(jnp.int32, (rows, NCOL), 1)
        tq = c * QTILE + (col & (QTILE - 1))
        dist = tq - (n_i * CMP_STRIDE + (CMP_LEN - 1))
        s = jnp.where(dist >= 0, s - slope_ref[0] * dist.astype(F32), NEG)
        m = jnp.max(s, axis=0, keepdims=True)
        e = jnp.exp2(s - m)
        l = jnp.sum(e, axis=0, keepdims=True)
        r = jnp.where(tq[0:1, :] >= CMP_LEN - 1, 1.0 / jnp.maximum(l, 1e-30), 0.0)
        p = e * r
        ocmp_ref[0, 0, 0] = _dot(vct_ref[0, 0, :, :rows], _mx(p))
        psum = p[:, 0:QTILE]
        for h in range(1, NSA_HPG):
            psum = psum + p[:, h * QTILE:(h + 1) * QTILE]
        hi = _mx(psum)
        lo = _mx(psum - hi.astype(F32))
        ovl = ovl_ref[:, :rows]
        imp_ref[...] = _dot(ovl, hi) + _dot(ovl, lo)

    for k in range(1, ncp // chunk + 1):
        pl.when(n_chunks == k)(functools.partial(cmp_branch, k * chunk))

    j_i = lax.broadcasted_iota(jnp.int32, (ns, QTILE), 0)
    tq2 = c * QTILE + lax.broadcasted_iota(jnp.int32, (ns, QTILE), 1)
    valid = j_i * SLC_LEN <= tq2
    cur = tq2 >> (SLC_LEN.bit_length() - 1)
    forced = valid & ((j_i == 0) | (j_i == cur) | (j_i == cur - 1))
    score = jnp.where(forced, 1e9, jnp.where(valid, imp_ref[...], -1e9))
    bits = lax.bitcast_convert_type(score, jnp.int32)
    key_ref[...] = jnp.where(bits < 0, bits ^ 0x7FFFFFFF, bits)
    cnt_ref[...] = jnp.zeros(cnt_ref.shape, jnp.int32)

    sub = lax.broadcasted_iota(jnp.int32, (8, QTILE), 0)
    n_it = (c * (QTILE // SLC_LEN) + QTILE // SLC_LEN + 7) // 8

    def it_body(it, carry):
        base = pl.multiple_of(it * 8, 8)
        rows8 = key_ref[pl.ds(base, 8), :]
        rb = [jnp.broadcast_to(rows8[il:il + 1, :], (8, QTILE)) for il in range(8)]
        for jt in range(ns // 8):
            js = slice(jt * 8, (jt + 1) * 8)
            thr = key_ref[js, :] - jnp.where(jt > it, 1, 0)
            acc = cnt_ref[js, :]
            for il in range(8):
                acc = acc + jnp.where(rb[il] > thr, 1, 0)
            cnt_ref[js, :] = acc
        corr = jnp.zeros((8, QTILE), jnp.int32)
        for il in range(8):
            corr = corr + jnp.where((rb[il] == rows8) & (sub > il), 1, 0)
        cnt_ref[pl.ds(base, 8), :] += corr
        return carry

    lax.fori_loop(0, n_it, it_body, 0)
    sel = jnp.where(cnt_ref[...] < min(N_SEL, ns), 1.0, 0.0)
    sel_ref[0, 0, 0] = sel
    flag_ref[0, 0, 0] = _dot_nt(jnp.ones((8, QTILE), MXU_DTYPE), _mx(sel))


def _nsa_cmp_select(qt, kcmp, vcmp_t, ovl_t, slopes):
    bsz, g, nt, ka, _ = qt.shape
    ncp, dk = kcmp.shape[2:]
    ns = ovl_t.shape[0]
    return pl.pallas_call(
        _nsa_cmp_kernel,
        grid=(bsz, g, nt),
        in_specs=[pl.BlockSpec((1, 1, 1, ka, NCOL), lambda b, gg, c: (b, gg, c, 0, 0)),
                  pl.BlockSpec((1, 1, ncp, dk), lambda b, gg, c: (b, gg, 0, 0)),
                  pl.BlockSpec((1, 1, dk, ncp), lambda b, gg, c: (b, gg, 0, 0)),
                  pl.BlockSpec((ns, ncp), lambda b, gg, c: (0, 0)),
                  pl.BlockSpec((1, 1, NCOL), lambda b, gg, c: (gg, 0, 0))],
        out_specs=[pl.BlockSpec((1, 1, 1, dk, NCOL), lambda b, gg, c: (b, gg, c, 0, 0)),
                   pl.BlockSpec((1, 1, 1, ns, QTILE), lambda b, gg, c: (b, gg, c, 0, 0)),
                   pl.BlockSpec((1, 1, 1, 8, ns), lambda b, gg, c: (b, gg, c, 0, 0))],
        out_shape=[jax.ShapeDtypeStruct((bsz, g, nt, dk, NCOL), F32),
                   jax.ShapeDtypeStruct((bsz, g, nt, ns, QTILE), F32),
                   jax.ShapeDtypeStruct((bsz, g, nt, 8, ns), F32)],
        scratch_shapes=[pltpu.VMEM((ns, QTILE), F32), pltpu.VMEM((ns, QTILE), jnp.int32),
                        pltpu.VMEM((ns, QTILE), jnp.int32)],
        compiler_params=_params("parallel", "parallel", "parallel"),
        name="nsa_cmp_select",
    )(qt, kcmp, vcmp_t, ovl_t, slopes)


def _nsa_attn_kernel(lst_ref, cnt_ref, qt_ref, ks_ref, vst_ref, kw_ref, vwt_ref, sel_ref, ocmp_ref, gl_ref,
                     o_ref, m_sc, l_sc, acc_sc):
    b = pl.program_id(0)
    g = pl.program_id(1)
    c = pl.program_id(2)
    nt = pl.num_programs(2)
    npair = ks_ref.shape[2]
    dk = acc_sc.shape[0]
    qt = qt_ref[0, 0, 0]
    key_i = lax.broadcasted_iota(jnp.int32, (KBLK, QTILE), 0)
    tq = c * QTILE + lax.broadcasted_iota(jnp.int32, (KBLK, QTILE), 1)
    heads = [slice(h * QTILE, (h + 1) * QTILE) for h in range(NSA_HPG)]

    m_sc[...] = jnp.full(m_sc.shape, NEG, F32)
    l_sc[...] = jnp.zeros(l_sc.shape, F32)
    acc_sc[...] = jnp.zeros(acc_sc.shape, F32)
    step = (b * NSA_KV_GROUPS + g) * nt + c

    def group_body(r, carry):
        blocks, scores, masks = [], [], []
        for u in range(SEL_UNROLL):
            jj = lst_ref[step * npair + r * SEL_UNROLL + u]
            js = jnp.maximum(jj, 0)
            kbase = jnp.where(jj >= 0, js * KBLK, nt * QTILE)
            rows = sel_ref[0, 0, 0, pl.ds(2 * js, 2), :]
            picked = jnp.where(key_i < SLC_LEN, rows[0:1, :], rows[1:2, :]) > 0.5
            blocks.append(js)
            masks.append(picked & (kbase + key_i <= tq))
            scores.append(_dot(ks_ref[0, 0, js], qt))
        es = [[] for _ in range(SEL_UNROLL)]
        for cs in heads:
            sh = [jnp.where(masks[u], scores[u][:, cs], NEG) for u in range(SEL_UNROLL)]
            m_old = m_sc[:, cs]
            m_new = m_old
            for u in range(SEL_UNROLL):
                m_new = jnp.maximum(m_new, jnp.max(sh[u], axis=0, keepdims=True))
            alpha = jnp.exp2(m_old - m_new)
            l_new = alpha * l_sc[:, cs]
            for u in range(SEL_UNROLL):
                e = jnp.exp2(sh[u] - m_new)
                l_new = l_new + jnp.sum(e, axis=0, keepdims=True)
                es[u].append(_mx(e))
            l_sc[:, cs] = l_new
            acc_sc[:, cs] = alpha * acc_sc[:, cs]
            m_sc[:, cs] = m_new
        upd = _dot(vst_ref[0, 0, blocks[0]], jnp.concatenate(es[0], axis=1))
        for u in range(1, SEL_UNROLL):
            upd = upd + _dot(vst_ref[0, 0, blocks[u]], jnp.concatenate(es[u], axis=1))
        acc_sc[...] += upd
        return carry

    lax.fori_loop(0, (cnt_ref[step] + SEL_UNROLL - 1) // SEL_UNROLL, group_body, 0)
    o_sel = acc_sc[...] * (1.0 / jnp.maximum(l_sc[...], 1e-30))

    nwin = WINDOW // KBLK + 1
    kwin = kw_ref[0, 0, pl.ds(c, nwin)].reshape(nwin * KBLK, qt.shape[0])
    s = _dot(kwin, qt)
    row = lax.broadcasted_iota(jnp.int32, (nwin * KBLK, QTILE), 0)
    dist = lax.broadcasted_iota(jnp.int32, (nwin * KBLK, QTILE), 1) + WINDOW - row
    mask = (dist >= 0) & (dist < WINDOW) & (row >= (nwin - 1 - c) * KBLK)
    es, rs = [], []
    for cs in heads:
        sh = jnp.where(mask, s[:, cs], NEG)
        e = jnp.exp2(sh - jnp.max(sh, axis=0, keepdims=True))
        rs.append(1.0 / jnp.maximum(jnp.sum(e, axis=0, keepdims=True), 1e-30))
        es.append(_mx(e))
    e_all = jnp.concatenate(es, axis=1)
    acc = jnp.zeros((dk, NCOL), F32)
    for t in range(nwin):
        acc = acc + _dot(vwt_ref[0, 0, c + t], e_all[t * KBLK:(t + 1) * KBLK, :])
    o_win = acc * jnp.concatenate(rs, axis=1)

    gates = jax.nn.sigmoid(gl_ref[0, 0, 0])
    o_ref[0, 0, 0] = gates[0:1] * ocmp_ref[0, 0, 0] + gates[1:2] * o_sel + gates[2:3] * o_win


def _nsa_attend(blk_list, blk_count, qt, ks, vs_t, kw, vw_t, sel_t, ocmp_t, gl_t):
    bsz, g, nt, ka, _ = qt.shape
    npair = ks.shape[2]
    nwpad = kw.shape[2]
    dk = vs_t.shape[3]
    ns = sel_t.shape[3]
    tile = lambda b, gg, c, lst, cnt: (b, gg, c, 0, 0)
    whole = lambda b, gg, c, lst, cnt: (b, gg, 0, 0, 0)
    grid_spec = pltpu.PrefetchScalarGridSpec(
        num_scalar_prefetch=2,
        grid=(bsz, g, nt),
        in_specs=[pl.BlockSpec((1, 1, 1, ka, NCOL), tile),
                  pl.BlockSpec((1, 1, npair, KBLK, ka), whole),
                  pl.BlockSpec((1, 1, npair, dk, KBLK), whole),
                  pl.BlockSpec((1, 1, nwpad, KBLK, ka), whole),
                  pl.BlockSpec((1, 1, nwpad, dk, KBLK), whole),
                  pl.BlockSpec((1, 1, 1, ns, QTILE), tile),
                  pl.BlockSpec((1, 1, 1, dk, NCOL), tile),
                  pl.BlockSpec((1, 1, 1, 3, NCOL), tile)],
        out_specs=pl.BlockSpec((1, 1, 1, dk, NCOL), tile),
        scratch_shapes=[pltpu.VMEM((1, NCOL), F32), pltpu.VMEM((1, NCOL), F32),
                        pltpu.VMEM((dk, NCOL), F32)],
    )
    return pl.pallas_call(
        _nsa_attn_kernel,
        grid_spec=grid_spec,
        out_shape=jax.ShapeDtypeStruct((bsz, g, nt, dk, NCOL), F32),
        compiler_params=_params("parallel", "parallel", "parallel"),
        name="nsa_attend",
    )(blk_list, blk_count, qt, ks, vs_t, kw, vw_t, sel_t, ocmp_t, gl_t)


def _nsa_mixer(x2, bsz, t, w_in, pe_k, pe_v, wk1, wk2, wv1, wv2):
    g, hpg, dk = NSA_KV_GROUPS, NSA_HPG, NSA_HEAD_DIM
    nt = t // QTILE
    n_in = w_in.shape[1]
    n_pad = -n_in % LANE
    w_in_p = jnp.pad(_mx(w_in), ((0, 0), (0, n_pad)))
    tn = (n_in + n_pad) // 3 if (n_in + n_pad) % (3 * LANE) == 0 else LANE
    proj = _matmul(x2, w_in_p, 512, tn)
    splits = np.cumsum([D_MODEL] + [NSA_KV_DIM] * 6).tolist()
    q, kc, vc, ks, vs, kw, vw, gl = jnp.split(proj[:, :n_in], splits, axis=-1)

    slopes = 2.0 ** (-8.0 * (jnp.arange(NSA_HEADS, dtype=F32) + 1.0) / NSA_HEADS) * LOG2E
    slopes = jnp.broadcast_to(slopes.reshape(g, 1, hpg, 1), (g, 1, hpg, QTILE)).reshape(g, 1, NCOL)
    s1 = slopes.astype(jnp.bfloat16)
    s2 = (slopes - s1.astype(F32)).astype(jnp.bfloat16)
    s3 = (slopes - s1.astype(F32) - s2.astype(F32)).astype(jnp.bfloat16)
    q_feat = jnp.concatenate([s1 * SLC_LEN, s2 * SLC_LEN, s3 * SLC_LEN, s1, s2, s3]
                             + [jnp.zeros_like(s1)] * (POS_FEATS - 6), axis=1)
    kpos = jnp.arange(t, dtype=jnp.int32)
    ka_, kb_ = (kpos // SLC_LEN).astype(jnp.bfloat16), (kpos % SLC_LEN).astype(jnp.bfloat16)
    k_feat = jnp.stack([ka_, ka_, ka_, kb_, kb_, kb_] + [jnp.zeros_like(ka_)] * (POS_FEATS - 6), axis=1)
    k_feat = jnp.broadcast_to(k_feat.reshape(1, 1, t // KBLK, KBLK, POS_FEATS),
                              (bsz, g, t // KBLK, KBLK, POS_FEATS)).astype(MXU_DTYPE)

    qt = (q * (dk ** -0.5 * LOG2E)).reshape(bsz, nt, QTILE, g, hpg, dk).transpose(0, 3, 1, 5, 4, 2)
    qt = _mx(qt).reshape(bsz, g, nt, dk, NCOL)
    qt = jnp.concatenate([qt, jnp.broadcast_to(q_feat.astype(MXU_DTYPE)[None, :, None],
                                               (bsz, g, nt, POS_FEATS, NCOL))], axis=3)
    gl_t = gl.reshape(bsz, nt, QTILE, g, hpg, 3).transpose(0, 3, 1, 5, 4, 2).reshape(bsz, g, nt, 3, NCOL)

    def key_blocks(a):
        kb = _mx(a).reshape(bsz, t // KBLK, KBLK, g, dk).transpose(0, 3, 1, 2, 4)
        return jnp.concatenate([kb, k_feat], axis=-1)

    def val_blocks_t(a):
        return _mx(a).reshape(bsz, t // KBLK, KBLK, g, dk).transpose(0, 3, 1, 4, 2)

    def front_pad(a):
        return jnp.pad(a, ((0, 0), (0, 0), (WINDOW // KBLK, 0), (0, 0), (0, 0)))

    four = lambda a: a.reshape(bsz, t, g, dk)
    k_cmp = _compress(four(kc), pe_k, wk1, wk2)
    v_cmp = _compress(four(vc), pe_v, wv1, wv2)
    kcmp = _mx(k_cmp).transpose(0, 2, 1, 3)
    vcmp_t = _mx(v_cmp).transpose(0, 2, 3, 1)

    ncp = t // CMP_STRIDE
    ns = t // SLC_LEN
    cs = np.arange(ncp) * CMP_STRIDE
    ce = cs + CMP_LEN - 1
    bs = np.arange(ns) * SLC_LEN
    be = bs + SLC_LEN - 1
    ovl_t = ((cs[None, :] <= be[:, None]) & (ce[None, :] >= bs[:, None])).astype(np.float32)
    ovl_t[:, ncp - 1] = 0.0
    ovl_t = jnp.asarray(ovl_t, MXU_DTYPE)

    ocmp_t, sel_t, counts = _nsa_cmp_select(qt, kcmp, vcmp_t, ovl_t, slopes)
    npair = t // KBLK
    picked = counts[:, :, :, 0, :] > 0.5
    jj = jnp.arange(npair)
    own = jj[None, :] == jnp.arange(nt)[:, None]
    active = ((picked[..., 0::2] | picked[..., 1::2]) & (jj[None, :] < jnp.arange(nt)[:, None])) | own
    blk_count = jnp.sum(active, axis=-1).astype(jnp.int32)
    rank = jnp.cumsum(active, axis=-1) - 1
    hit = active[..., None, :] & (rank[..., None, :] == jj[:, None])
    blk_list = jnp.sum(jnp.where(hit, jj, 0), axis=-1)
    blk_list = jnp.where(jj < blk_count[..., None], blk_list, -1).astype(jnp.int32)

    o_t = _nsa_attend(blk_list.reshape(-1), blk_count.reshape(-1), qt,
                      key_blocks(four(ks)), val_blocks_t(four(vs)),
                      front_pad(key_blocks(four(kw))), front_pad(val_blocks_t(four(vw))),
                      sel_t, ocmp_t, gl_t)
    o = o_t.reshape(bsz, g, nt, dk, hpg, QTILE).transpose(0, 2, 5, 1, 4, 3)
    return o.reshape(bsz * t, NSA_HEADS * dk)


def _log_sigmoid(z):
    return jnp.minimum(z, 0.0) - jnp.log1p(jnp.exp(-jnp.abs(z)))


def _gla_kernel(q_ref, k_ref, v_ref, g_ref, al_ref, w2_ref, b2_ref, hn_ref, o_ref, st_sc):
    @pl.when(pl.program_id(0) == 0)
    def _():
        st_sc[...] = jnp.zeros(st_sc.shape, F32)

    cc = GLA_CHUNK
    r_i = lax.broadcasted_iota(jnp.int32, (cc, cc), 0)
    c_i = lax.broadcasted_iota(jnp.int32, (cc, cc), 1)
    causal = r_i >= c_i
    tri = jnp.where(causal, 1.0, 0.0).astype(MXU_DTYPE)
    ref = cc // 2

    def chunk(ci, carry):
        rows = pl.ds(pl.multiple_of(ci * cc, cc), cc)
        for b in range(q_ref.shape[0]):
            z = _dot(_mx(al_ref[b, rows, :]), w2_ref[...]) + b2_ref[...]
            log_a = _log_sigmoid(z) / GLA_GATE_NORM
            a1 = _mx(log_a)
            r1 = log_a - a1.astype(F32)
            a2 = _mx(r1)
            a3 = _mx(r1 - a2.astype(F32))
            bc_all = _dot(tri, a1) + _dot(tri, a2) + _dot(tri, a3)
            for h in range(GLA_HEADS):
                ks = slice(h * GLA_DK, (h + 1) * GLA_DK)
                vs = slice(h * GLA_DV, (h + 1) * GLA_DV)
                bc = bc_all[:, ks]
                q = q_ref[b, rows, ks] * GLA_DK ** -0.5
                k = k_ref[b, rows, ks]
                v = v_ref[b, rows, vs]
                b_ref = bc[ref:ref + 1, :]
                b_last = bc[cc - 1:cc, :]
                a = _dot_nt(_mx(q * jnp.exp(bc - b_ref)), _mx(k * jnp.exp(b_ref - bc)))
                a = jnp.where(causal, a, 0.0)
                st = st_sc[b * GLA_HEADS + h]
                o = _dot(_mx(a), _mx(v)) + _dot_nt(_mx(q * jnp.exp(bc)), _mx(st))
                kl = k * jnp.exp(b_last - bc)
                st_sc[b * GLA_HEADS + h] = st * jnp.exp(b_last) + _dot(_mx(v.T), _mx(kl))
                o = o * lax.rsqrt(jnp.mean(o * o, -1, keepdims=True) + RMS_EPS)
                gg = g_ref[b, rows, vs]
                o_ref[b, rows, vs] = o * hn_ref[:, vs] * (gg * jax.nn.sigmoid(gg))
        return carry

    lax.fori_loop(0, q_ref.shape[1] // cc, chunk, 0)


def _gla_mixer(x2, bsz, t, w_in, w_gate2, b_gate2, head_norm_g):
    n_main = 2 * GLA_KEY_DIM + 2 * GLA_VAL_DIM
    w_main = _mx(w_in[:, :n_main])
    w_low = jnp.pad(_mx(w_in[:, n_main:]), ((0, 0), (0, LANE - GLA_GATE_RANK)))
    proj = _matmul(x2, w_main, 512, 1024).reshape(bsz, t, n_main)
    a_low = _matmul(x2, w_low, 512, LANE).reshape(bsz, t, LANE)
    w2p = jnp.pad(_mx(w_gate2), ((0, LANE - GLA_GATE_RANK), (0, 0)))
    ts = min(GLA_TSTEP, t)
    out = pl.pallas_call(
        _gla_kernel,
        grid=(t // ts,),
        in_specs=[pl.BlockSpec((bsz, ts, GLA_KEY_DIM), lambda s: (0, s, 0)),
                  pl.BlockSpec((bsz, ts, GLA_KEY_DIM), lambda s: (0, s, 1)),
                  pl.BlockSpec((bsz, ts, GLA_VAL_DIM), lambda s: (0, s, 1)),
                  pl.BlockSpec((bsz, ts, GLA_VAL_DIM), lambda s: (0, s, 2)),
                  pl.BlockSpec((bsz, ts, LANE), lambda s: (0, s, 0)),
                  pl.BlockSpec((LANE, GLA_KEY_DIM), lambda s: (0, 0)),
                  pl.BlockSpec((1, GLA_KEY_DIM), lambda s: (0, 0)),
                  pl.BlockSpec((1, GLA_VAL_DIM), lambda s: (0, 0))],
        out_specs=pl.BlockSpec((bsz, ts, GLA_VAL_DIM), lambda s: (0, s, 0)),
        out_shape=jax.ShapeDtypeStruct((bsz, t, GLA_VAL_DIM), F32),
        scratch_shapes=[pltpu.VMEM((bsz * GLA_HEADS, GLA_DV, GLA_DK), F32)],
        compiler_params=_params("arbitrary"),
        name="gla",
    )(proj, proj, proj, proj, a_low, w2p, b_gate2.reshape(1, -1), head_norm_g.reshape(1, -1))
    return out.reshape(bsz * t, GLA_VAL_DIM)


def _router_kernel(x_ref, wh_ref, wl_ref, o_ref):
    x = x_ref[...]
    xh = _mx(x)
    xl = _mx(x - xh.astype(F32))
    wh = wh_ref[...]
    logits = _dot(xh, wh) + _dot(xl, wh) + _dot(xh, wl_ref[...])
    lane = lax.broadcasted_iota(jnp.int32, logits.shape, 1)
    lg = jnp.where(lane < N_EXPERTS, logits, -jnp.inf)
    m1 = jnp.max(lg, axis=1, keepdims=True)
    i1 = jnp.min(jnp.where(lg == m1, lane, LANE), axis=1, keepdims=True)
    lg2 = jnp.where(lane == i1, -jnp.inf, lg)
    m2 = jnp.max(lg2, axis=1, keepdims=True)
    i2 = jnp.min(jnp.where(lg2 == m2, lane, LANE), axis=1, keepdims=True)
    e2 = jnp.exp(m2 - m1)
    den = 1.0 + e2
    w1 = 1.0 / den
    w2 = e2 / den
    out = jnp.where(lane == N_EXPERTS, i1.astype(F32), 0.0)
    out = jnp.where(lane == N_EXPERTS + 1, i2.astype(F32), out)
    out = jnp.where(lane == N_EXPERTS + 2, w1, out)
    out = jnp.where(lane == N_EXPERTS + 3, w2, out)
    o_ref[...] = out


def _moe_up_kernel(te_ref, x_ref, wg_ref, wu_ref, o_ref, wg_sc, wu_sc):
    i = pl.program_id(1)

    @pl.when((i == 0) | (te_ref[i] != te_ref[jnp.maximum(i - 1, 0)]))
    def _():
        wg_sc[...] = _mx(wg_ref[0])
        wu_sc[...] = _mx(wu_ref[0])

    x = _mx(x_ref[...])
    a = _dot(x, wg_sc[...])
    b = _dot(x, wu_sc[...])
    o_ref[...] = (a * jax.nn.sigmoid(a) * b).astype(o_ref.dtype)


def _moe_down_kernel(te_ref, h_ref, w_ref, rw_ref, o_ref, w_sc):
    i = pl.program_id(0)

    @pl.when((i == 0) | (te_ref[i] != te_ref[jnp.maximum(i - 1, 0)]))
    def _():
        w_sc[...] = _mx(w_ref[0])

    o_ref[...] = _dot(h_ref[...], w_sc[...]) * rw_ref[...]


def _add_ln_kernel(x_ref, y0_ref, y1_ref, g_ref, b_ref, o_ref):
    h = DN_ALPHA * x_ref[...] + (y0_ref[...] + y1_ref[...])
    o_ref[...] = _layer_norm_rows(h, g_ref[...], b_ref[...])


def _moe_layer(x2, w_router, w_gu, w_down, ln_g, ln_b):
    m, d = x2.shape
    tm = MOE_TM
    wr = jnp.pad(w_router, ((0, 0), (0, LANE - N_EXPERTS)))
    wr_hi = _mx(wr)
    wr_lo = _mx(wr - wr_hi.astype(F32))
    rt = pl.pallas_call(
        _router_kernel,
        grid=(m // 512,),
        in_specs=[pl.BlockSpec((512, d), lambda i: (i, 0)),
                  pl.BlockSpec((d, LANE), lambda i: (0, 0)),
                  pl.BlockSpec((d, LANE), lambda i: (0, 0))],
        out_specs=pl.BlockSpec((512, LANE), lambda i: (i, 0)),
        out_shape=jax.ShapeDtypeStruct((m, LANE), F32),
        compiler_params=_params("parallel"),
        name="moe_router",
    )(x2, wr_hi, wr_lo)
    top_idx = rt[:, N_EXPERTS:N_EXPERTS + 2].astype(jnp.int32)
    top_w = rt[:, N_EXPERTS + 2:N_EXPERTS + 4]

    n_asg = m * TOP_K
    n_rows = n_asg + N_EXPERTS * tm
    n_tiles = n_rows // tm
    e_flat = top_idx.reshape(-1)
    order = jnp.argsort(e_flat, stable=True).astype(jnp.int32)
    slot = jnp.argsort(order).astype(jnp.int32)
    counts = jnp.sum((e_flat[:, None] == jnp.arange(N_EXPERTS)[None, :]).astype(jnp.int32), axis=0)
    tiles_per = (counts + tm - 1) // tm
    tile_end = jnp.cumsum(tiles_per)
    row_start = (tile_end - tiles_per) * tm
    grp_start = jnp.cumsum(counts) - counts
    tile_expert = jnp.minimum(jnp.sum((jnp.arange(n_tiles)[:, None] >= tile_end[None, :]).astype(jnp.int32),
                                      axis=1), N_EXPERTS - 1).astype(jnp.int32)
    pos = (row_start[e_flat] + slot - grp_start[e_flat]).astype(jnp.int32).reshape(m, TOP_K)
    row_e = jnp.repeat(tile_expert, tm)
    row_off = jnp.arange(n_rows, dtype=jnp.int32) - row_start[row_e].astype(jnp.int32)
    row_live = row_off < counts[row_e]
    row_asg = order[jnp.clip(grp_start[row_e].astype(jnp.int32) + row_off, 0, n_asg - 1)]
    row_token = jnp.where(row_live, row_asg // TOP_K, 0)
    row_w = jnp.where(row_live, top_w.reshape(-1)[row_asg], 0.0)

    xs = x2[row_token]
    f = w_gu.shape[2] // 2
    tf = MOE_TF
    nf = f // tf
    h = pl.pallas_call(
        _moe_up_kernel,
        grid_spec=pltpu.PrefetchScalarGridSpec(
            num_scalar_prefetch=1,
            grid=(nf, n_tiles),
            in_specs=[pl.BlockSpec((tm, d), lambda j, i, te: (i, 0)),
                      pl.BlockSpec((1, d, tf), lambda j, i, te: (te[i], 0, j)),
                      pl.BlockSpec((1, d, tf), lambda j, i, te: (te[i], 0, j + nf))],
            out_specs=pl.BlockSpec((tm, tf), lambda j, i, te: (i, j)),
            scratch_shapes=[pltpu.VMEM((d, tf), MXU_DTYPE), pltpu.VMEM((d, tf), MXU_DTYPE)],
        ),
        out_shape=jax.ShapeDtypeStruct((n_rows, f), MXU_DTYPE),
        compiler_params=_params("arbitrary", "arbitrary"),
        name="moe_up",
    )(tile_expert, xs, w_gu, w_gu)
    ys = pl.pallas_call(
        _moe_down_kernel,
        grid_spec=pltpu.PrefetchScalarGridSpec(
            num_scalar_prefetch=1,
            grid=(n_tiles,),
            in_specs=[pl.BlockSpec((tm, f), lambda i, te: (i, 0)),
                      pl.BlockSpec((1, f, d), lambda i, te: (te[i], 0, 0)),
                      pl.BlockSpec((tm, 1), lambda i, te: (i, 0))],
            out_specs=pl.BlockSpec((tm, d), lambda i, te: (i, 0)),
            scratch_shapes=[pltpu.VMEM((f, d), MXU_DTYPE)],
        ),
        out_shape=jax.ShapeDtypeStruct((n_rows, d), F32),
        compiler_params=_params("arbitrary"),
        name="moe_down",
    )(tile_expert, h, w_down, row_w.reshape(n_rows, 1))

    lo_first = top_idx[:, 0:1] < top_idx[:, 1:2]
    p0 = jnp.where(lo_first[:, 0], pos[:, 0], pos[:, 1])
    p1 = jnp.where(lo_first[:, 0], pos[:, 1], pos[:, 0])
    y0 = ys[p0]
    y1 = ys[p1]
    return pl.pallas_call(
        _add_ln_kernel,
        grid=(m // 512,),
        in_specs=[pl.BlockSpec((512, d), lambda i: (i, 0))] * 3
        + [pl.BlockSpec((1, d), lambda i: (0, 0))] * 2,
        out_specs=pl.BlockSpec((512, d), lambda i: (i, 0)),
        out_shape=jax.ShapeDtypeStruct((m, d), F32),
        compiler_params=_params("parallel"),
        name="moe_add_ln",
    )(x2, y0, y1, ln_g.reshape(1, d), ln_b.reshape(1, d))


def kernel(x, l0_w_in, l0_cmp_pe_k, l0_cmp_pe_v, l0_cmp_wk1, l0_cmp_wk2, l0_cmp_wv1, l0_cmp_wv2, l0_w_o, l0_ln1_g, l0_ln1_b, l0_ffn_w_gu, l0_ffn_w_down, l0_ln2_g, l0_ln2_b, l1_w_in, l1_w_gate2, l1_b_gate2, l1_head_norm_g, l1_w_o, l1_ln1_g, l1_ln1_b, l1_router, l1_moe_w_gu, l1_moe_w_down, l1_ln2_g, l1_ln2_b):
    bsz, t, d = x.shape
    x2 = x.reshape(bsz * t, d)

    o = _nsa_mixer(x2, bsz, t, l0_w_in, l0_cmp_pe_k, l0_cmp_pe_v, l0_cmp_wk1, l0_cmp_wk2,
                   l0_cmp_wv1, l0_cmp_wv2)
    x2 = _matmul_res_ln(o, _mx(l0_w_o), x2, l0_ln1_g, l0_ln1_b, 512)
    hmid = _swiglu_up(x2, _mx(l0_ffn_w_gu), 512, FFN_DENSE // 2)
    x2 = _matmul_res_ln(hmid, _mx(l0_ffn_w_down), x2, l0_ln2_g, l0_ln2_b, 512)

    o = _gla_mixer(x2, bsz, t, l1_w_in, l1_w_gate2, l1_b_gate2, l1_head_norm_g)
    x2 = _matmul_res_ln(o, _mx(l1_w_o), x2, l1_ln1_g, l1_ln1_b, 512)
    x2 = _moe_layer(x2, l1_router, l1_moe_w_gu, l1_moe_w_down, l1_ln2_g, l1_ln2_b)
    return x2.reshape(bsz, t, d)
```

```python
import functools

import numpy as np
import jax
import jax.numpy as jnp
from jax import lax
from jax.experimental import pallas as pl
from jax.experimental.pallas import tpu as pltpu

F32 = jnp.float32
MXU_DTYPE = jnp.bfloat16

D_MODEL = 1024
DEPTH = 2
DN_ALPHA = (2 * DEPTH) ** 0.25
LN_EPS = 1e-5
RMS_EPS = 1e-6

NSA_HEADS = 16
NSA_HEAD_DIM = 64
NSA_KV_GROUPS = 4
NSA_HPG = NSA_HEADS // NSA_KV_GROUPS
NSA_KV_DIM = NSA_KV_GROUPS * NSA_HEAD_DIM
CMP_LEN = 32
CMP_STRIDE = 16
CMP_HIDDEN = 256
SLC_LEN = 64
N_SEL = 16
WINDOW = 512
QTILE = 128
KBLK = 128
NCOL = NSA_HPG * QTILE
CMP_CHUNK = 128
SEL_UNROLL = 4
LOG2E = 1.4426950408889634

GLA_HEADS = 4
GLA_KEY_DIM = D_MODEL // 2
GLA_VAL_DIM = D_MODEL
GLA_DK = GLA_KEY_DIM // GLA_HEADS
GLA_DV = GLA_VAL_DIM // GLA_HEADS
GLA_GATE_RANK = 16
GLA_GATE_NORM = 16.0
GLA_CHUNK = 64
GLA_TSTEP = 256

FFN_DENSE = 2816
N_EXPERTS = 8
TOP_K = 2
FFN_EXPERT = 3584
MOE_TM = 256
MOE_TF = 896

LANE = 128
NEG = -1e30
VMEM_LIMIT = 56 * 1024 * 1024


def _params(*sem):
    return pltpu.CompilerParams(dimension_semantics=sem, vmem_limit_bytes=VMEM_LIMIT)


def _mx(a):
    return a.astype(MXU_DTYPE)


def _dot(a, b):
    return jnp.dot(a, b, preferred_element_type=F32)


def _dot_nt(a, b):
    return lax.dot_general(a, b, (((1,), (1,)), ((), ())), preferred_element_type=F32)


def _mm_kernel(x_ref, w_ref, o_ref):
    o_ref[...] = _dot(_mx(x_ref[...]), w_ref[...]).astype(o_ref.dtype)


def _matmul(x, w, tm, tn, out_dtype=F32):
    m, k = x.shape
    n = w.shape[1]
    return pl.pallas_call(
        _mm_kernel,
        grid=(m // tm, n // tn),
        in_specs=[pl.BlockSpec((tm, k), lambda i, j: (i, 0)),
                  pl.BlockSpec((k, tn), lambda i, j: (0, j))],
        out_specs=pl.BlockSpec((tm, tn), lambda i, j: (i, j)),
        out_shape=jax.ShapeDtypeStruct((m, n), out_dtype),
        compiler_params=_params("parallel", "parallel"),
        name="matmul",
    )(x, w)


def _mm_scaled_kernel(x_ref, w_ref, s_ref, o_ref):
    o_ref[...] = (_dot(_mx(x_ref[...]), w_ref[...]) * s_ref[...]).astype(o_ref.dtype)


def _matmul_scaled(x, w, col_scale, tm, tn, out_dtype):
    m, k = x.shape
    n = w.shape[1]
    return pl.pallas_call(
        _mm_scaled_kernel,
        grid=(m // tm, n // tn),
        in_specs=[pl.BlockSpec((tm, k), lambda i, j: (i, 0)),
                  pl.BlockSpec((k, tn), lambda i, j: (0, j)),
                  pl.BlockSpec((1, tn), lambda i, j: (0, j))],
        out_specs=pl.BlockSpec((tm, tn), lambda i, j: (i, j)),
        out_shape=jax.ShapeDtypeStruct((m, n), out_dtype),
        compiler_params=_params("parallel", "parallel"),
        name="matmul_scaled",
    )(x, w, col_scale)


def _layer_norm_rows(h, g, b):
    mu = jnp.mean(h, -1, keepdims=True)
    d = h - mu
    var = jnp.mean(d * d, -1, keepdims=True)
    return d * lax.rsqrt(var + LN_EPS) * g + b


def _mm_ln_kernel(x_ref, w_ref, r_ref, g_ref, b_ref, o_ref):
    y = _dot(_mx(x_ref[...]), w_ref[...])
    o_ref[...] = _layer_norm_rows(DN_ALPHA * r_ref[...] + y, g_ref[...], b_ref[...])


def _matmul_res_ln(x, w, res, g, b, tm):
    m, k = x.shape
    n = w.shape[1]
    return pl.pallas_call(
        _mm_ln_kernel,
        grid=(m // tm,),
        in_specs=[pl.BlockSpec((tm, k), lambda i: (i, 0)),
                  pl.BlockSpec((k, n), lambda i: (0, 0)),
                  pl.BlockSpec((tm, n), lambda i: (i, 0)),
                  pl.BlockSpec((1, n), lambda i: (0, 0)),
                  pl.BlockSpec((1, n), lambda i: (0, 0))],
        out_specs=pl.BlockSpec((tm, n), lambda i: (i, 0)),
        out_shape=jax.ShapeDtypeStruct((m, n), F32),
        compiler_params=_params("parallel"),
        name="matmul_res_ln",
    )(x, w, res, g.reshape(1, n), b.reshape(1, n))


def _swiglu_up_kernel(x_ref, wg_ref, wu_ref, o_ref):
    x = _mx(x_ref[...])
    a = _dot(x, wg_ref[...])
    b = _dot(x, wu_ref[...])
    o_ref[...] = (a * jax.nn.sigmoid(a) * b).astype(o_ref.dtype)


def _swiglu_up(x, w_gu, tm, tn):
    m, k = x.shape
    f = w_gu.shape[1] // 2
    nj = f // tn
    return pl.pallas_call(
        _swiglu_up_kernel,
        grid=(m // tm, nj),
        in_specs=[pl.BlockSpec((tm, k), lambda i, j: (i, 0)),
                  pl.BlockSpec((k, tn), lambda i, j: (0, j)),
                  pl.BlockSpec((k, tn), lambda i, j: (0, j + nj))],
        out_specs=pl.BlockSpec((tm, tn), lambda i, j: (i, j)),
        out_shape=jax.ShapeDtypeStruct((m, f), MXU_DTYPE),
        compiler_params=_params("parallel", "parallel"),
        name="swiglu_up",
    )(x, w_gu, w_gu)


def _cmp_up_kernel(c_ref, pea_ref, peb_ref, wa_ref, wb_ref, p_ref, q_ref):
    c = c_ref[...]
    p_ref[...] = _dot(_mx(c + pea_ref[...]), wa_ref[...])
    q_ref[...] = _dot(_mx(c + peb_ref[...]), wb_ref[...])


def _cmp_down_kernel(p_ref, q_ref, w_ref, o_ref):
    h = jax.nn.gelu(p_ref[...] + q_ref[...])
    o_ref[...] = _dot(_mx(h), w_ref[...])


def _compress(a, pe, w1, w2):
    bsz, t, g, dk = a.shape
    nch = t // CMP_STRIDE
    half = CMP_STRIDE * dk
    chunks = a.reshape(bsz, nch, CMP_STRIDE, g, dk).transpose(0, 1, 3, 2, 4).reshape(bsz * nch * g, half)
    rows = chunks.shape[0]
    tm = min(512, rows)
    pe_a = pe[:CMP_STRIDE].reshape(1, half)
    pe_b = pe[CMP_STRIDE:].reshape(1, half)
    w1m = _mx(w1)
    p, q = pl.pallas_call(
        _cmp_up_kernel,
        grid=(rows // tm,),
        in_specs=[pl.BlockSpec((tm, half), lambda i: (i, 0)),
                  pl.BlockSpec((1, half), lambda i: (0, 0)),
                  pl.BlockSpec((1, half), lambda i: (0, 0)),
                  pl.BlockSpec((half, CMP_HIDDEN), lambda i: (0, 0)),
                  pl.BlockSpec((half, CMP_HIDDEN), lambda i: (0, 0))],
        out_specs=[pl.BlockSpec((tm, CMP_HIDDEN), lambda i: (i, 0)),
                   pl.BlockSpec((tm, CMP_HIDDEN), lambda i: (i, 0))],
        out_shape=[jax.ShapeDtypeStruct((rows, CMP_HIDDEN), F32)] * 2,
        compiler_params=_params("parallel"),
        name="cmp_up",
    )(chunks, pe_a, pe_b, w1m[:half], w1m[half:])
    q = q.reshape(bsz, nch, g, CMP_HIDDEN)
    q = jnp.concatenate([q[:, 1:], jnp.zeros_like(q[:, :1])], axis=1).reshape(rows, CMP_HIDDEN)
    w2p = jnp.pad(_mx(w2), ((0, 0), (0, LANE - dk)))
    out = pl.pallas_call(
        _cmp_down_kernel,
        grid=(rows // tm,),
        in_specs=[pl.BlockSpec((tm, CMP_HIDDEN), lambda i: (i, 0)),
                  pl.BlockSpec((tm, CMP_HIDDEN), lambda i: (i, 0)),
                  pl.BlockSpec((CMP_HIDDEN, LANE), lambda i: (0, 0))],
        out_specs=pl.BlockSpec((tm, LANE), lambda i: (i, 0)),
        out_shape=jax.ShapeDtypeStruct((rows, LANE), F32),
        compiler_params=_params("parallel"),
        name="cmp_down",
    )(p, q, w2p)
    out = out[:, :dk].reshape(bsz, nch, g, dk)
    valid = (jnp.arange(nch) < nch - 1)[None, :, None, None]
    return jnp.where(valid, out, 0.0)


def _queries_t(q_blk):
    qt = q_blk.astype(F32).T
    dk = NSA_HEAD_DIM
    return _mx(jnp.concatenate([qt[h * dk:(h + 1) * dk, :] for h in range(NSA_HPG)], axis=1))


def _nsa_cmp_kernel(q_ref, kc_ref, vct_ref, ovl_ref, slope_ref, ocmp_ref, sel_ref, flag_ref,
                    imp_ref, key_ref, cnt_ref):
    c = pl.program_id(2)
    ncp = kc_ref.shape[2]
    ns = ovl_ref.shape[0]
    qt = _queries_t(q_ref[...])
    chunk = min(CMP_CHUNK, ncp)
    n_chunks = (c * (QTILE // CMP_STRIDE) + (QTILE - CMP_LEN) // CMP_STRIDE + chunk) // chunk

    def cmp_branch(rows):
        s = _dot(kc_ref[0, 0, :rows, :], qt)
        n_i = lax.broadcasted_iota(jnp.int32, (rows, NCOL), 0)
        col = lax.broadcasted_iota(jnp.int32, (rows, NCOL), 1)
        tq = c * QTILE + (col & (QTILE - 1))
        dist = tq - (n_i * CMP_STRIDE + (CMP_LEN - 1))
        s = jnp.where(dist >= 0, s - slope_ref[0] * dist.astype(F32), NEG)
        m = jnp.max(s, axis=0, keepdims=True)
        e = jnp.exp2(s - m)
        l = jnp.sum(e, axis=0, keepdims=True)
        r = jnp.where(tq[0:1, :] >= CMP_LEN - 1, 1.0 / jnp.maximum(l, 1e-30), 0.0)
        p = e * r
        ocmp_ref[0, 0, 0] = _dot(vct_ref[0, 0, :, :rows], _mx(p))
        psum = p[:, 0:QTILE]
        for h in range(1, NSA_HPG):
            psum = psum + p[:, h * QTILE:(h + 1) * QTILE]
        hi = _mx(psum)
        lo = _mx(psum - hi.astype(F32))
        ovl = ovl_ref[:, :rows]
        imp_ref[...] = _dot(ovl, hi) + _dot(ovl, lo)

    for k in range(1, ncp // chunk + 1):
        pl.when(n_chunks == k)(functools.partial(cmp_branch, k * chunk))

    j_i = lax.broadcasted_iota(jnp.int32, (ns, QTILE), 0)
    tq2 = c * QTILE + lax.broadcasted_iota(jnp.int32, (ns, QTILE), 1)
    valid = j_i * SLC_LEN <= tq2
    cur = tq2 >> (SLC_LEN.bit_length() - 1)
    forced = valid & ((j_i == 0) | (j_i == cur) | (j_i == cur - 1))
    score = jnp.where(forced, 1e9, jnp.where(valid, imp_ref[...], -1e9))
    bits = lax.bitcast_convert_type(score, jnp.int32)
    key_ref[...] = jnp.where(bits < 0, bits ^ 0x7FFFFFFF, bits)
    cnt_ref[...] = jnp.zeros(cnt_ref.shape, jnp.int32)

    sub = lax.broadcasted_iota(jnp.int32, (8, QTILE), 0)
    n_it = (c * (QTILE // SLC_LEN) + QTILE // SLC_LEN + 7) // 8

    def it_body(it, carry):
        base = pl.multiple_of(it * 8, 8)
        rows8 = key_ref[pl.ds(base, 8), :]
        rb = [jnp.broadcast_to(rows8[il:il + 1, :], (8, QTILE)) for il in range(8)]
        for jt in range(ns // 8):
            js = slice(jt * 8, (jt + 1) * 8)
            thr = key_ref[js, :] - jnp.where(jt > it, 1, 0)
            acc = cnt_ref[js, :]
            for il in range(8):
                acc = acc + jnp.where(rb[il] > thr, 1, 0)
            cnt_ref[js, :] = acc
        corr = jnp.zeros((8, QTILE), jnp.int32)
        for il in range(8):
            corr = corr + jnp.where((rb[il] == rows8) & (sub > il), 1, 0)
        cnt_ref[pl.ds(base, 8), :] += corr
        return carry

    lax.fori_loop(0, n_it, it_body, 0)
    sel = jnp.where(cnt_ref[...] < min(N_SEL, ns), 1.0, 0.0)
    sel_ref[0, 0, 0] = sel
    flag_ref[0, 0, 0] = _dot_nt(jnp.ones((8, QTILE), MXU_DTYPE), _mx(sel))


def _nsa_cmp_select(proj, kcmp, vcmp_t, ovl_t, slopes):
    bsz, g, ncp, dk = kcmp.shape
    nt = proj.shape[0] // (bsz * QTILE)
    ns = ovl_t.shape[0]
    return pl.pallas_call(
        _nsa_cmp_kernel,
        grid=(bsz, g, nt),
        in_specs=[pl.BlockSpec((QTILE, NSA_HPG * dk), lambda b, gg, c: (b * nt + c, gg)),
                  pl.BlockSpec((1, 1, ncp, dk), lambda b, gg, c: (b, gg, 0, 0)),
                  pl.BlockSpec((1, 1, dk, ncp), lambda b, gg, c: (b, gg, 0, 0)),
                  pl.BlockSpec((ns, ncp), lambda b, gg, c: (0, 0)),
                  pl.BlockSpec((1, 1, NCOL), lambda b, gg, c: (gg, 0, 0))],
        out_specs=[pl.BlockSpec((1, 1, 1, dk, NCOL), lambda b, gg, c: (b, gg, c, 0, 0)),
                   pl.BlockSpec((1, 1, 1, ns, QTILE), lambda b, gg, c: (b, gg, c, 0, 0)),
                   pl.BlockSpec((1, 1, 1, 8, ns), lambda b, gg, c: (b, gg, c, 0, 0))],
        out_shape=[jax.ShapeDtypeStruct((bsz, g, nt, dk, NCOL), F32),
                   jax.ShapeDtypeStruct((bsz, g, nt, ns, QTILE), F32),
                   jax.ShapeDtypeStruct((bsz, g, nt, 8, ns), F32)],
        scratch_shapes=[pltpu.VMEM((ns, QTILE), F32), pltpu.VMEM((ns, QTILE), jnp.int32),
                        pltpu.VMEM((ns, QTILE), jnp.int32)],
        compiler_params=_params("parallel", "parallel", "parallel"),
        name="nsa_cmp_select",
    )(proj, kcmp, vcmp_t, ovl_t, slopes)


def _nsa_attn_kernel(lst_ref, cnt_ref, q_ref, qf_ref, ks_ref, vs_ref, kw_ref, vw_ref, pos_ref, sel_ref,
                     ocmp_ref, gl_ref, o_ref, ksa_sc, kwa_sc, vst_sc, vwt_sc, gate_sc, m_sc, l_sc, acc_sc):
    b = pl.program_id(0)
    g = pl.program_id(1)
    c = pl.program_id(2)
    nt = pl.num_programs(2)
    npair = ks_ref.shape[1] // KBLK
    dk = acc_sc.shape[0]

    @pl.when(c == 0)
    def _():
        def prep(j, carry):
            rows = pl.ds(pl.multiple_of(j * KBLK, KBLK), KBLK)
            feats = pos_ref[rows, :]
            ksa_sc[rows, :] = ks_ref[0, rows, :] + feats
            kwa_sc[rows, :] = kw_ref[0, rows, :] + feats
            vst_sc[j] = _mx(vs_ref[0, rows, :].astype(F32).T[:dk, :])
            vwt_sc[j] = _mx(vw_ref[0, rows, :].astype(F32).T[:dk, :])
            return carry

        lax.fori_loop(0, npair, prep, 0)

    qt = jnp.concatenate([_queries_t(q_ref[...]), qf_ref[0]], axis=0)
    key_i = lax.broadcasted_iota(jnp.int32, (KBLK, QTILE), 0)
    tq = c * QTILE + lax.broadcasted_iota(jnp.int32, (KBLK, QTILE), 1)
    heads = [slice(h * QTILE, (h + 1) * QTILE) for h in range(NSA_HPG)]

    m_sc[...] = jnp.full(m_sc.shape, NEG, F32)
    l_sc[...] = jnp.zeros(l_sc.shape, F32)
    acc_sc[...] = jnp.zeros(acc_sc.shape, F32)
    step = (b * NSA_KV_GROUPS + g) * nt + c

    def group_body(r, carry):
        blocks, scores, masks = [], [], []
        for u in range(SEL_UNROLL):
            jj = lst_ref[step * npair + r * SEL_UNROLL + u]
            js = jnp.maximum(jj, 0)
            kbase = jnp.where(jj >= 0, js * KBLK, nt * QTILE)
            rows = sel_ref[0, 0, 0, pl.ds(2 * js, 2), :]
            picked = jnp.where(key_i < SLC_LEN, rows[0:1, :], rows[1:2, :]) > 0.5
            blocks.append(js)
            masks.append(picked & (kbase + key_i <= tq))
            scores.append(_dot(ksa_sc[pl.ds(pl.multiple_of(js * KBLK, KBLK), KBLK), :], qt))
        es = [[] for _ in range(SEL_UNROLL)]
        for cs in heads:
            sh = [jnp.where(masks[u], scores[u][:, cs], NEG) for u in range(SEL_UNROLL)]
            m_old = m_sc[:, cs]
            m_new = m_old
            for u in range(SEL_UNROLL):
                m_new = jnp.maximum(m_new, jnp.max(sh[u], axis=0, keepdims=True))
            alpha = jnp.exp2(m_old - m_new)
            l_new = alpha * l_sc[:, cs]
            for u in range(SEL_UNROLL):
                e = jnp.exp2(sh[u] - m_new)
                l_new = l_new + jnp.sum(e, axis=0, keepdims=True)
                es[u].append(_mx(e))
            l_sc[:, cs] = l_new
            acc_sc[:, cs] = alpha * acc_sc[:, cs]
            m_sc[:, cs] = m_new
        upd = _dot(vst_sc[blocks[0]], jnp.concatenate(es[0], axis=1))
        for u in range(1, SEL_UNROLL):
            upd = upd + _dot(vst_sc[blocks[u]], jnp.concatenate(es[u], axis=1))
        acc_sc[...] += upd
        return carry

    lax.fori_loop(0, (cnt_ref[step] + SEL_UNROLL - 1) // SEL_UNROLL, group_body, 0)
    o_sel = acc_sc[...] * (1.0 / jnp.maximum(l_sc[...], 1e-30))

    nwin = WINDOW // KBLK + 1
    blk0 = jnp.maximum(c - (nwin - 1), 0)
    s = _dot(kwa_sc[pl.ds(pl.multiple_of(blk0 * KBLK, KBLK), nwin * KBLK), :], qt)
    row = lax.broadcasted_iota(jnp.int32, (nwin * KBLK, QTILE), 0)
    dist = lax.broadcasted_iota(jnp.int32, (nwin * KBLK, QTILE), 1) + (c - blk0) * KBLK - row
    mask = (dist >= 0) & (dist < WINDOW)
    es, rs = [], []
    for cs in heads:
        sh = jnp.where(mask, s[:, cs], NEG)
        e = jnp.exp2(sh - jnp.max(sh, axis=0, keepdims=True))
        rs.append(1.0 / jnp.maximum(jnp.sum(e, axis=0, keepdims=True), 1e-30))
        es.append(_mx(e))
    e_all = jnp.concatenate(es, axis=1)
    acc = jnp.zeros((dk, NCOL), F32)
    for t in range(nwin):
        acc = acc + _dot(vwt_sc[blk0 + t], e_all[t * KBLK:(t + 1) * KBLK, :])
    o_win = acc * jnp.concatenate(rs, axis=1)

    gate_sc[...] = gl_ref[...].T
    ocmp = ocmp_ref[0, 0, 0]
    outs = []
    for h, cs in enumerate(heads):
        gt = jax.nn.sigmoid(gate_sc[pl.ds((g * NSA_HPG + h) * 3, 3), :])
        outs.append(gt[0:1] * ocmp[:, cs] + gt[1:2] * o_sel[:, cs] + gt[2:3] * o_win[:, cs])
    o_ref[...] = jnp.concatenate(outs, axis=0).T


def _nsa_attend(blk_list, blk_count, proj, proj_f, q_feat, pos_feat, sel_t, ocmp_t, bsz):
    g, dk = NSA_KV_GROUPS, NSA_HEAD_DIM
    t = proj.shape[0] // bsz
    nt = t // QTILE
    npair = t // KBLK
    ns = sel_t.shape[3]
    proj3 = proj.reshape(bsz, t, proj.shape[1])
    seg = D_MODEL // LANE
    gate_blk = proj_f.shape[1] // LANE - 1
    tile = lambda b, gg, c, lst, cnt: (b, gg, c, 0, 0)
    rows = lambda b, gg, c, lst, cnt: (b * nt + c, gg)
    kv = lambda k: (lambda b, gg, c, lst, cnt: (b, 0, seg + k * g + gg))
    grid_spec = pltpu.PrefetchScalarGridSpec(
        num_scalar_prefetch=2,
        grid=(bsz, g, nt),
        in_specs=[pl.BlockSpec((QTILE, NSA_HPG * dk), rows),
                  pl.BlockSpec((1, dk, NCOL), lambda b, gg, c, lst, cnt: (gg, 0, 0)),
                  pl.BlockSpec((1, t, LANE), kv(0)),
                  pl.BlockSpec((1, t, LANE), kv(1)),
                  pl.BlockSpec((1, t, LANE), kv(2)),
                  pl.BlockSpec((1, t, LANE), kv(3)),
                  pl.BlockSpec((t, LANE), lambda b, gg, c, lst, cnt: (0, 0)),
                  pl.BlockSpec((1, 1, 1, ns, QTILE), tile),
                  pl.BlockSpec((1, 1, 1, dk, NCOL), tile),
                  pl.BlockSpec((QTILE, LANE), lambda b, gg, c, lst, cnt: (b * nt + c, gate_blk))],
        out_specs=pl.BlockSpec((QTILE, NSA_HPG * dk), rows),
        scratch_shapes=[pltpu.VMEM((t, LANE), MXU_DTYPE), pltpu.VMEM((t, LANE), MXU_DTYPE),
                        pltpu.VMEM((npair, dk, KBLK), MXU_DTYPE), pltpu.VMEM((npair, dk, KBLK), MXU_DTYPE),
                        pltpu.VMEM((LANE, QTILE), F32),
                        pltpu.VMEM((1, NCOL), F32), pltpu.VMEM((1, NCOL), F32),
                        pltpu.VMEM((dk, NCOL), F32)],
    )
    return pl.pallas_call(
        _nsa_attn_kernel,
        grid_spec=grid_spec,
        out_shape=jax.ShapeDtypeStruct((bsz * t, NSA_HEADS * dk), F32),
        compiler_params=_params("parallel", "parallel", "arbitrary"),
        name="nsa_attend",
    )(blk_list, blk_count, proj, q_feat, proj3, proj3, proj3, proj3, pos_feat, sel_t, ocmp_t, proj_f)


def _nsa_mixer(x2, bsz, t, w_in, pe_k, pe_v, wk1, wk2, wv1, wv2):
    g, hpg, dk = NSA_KV_GROUPS, NSA_HPG, NSA_HEAD_DIM
    nt = t // QTILE
    d = w_in.shape[0]
    splits = np.cumsum([D_MODEL] + [NSA_KV_DIM] * 6).tolist()
    wq, wkc, wvc, wks, wvs, wkw, wvw, wgl = jnp.split(w_in, splits, axis=1)

    def lane_groups(w):
        return jnp.pad(w.reshape(d, g, dk), ((0, 0), (0, 0), (0, LANE - dk))).reshape(d, g * LANE)

    w_a = _mx(jnp.concatenate([wq] + [lane_groups(w) for w in (wks, wvs, wkw, wvw)], axis=1))
    scale = jnp.concatenate([jnp.full((1, D_MODEL), dk ** -0.5 * LOG2E, F32),
                             jnp.ones((1, w_a.shape[1] - D_MODEL), F32)], axis=1)
    w_b = _mx(jnp.concatenate([wkc, wvc, jnp.pad(wgl, ((0, 0), (0, LANE - wgl.shape[1])))], axis=1))
    proj = _matmul_scaled(x2, w_a, scale, 512, 1024, MXU_DTYPE)
    proj_f = _matmul(x2, w_b, 512, w_b.shape[1])
    kc = proj_f[:, :NSA_KV_DIM]
    vc = proj_f[:, NSA_KV_DIM:2 * NSA_KV_DIM]

    slopes = 2.0 ** (-8.0 * (jnp.arange(NSA_HEADS, dtype=F32) + 1.0) / NSA_HEADS) * LOG2E
    slopes = jnp.broadcast_to(slopes.reshape(g, 1, hpg, 1), (g, 1, hpg, QTILE)).reshape(g, 1, NCOL)
    s1 = slopes.astype(jnp.bfloat16)
    s2 = (slopes - s1.astype(F32)).astype(jnp.bfloat16)
    s3 = (slopes - s1.astype(F32) - s2.astype(F32)).astype(jnp.bfloat16)
    q_feat = jnp.concatenate([s1 * SLC_LEN, s2 * SLC_LEN, s3 * SLC_LEN, s1, s2, s3]
                             + [jnp.zeros_like(s1)] * (dk - 6), axis=1).astype(MXU_DTYPE)
    kpos = jnp.arange(t, dtype=jnp.int32)
    ka_, kb_ = (kpos // SLC_LEN).astype(jnp.bfloat16), (kpos % SLC_LEN).astype(jnp.bfloat16)
    zero = jnp.zeros_like(ka_)
    pos_feat = jnp.stack([zero] * dk + [ka_, ka_, ka_, kb_, kb_, kb_] + [zero] * (LANE - dk - 6),
                         axis=1).astype(MXU_DTYPE)

    four = lambda a: a.reshape(bsz, t, g, dk)
    k_cmp = _compress(four(kc), pe_k, wk1, wk2)
    v_cmp = _compress(four(vc), pe_v, wv1, wv2)
    kcmp = _mx(k_cmp).transpose(0, 2, 1, 3)
    vcmp_t = _mx(v_cmp).transpose(0, 2, 3, 1)

    ncp = t // CMP_STRIDE
    ns = t // SLC_LEN
    cs = np.arange(ncp) * CMP_STRIDE
    ce = cs + CMP_LEN - 1
    bs = np.arange(ns) * SLC_LEN
    be = bs + SLC_LEN - 1
    ovl_t = ((cs[None, :] <= be[:, None]) & (ce[None, :] >= bs[:, None])).astype(np.float32)
    ovl_t[:, ncp - 1] = 0.0
    ovl_t = jnp.asarray(ovl_t, MXU_DTYPE)

    ocmp_t, sel_t, counts = _nsa_cmp_select(proj, kcmp, vcmp_t, ovl_t, slopes)
    npair = t // KBLK
    picked = counts[:, :, :, 0, :] > 0.5
    jj = jnp.arange(npair)
    own = jj[None, :] == jnp.arange(nt)[:, None]
    active = ((picked[..., 0::2] | picked[..., 1::2]) & (jj[None, :] < jnp.arange(nt)[:, None])) | own
    blk_count = jnp.sum(active, axis=-1).astype(jnp.int32)
    rank = jnp.cumsum(active, axis=-1) - 1
    hit = active[..., None, :] & (rank[..., None, :] == jj[:, None])
    blk_list = jnp.sum(jnp.where(hit, jj, 0), axis=-1)
    blk_list = jnp.where(jj < blk_count[..., None], blk_list, -1).astype(jnp.int32)

    return _nsa_attend(blk_list.reshape(-1), blk_count.reshape(-1), proj, proj_f, q_feat, pos_feat,
                       sel_t, ocmp_t, bsz)


def _log_sigmoid(z):
    return jnp.minimum(z, 0.0) - jnp.log1p(jnp.exp(-jnp.abs(z)))


def _gla_kernel(q_ref, k_ref, v_ref, g_ref, al_ref, w2_ref, b2_ref, hn_ref, o_ref, st_sc):
    @pl.when(pl.program_id(0) == 0)
    def _():
        st_sc[...] = jnp.zeros(st_sc.shape, F32)

    cc = GLA_CHUNK
    r_i = lax.broadcasted_iota(jnp.int32, (cc, cc), 0)
    c_i = lax.broadcasted_iota(jnp.int32, (cc, cc), 1)
    causal = r_i >= c_i
    tri = jnp.where(causal, 1.0, 0.0).astype(MXU_DTYPE)
    ref = cc // 2

    def chunk(ci, carry):
        rows = pl.ds(pl.multiple_of(ci * cc, cc), cc)
        for b in range(q_ref.shape[0]):
            z = _dot(_mx(al_ref[b, rows, :]), w2_ref[...]) + b2_ref[...]
            log_a = _log_sigmoid(z) / GLA_GATE_NORM
            a1 = _mx(log_a)
            r1 = log_a - a1.astype(F32)
            a2 = _mx(r1)
            a3 = _mx(r1 - a2.astype(F32))
            bc_all = _dot(tri, a1) + _dot(tri, a2) + _dot(tri, a3)
            for h in range(GLA_HEADS):
                ks = slice(h * GLA_DK, (h + 1) * GLA_DK)
                vs = slice(h * GLA_DV, (h + 1) * GLA_DV)
                bc = bc_all[:, ks]
                q = q_ref[b, rows, ks] * GLA_DK ** -0.5
                k = k_ref[b, rows, ks]
                v = v_ref[b, rows, vs]
                b_ref = bc[ref:ref + 1, :]
                b_last = bc[cc - 1:cc, :]
                a = _dot_nt(_mx(q * jnp.exp(bc - b_ref)), _mx(k * jnp.exp(b_ref - bc)))
                a = jnp.where(causal, a, 0.0)
                st = st_sc[b * GLA_HEADS + h]
                o = _dot(_mx(a), _mx(v)) + _dot_nt(_mx(q * jnp.exp(bc)), _mx(st))
                kl = k * jnp.exp(b_last - bc)
                st_sc[b * GLA_HEADS + h] = st * jnp.exp(b_last) + _dot(_mx(v.T), _mx(kl))
                o = o * lax.rsqrt(jnp.mean(o * o, -1, keepdims=True) + RMS_EPS)
                gg = g_ref[b, rows, vs]
                o_ref[b, rows, vs] = o * hn_ref[:, vs] * (gg * jax.nn.sigmoid(gg))
        return carry

    lax.fori_loop(0, q_ref.shape[1] // cc, chunk, 0)


def _gla_mixer(x2, bsz, t, w_in, w_gate2, b_gate2, head_norm_g):
    n_main = 2 * GLA_KEY_DIM + 2 * GLA_VAL_DIM
    w_main = _mx(w_in[:, :n_main])
    w_low = jnp.pad(_mx(w_in[:, n_main:]), ((0, 0), (0, LANE - GLA_GATE_RANK)))
    proj = _matmul(x2, w_main, 512, 1024).reshape(bsz, t, n_main)
    a_low = _matmul(x2, w_low, 512, LANE).reshape(bsz, t, LANE)
    w2p = jnp.pad(_mx(w_gate2), ((0, LANE - GLA_GATE_RANK), (0, 0)))
    ts = min(GLA_TSTEP, t)
    out = pl.pallas_call(
        _gla_kernel,
        grid=(t // ts,),
        in_specs=[pl.BlockSpec((bsz, ts, GLA_KEY_DIM), lambda s: (0, s, 0)),
                  pl.BlockSpec((bsz, ts, GLA_KEY_DIM), lambda s: (0, s, 1)),
                  pl.BlockSpec((bsz, ts, GLA_VAL_DIM), lambda s: (0, s, 1)),
                  pl.BlockSpec((bsz, ts, GLA_VAL_DIM), lambda s: (0, s, 2)),
                  pl.BlockSpec((bsz, ts, LANE), lambda s: (0, s, 0)),
                  pl.BlockSpec((LANE, GLA_KEY_DIM), lambda s: (0, 0)),
                  pl.BlockSpec((1, GLA_KEY_DIM), lambda s: (0, 0)),
                  pl.BlockSpec((1, GLA_VAL_DIM), lambda s: (0, 0))],
        out_specs=pl.BlockSpec((bsz, ts, GLA_VAL_DIM), lambda s: (0, s, 0)),
        out_shape=jax.ShapeDtypeStruct((bsz, t, GLA_VAL_DIM), F32),
        scratch_shapes=[pltpu.VMEM((bsz * GLA_HEADS, GLA_DV, GLA_DK), F32)],
        compiler_params=_params("arbitrary"),
        name="gla",
    )(proj, proj, proj, proj, a_low, w2p, b_gate2.reshape(1, -1), head_norm_g.reshape(1, -1))
    return out.reshape(bsz * t, GLA_VAL_DIM)


def _router_kernel(x_ref, wh_ref, wl_ref, o_ref):
    x = x_ref[...]
    xh = _mx(x)
    xl = _mx(x - xh.astype(F32))
    wh = wh_ref[...]
    logits = _dot(xh, wh) + _dot(xl, wh) + _dot(xh, wl_ref[...])
    lane = lax.broadcasted_iota(jnp.int32, logits.shape, 1)
    lg = jnp.where(lane < N_EXPERTS, logits, -jnp.inf)
    m1 = jnp.max(lg, axis=1, keepdims=True)
    i1 = jnp.min(jnp.where(lg == m1, lane, LANE), axis=1, keepdims=True)
    lg2 = jnp.where(lane == i1, -jnp.inf, lg)
    m2 = jnp.max(lg2, axis=1, keepdims=True)
    i2 = jnp.min(jnp.where(lg2 == m2, lane, LANE), axis=1, keepdims=True)
    e2 = jnp.exp(m2 - m1)
    den = 1.0 + e2
    w1 = 1.0 / den
    w2 = e2 / den
    out = jnp.where(lane == N_EXPERTS, i1.astype(F32), 0.0)
    out = jnp.where(lane == N_EXPERTS + 1, i2.astype(F32), out)
    out = jnp.where(lane == N_EXPERTS + 2, w1, out)
    out = jnp.where(lane == N_EXPERTS + 3, w2, out)
    o_ref[...] = out


def _moe_up_kernel(te_ref, x_ref, wg_ref, wu_ref, o_ref, wg_sc, wu_sc):
    i = pl.program_id(1)

    @pl.when((i == 0) | (te_ref[i] != te_ref[jnp.maximum(i - 1, 0)]))
    def _():
        wg_sc[...] = _mx(wg_ref[0])
        wu_sc[...] = _mx(wu_ref[0])

    x = _mx(x_ref[...])
    a = _dot(x, wg_sc[...])
    b = _dot(x, wu_sc[...])
    o_ref[...] = (a * jax.nn.sigmoid(a) * b).astype(o_ref.dtype)


def _moe_down_kernel(te_ref, h_ref, w_ref, rw_ref, o_ref, w_sc):
    i = pl.program_id(0)

    @pl.when((i == 0) | (te_ref[i] != te_ref[jnp.maximum(i - 1, 0)]))
    def _():
        w_sc[...] = _mx(w_ref[0])

    o_ref[...] = _dot(h_ref[...], w_sc[...]) * rw_ref[...]


def _add_ln_kernel(x_ref, y0_ref, y1_ref, g_ref, b_ref, o_ref):
    h = DN_ALPHA * x_ref[...] + (y0_ref[...] + y1_ref[...])
    o_ref[...] = _layer_norm_rows(h, g_ref[...], b_ref[...])


def _moe_layer(x2, w_router, w_gu, w_down, ln_g, ln_b):
    m, d = x2.shape
    tm = MOE_TM
    wr = jnp.pad(w_router, ((0, 0), (0, LANE - N_EXPERTS)))
    wr_hi = _mx(wr)
    wr_lo = _mx(wr - wr_hi.astype(F32))
    rt = pl.pallas_call(
        _router_kernel,
        grid=(m // 512,),
        in_specs=[pl.BlockSpec((512, d), lambda i: (i, 0)),
                  pl.BlockSpec((d, LANE), lambda i: (0, 0)),
                  pl.BlockSpec((d, LANE), lambda i: (0, 0))],
        out_specs=pl.BlockSpec((512, LANE), lambda i: (i, 0)),
        out_shape=jax.ShapeDtypeStruct((m, LANE), F32),
        compiler_params=_params("parallel"),
        name="moe_router",
    )(x2, wr_hi, wr_lo)
    top_idx = rt[:, N_EXPERTS:N_EXPERTS + 2].astype(jnp.int32)
    top_w = rt[:, N_EXPERTS + 2:N_EXPERTS + 4]

    n_asg = m * TOP_K
    n_rows = n_asg + N_EXPERTS * tm
    n_tiles = n_rows // tm
    e_flat = top_idx.reshape(-1)
    order = jnp.argsort(e_flat, stable=True).astype(jnp.int32)
    slot = jnp.argsort(order).astype(jnp.int32)
    counts = jnp.sum((e_flat[:, None] == jnp.arange(N_EXPERTS)[None, :]).astype(jnp.int32), axis=0)
    tiles_per = (counts + tm - 1) // tm
    tile_end = jnp.cumsum(tiles_per)
    row_start = (tile_end - tiles_per) * tm
    grp_start = jnp.cumsum(counts) - counts
    tile_expert = jnp.minimum(jnp.sum((jnp.arange(n_tiles)[:, None] >= tile_end[None, :]).astype(jnp.int32),
                                      axis=1), N_EXPERTS - 1).astype(jnp.int32)
    pos = (row_start[e_flat] + slot - grp_start[e_flat]).astype(jnp.int32).reshape(m, TOP_K)
    row_e = jnp.repeat(tile_expert, tm)
    row_off = jnp.arange(n_rows, dtype=jnp.int32) - row_start[row_e].astype(jnp.int32)
    row_live = row_off < counts[row_e]
    row_asg = order[jnp.clip(grp_start[row_e].astype(jnp.int32) + row_off, 0, n_asg - 1)]
    row_token = jnp.where(row_live, row_asg // TOP_K, 0)
    row_w = jnp.where(row_live, top_w.reshape(-1)[row_asg], 0.0)

    xs = x2[row_token]
    f = w_gu.shape[2] // 2
    tf = MOE_TF
    nf = f // tf
    h = pl.pallas_call(
        _moe_up_kernel,
        grid_spec=pltpu.PrefetchScalarGridSpec(
            num_scalar_prefetch=1,
            grid=(nf, n_tiles),
            in_specs=[pl.BlockSpec((tm, d), lambda j, i, te: (i, 0)),
                      pl.BlockSpec((1, d, tf), lambda j, i, te: (te[i], 0, j)),
                      pl.BlockSpec((1, d, tf), lambda j, i, te: (te[i], 0, j + nf))],
            out_specs=pl.BlockSpec((tm, tf), lambda j, i, te: (i, j)),
            scratch_shapes=[pltpu.VMEM((d, tf), MXU_DTYPE), pltpu.VMEM((d, tf), MXU_DTYPE)],
        ),
        out_shape=jax.ShapeDtypeStruct((n_rows, f), MXU_DTYPE),
        compiler_params=_params("arbitrary", "arbitrary"),
        name="moe_up",
    )(tile_expert, xs, w_gu, w_gu)
    ys = pl.pallas_call(
        _moe_down_kernel,
        grid_spec=pltpu.PrefetchScalarGridSpec(
            num_scalar_prefetch=1,
            grid=(n_tiles,),
            in_specs=[pl.BlockSpec((tm, f), lambda i, te: (i, 0)),
                      pl.BlockSpec((1, f, d), lambda i, te: (te[i], 0, 0)),
                      pl.BlockSpec((tm, 1), lambda i, te: (i, 0))],
            out_specs=pl.BlockSpec((tm, d), lambda i, te: (i, 0)),
            scratch_shapes=[pltpu.VMEM((f, d), MXU_DTYPE)],
        ),
        out_shape=jax.ShapeDtypeStruct((n_rows, d), F32),
        compiler_params=_params("arbitrary"),
        name="moe_down",
    )(tile_expert, h, w_down, row_w.reshape(n_rows, 1))

    lo_first = top_idx[:, 0:1] < top_idx[:, 1:2]
    p0 = jnp.where(lo_first[:, 0], pos[:, 0], pos[:, 1])
    p1 = jnp.where(lo_first[:, 0], pos[:, 1], pos[:, 0])
    y0 = ys[p0]
    y1 = ys[p1]
    return pl.pallas_call(
        _add_ln_kernel,
        grid=(m // 512,),
        in_specs=[pl.BlockSpec((512, d), lambda i: (i, 0))] * 3
        + [pl.BlockSpec((1, d), lambda i: (0, 0))] * 2,
        out_specs=pl.BlockSpec((512, d), lambda i: (i, 0)),
        out_shape=jax.ShapeDtypeStruct((m, d), F32),
        compiler_params=_params("parallel"),
        name="moe_add_ln",
    )(x2, y0, y1, ln_g.reshape(1, d), ln_b.reshape(1, d))


def kernel(x, l0_w_in, l0_cmp_pe_k, l0_cmp_pe_v, l0_cmp_wk1, l0_cmp_wk2, l0_cmp_wv1, l0_cmp_wv2, l0_w_o, l0_ln1_g, l0_ln1_b, l0_ffn_w_gu, l0_ffn_w_down, l0_ln2_g, l0_ln2_b, l1_w_in, l1_w_gate2, l1_b_gate2, l1_head_norm_g, l1_w_o, l1_ln1_g, l1_ln1_b, l1_router, l1_moe_w_gu, l1_moe_w_down, l1_ln2_g, l1_ln2_b):
    bsz, t, d = x.shape
    x2 = x.reshape(bsz * t, d)

    o = _nsa_mixer(x2, bsz, t, l0_w_in, l0_cmp_pe_k, l0_cmp_pe_v, l0_cmp_wk1, l0_cmp_wk2,
                   l0_cmp_wv1, l0_cmp_wv2)
    x2 = _matmul_res_ln(o, _mx(l0_w_o), x2, l0_ln1_g, l0_ln1_b, 512)
    hmid = _swiglu_up(x2, _mx(l0_ffn_w_gu), 512, FFN_DENSE // 2)
    x2 = _matmul_res_ln(hmid, _mx(l0_ffn_w_down), x2, l0_ln2_g, l0_ln2_b, 512)

    o = _gla_mixer(x2, bsz, t, l1_w_in, l1_w_gate2, l1_b_gate2, l1_head_norm_g)
    x2 = _matmul_res_ln(o, _mx(l1_w_o), x2, l1_ln1_g, l1_ln1_b, 512)
    x2 = _moe_layer(x2, l1_router, l1_moe_w_gu, l1_moe_w_down, l1_ln2_g, l1_ln2_b)
    return x2.reshape(bsz, t, d)
```

```python
import functools

import numpy as np
import jax
import jax.numpy as jnp
from jax import lax
from jax.experimental import pallas as pl
from jax.experimental.pallas import tpu as pltpu

F32 = jnp.float32
MXU_DTYPE = jnp.bfloat16

D_MODEL = 1024
DEPTH = 2
DN_ALPHA = (2 * DEPTH) ** 0.25
LN_EPS = 1e-5
RMS_EPS = 1e-6

NSA_HEADS = 16
NSA_HEAD_DIM = 64
NSA_KV_GROUPS = 4
NSA_HPG = NSA_HEADS // NSA_KV_GROUPS
NSA_KV_DIM = NSA_KV_GROUPS * NSA_HEAD_DIM
CMP_LEN = 32
CMP_STRIDE = 16
CMP_HIDDEN = 256
SLC_LEN = 64
N_SEL = 16
WINDOW = 512
QTILE = 128
KBLK = 128
NCOL = NSA_HPG * QTILE
CMP_CHUNK = 128
SEL_UNROLL = 8
LOG2E = 1.4426950408889634

GLA_HEADS = 4
GLA_KEY_DIM = D_MODEL // 2
GLA_VAL_DIM = D_MODEL
GLA_DK = GLA_KEY_DIM // GLA_HEADS
GLA_DV = GLA_VAL_DIM // GLA_HEADS
GLA_GATE_RANK = 16
GLA_GATE_NORM = 16.0
GLA_CHUNK = 64
GLA_TSTEP = 256

FFN_DENSE = 2816
N_EXPERTS = 8
TOP_K = 2
FFN_EXPERT = 3584
MOE_TM = 256
MOE_TF = 1792
SWIGLU_CHUNK = 256

LANE = 128
NEG = -1e30
VMEM_LIMIT = 56 * 1024 * 1024


def _params(*sem):
    return pltpu.CompilerParams(dimension_semantics=sem, vmem_limit_bytes=VMEM_LIMIT)


def _mx(a):
    return a.astype(MXU_DTYPE)


def _dot(a, b):
    return jnp.dot(a, b, preferred_element_type=F32)


def _dot_nt(a, b):
    return lax.dot_general(a, b, (((1,), (1,)), ((), ())), preferred_element_type=F32)


def _mm_kernel(x_ref, w_ref, o_ref):
    o_ref[...] = _dot(_mx(x_ref[...]), w_ref[...]).astype(o_ref.dtype)


def _matmul(x, w, tm, tn, out_dtype=F32):
    m, k = x.shape
    n = w.shape[1]
    return pl.pallas_call(
        _mm_kernel,
        grid=(m // tm, n // tn),
        in_specs=[pl.BlockSpec((tm, k), lambda i, j: (i, 0)),
                  pl.BlockSpec((k, tn), lambda i, j: (0, j))],
        out_specs=pl.BlockSpec((tm, tn), lambda i, j: (i, j)),
        out_shape=jax.ShapeDtypeStruct((m, n), out_dtype),
        compiler_params=_params("parallel", "parallel"),
        name="matmul",
    )(x, w)


def _mm_scaled_kernel(x_ref, w_ref, s_ref, o_ref):
    o_ref[...] = (_dot(_mx(x_ref[...]), w_ref[...]) * s_ref[...]).astype(o_ref.dtype)


def _matmul_scaled(x, w, col_scale, tm, tn, out_dtype):
    m, k = x.shape
    n = w.shape[1]
    return pl.pallas_call(
        _mm_scaled_kernel,
        grid=(m // tm, n // tn),
        in_specs=[pl.BlockSpec((tm, k), lambda i, j: (i, 0)),
                  pl.BlockSpec((k, tn), lambda i, j: (0, j)),
                  pl.BlockSpec((1, tn), lambda i, j: (0, j))],
        out_specs=pl.BlockSpec((tm, tn), lambda i, j: (i, j)),
        out_shape=jax.ShapeDtypeStruct((m, n), out_dtype),
        compiler_params=_params("parallel", "parallel"),
        name="matmul_scaled",
    )(x, w, col_scale)


def _layer_norm_rows(h, g, b):
    mu = jnp.mean(h, -1, keepdims=True)
    d = h - mu
    var = jnp.mean(d * d, -1, keepdims=True)
    return d * lax.rsqrt(var + LN_EPS) * g + b


def _mm_ln_kernel(x_ref, w_ref, r_ref, g_ref, b_ref, o_ref):
    y = _dot(_mx(x_ref[...]), w_ref[...])
    o_ref[...] = _layer_norm_rows(DN_ALPHA * r_ref[...] + y, g_ref[...], b_ref[...])


def _matmul_res_ln(x, w, res, g, b, tm):
    m, k = x.shape
    n = w.shape[1]
    return pl.pallas_call(
        _mm_ln_kernel,
        grid=(m // tm,),
        in_specs=[pl.BlockSpec((tm, k), lambda i: (i, 0)),
                  pl.BlockSpec((k, n), lambda i: (0, 0)),
                  pl.BlockSpec((tm, n), lambda i: (i, 0)),
                  pl.BlockSpec((1, n), lambda i: (0, 0)),
                  pl.BlockSpec((1, n), lambda i: (0, 0))],
        out_specs=pl.BlockSpec((tm, n), lambda i: (i, 0)),
        out_shape=jax.ShapeDtypeStruct((m, n), F32),
        compiler_params=_params("parallel"),
        name="matmul_res_ln",
    )(x, w, res, g.reshape(1, n), b.reshape(1, n))


def _swiglu_cols(x, wg_ref, wu_ref, o_ref):
    n = o_ref.shape[1]
    step = SWIGLU_CHUNK if n % SWIGLU_CHUNK == 0 else LANE
    for j in range(0, n, step):
        a = _dot(x, wg_ref[:, j:j + step])
        b = _dot(x, wu_ref[:, j:j + step])
        o_ref[:, j:j + step] = (a * jax.nn.sigmoid(a) * b).astype(o_ref.dtype)


def _swiglu_up_kernel(x_ref, wg_ref, wu_ref, o_ref):
    _swiglu_cols(_mx(x_ref[...]), wg_ref, wu_ref, o_ref)


def _swiglu_up(x, w_gu, tm, tn):
    m, k = x.shape
    f = w_gu.shape[1] // 2
    nj = f // tn
    return pl.pallas_call(
        _swiglu_up_kernel,
        grid=(m // tm, nj),
        in_specs=[pl.BlockSpec((tm, k), lambda i, j: (i, 0)),
                  pl.BlockSpec((k, tn), lambda i, j: (0, j)),
                  pl.BlockSpec((k, tn), lambda i, j: (0, j + nj))],
        out_specs=pl.BlockSpec((tm, tn), lambda i, j: (i, j)),
        out_shape=jax.ShapeDtypeStruct((m, f), MXU_DTYPE),
        compiler_params=_params("parallel", "parallel"),
        name="swiglu_up",
    )(x, w_gu, w_gu)


def _cmp_up_kernel(c_ref, pea_ref, peb_ref, wa_ref, wb_ref, p_ref, q_ref):
    c = c_ref[...]
    p_ref[...] = _dot(_mx(c + pea_ref[...]), wa_ref[...])
    q_ref[...] = _dot(_mx(c + peb_ref[...]), wb_ref[...])


def _cmp_down_kernel(p_ref, q_ref, w_ref, o_ref):
    h = jax.nn.gelu(p_ref[...] + q_ref[...])
    o_ref[...] = _dot(_mx(h), w_ref[...])


def _compress(a, pe, w1, w2):
    bsz, t, g, dk = a.shape
    nch = t // CMP_STRIDE
    half = CMP_STRIDE * dk
    chunks = a.reshape(bsz, nch, CMP_STRIDE, g, dk).transpose(0, 1, 3, 2, 4).reshape(bsz * nch * g, half)
    rows = chunks.shape[0]
    tm = min(512, rows)
    pe_a = pe[:CMP_STRIDE].reshape(1, half)
    pe_b = pe[CMP_STRIDE:].reshape(1, half)
    w1m = _mx(w1)
    p, q = pl.pallas_call(
        _cmp_up_kernel,
        grid=(rows // tm,),
        in_specs=[pl.BlockSpec((tm, half), lambda i: (i, 0)),
                  pl.BlockSpec((1, half), lambda i: (0, 0)),
                  pl.BlockSpec((1, half), lambda i: (0, 0)),
                  pl.BlockSpec((half, CMP_HIDDEN), lambda i: (0, 0)),
                  pl.BlockSpec((half, CMP_HIDDEN), lambda i: (0, 0))],
        out_specs=[pl.BlockSpec((tm, CMP_HIDDEN), lambda i: (i, 0)),
                   pl.BlockSpec((tm, CMP_HIDDEN), lambda i: (i, 0))],
        out_shape=[jax.ShapeDtypeStruct((rows, CMP_HIDDEN), F32)] * 2,
        compiler_params=_params("parallel"),
        name="cmp_up",
    )(chunks, pe_a, pe_b, w1m[:half], w1m[half:])
    q = q.reshape(bsz, nch, g, CMP_HIDDEN)
    q = jnp.concatenate([q[:, 1:], jnp.zeros_like(q[:, :1])], axis=1).reshape(rows, CMP_HIDDEN)
    w2p = jnp.pad(_mx(w2), ((0, 0), (0, LANE - dk)))
    out = pl.pallas_call(
        _cmp_down_kernel,
        grid=(rows // tm,),
        in_specs=[pl.BlockSpec((tm, CMP_HIDDEN), lambda i: (i, 0)),
                  pl.BlockSpec((tm, CMP_HIDDEN), lambda i: (i, 0)),
                  pl.BlockSpec((CMP_HIDDEN, LANE), lambda i: (0, 0))],
        out_specs=pl.BlockSpec((tm, LANE), lambda i: (i, 0)),
        out_shape=jax.ShapeDtypeStruct((rows, LANE), F32),
        compiler_params=_params("parallel"),
        name="cmp_down",
    )(p, q, w2p)
    out = out[:, :dk].reshape(bsz, nch, g, dk)
    valid = (jnp.arange(nch) < nch - 1)[None, :, None, None]
    return jnp.where(valid, out, 0.0)


def _queries_t(q_blk):
    qt = q_blk.astype(F32).T
    dk = NSA_HEAD_DIM
    return _mx(jnp.concatenate([qt[h * dk:(h + 1) * dk, :] for h in range(NSA_HPG)], axis=1))


def _nsa_cmp_kernel(q_ref, kc_ref, vct_ref, ovl_ref, slope_ref, ocmp_ref, sel_ref, flag_ref,
                    imp_ref, key_ref, cnt_ref):
    c = pl.program_id(2)
    ncp = kc_ref.shape[2]
    ns = ovl_ref.shape[0]
    qt = _queries_t(q_ref[...])
    chunk = min(CMP_CHUNK, ncp)
    n_chunks = (c * (QTILE // CMP_STRIDE) + (QTILE - CMP_LEN) // CMP_STRIDE + chunk) // chunk

    def cmp_branch(rows):
        s = _dot(kc_ref[0, 0, :rows, :], qt)
        n_i = lax.broadcasted_iota(jnp.int32, (rows, NCOL), 0)
        col = lax.broadcasted_iota(jnp.int32, (rows, NCOL), 1)
        tq = c * QTILE + (col & (QTILE - 1))
        dist = tq - (n_i * CMP_STRIDE + (CMP_LEN - 1))
        s = jnp.where(dist >= 0, s - slope_ref[0] * dist.astype(F32), NEG)
        m = jnp.max(s, axis=0, keepdims=True)
        e = jnp.exp2(s - m)
        l = jnp.sum(e, axis=0, keepdims=True)
        r = jnp.where(tq[0:1, :] >= CMP_LEN - 1, 1.0 / jnp.maximum(l, 1e-30), 0.0)
        p = e * r
        ocmp_ref[0, 0, 0] = _dot(vct_ref[0, 0, :, :rows], _mx(p))
        psum = p[:, 0:QTILE]
        for h in range(1, NSA_HPG):
            psum = psum + p[:, h * QTILE:(h + 1) * QTILE]
        hi = _mx(psum)
        lo = _mx(psum - hi.astype(F32))
        ovl = ovl_ref[:, :rows]
        imp_ref[...] = _dot(ovl, hi) + _dot(ovl, lo)

    for k in range(1, ncp // chunk + 1):
        pl.when(n_chunks == k)(functools.partial(cmp_branch, k * chunk))

    j_i = lax.broadcasted_iota(jnp.int32, (ns, QTILE), 0)
    tq2 = c * QTILE + lax.broadcasted_iota(jnp.int32, (ns, QTILE), 1)
    valid = j_i * SLC_LEN <= tq2
    cur = tq2 >> (SLC_LEN.bit_length() - 1)
    forced = valid & ((j_i == 0) | (j_i == cur) | (j_i == cur - 1))
    score = jnp.where(forced, 1e9, jnp.where(valid, imp_ref[...], -1e9))
    bits = lax.bitcast_convert_type(score, jnp.int32)
    key_ref[...] = jnp.where(bits < 0, bits ^ 0x7FFFFFFF, bits)
    cnt_ref[...] = jnp.zeros(cnt_ref.shape, jnp.int32)

    sub = lax.broadcasted_iota(jnp.int32, (8, QTILE), 0)
    n_it = (c * (QTILE // SLC_LEN) + QTILE // SLC_LEN + 7) // 8

    def it_body(it, carry):
        base = pl.multiple_of(it * 8, 8)
        rows8 = key_ref[pl.ds(base, 8), :]
        rb = [jnp.broadcast_to(rows8[il:il + 1, :], (8, QTILE)) for il in range(8)]
        for jt in range(ns // 8):
            js = slice(jt * 8, (jt + 1) * 8)
            thr = key_ref[js, :] - jnp.where(jt > it, 1, 0)
            acc = cnt_ref[js, :]
            for il in range(8):
                acc = acc + jnp.where(rb[il] > thr, 1, 0)
            cnt_ref[js, :] = acc
        corr = jnp.zeros((8, QTILE), jnp.int32)
        for il in range(8):
            corr = corr + jnp.where((rb[il] == rows8) & (sub > il), 1, 0)
        cnt_ref[pl.ds(base, 8), :] += corr
        return carry

    lax.fori_loop(0, n_it, it_body, 0)
    sel = jnp.where(cnt_ref[...] < min(N_SEL, ns), 1.0, 0.0)
    sel_ref[0, 0, 0] = sel
    flag_ref[0, 0, 0] = _dot_nt(jnp.ones((8, QTILE), MXU_DTYPE), _mx(sel))


def _nsa_cmp_select(proj, kcmp, vcmp_t, ovl_t, slopes):
    bsz, g, ncp, dk = kcmp.shape
    nt = proj.shape[0] // (bsz * QTILE)
    ns = ovl_t.shape[0]
    return pl.pallas_call(
        _nsa_cmp_kernel,
        grid=(bsz, g, nt),
        in_specs=[pl.BlockSpec((QTILE, NSA_HPG * dk), lambda b, gg, c: (b * nt + c, gg)),
                  pl.BlockSpec((1, 1, ncp, dk), lambda b, gg, c: (b, gg, 0, 0)),
                  pl.BlockSpec((1, 1, dk, ncp), lambda b, gg, c: (b, gg, 0, 0)),
                  pl.BlockSpec((ns, ncp), lambda b, gg, c: (0, 0)),
                  pl.BlockSpec((1, 1, NCOL), lambda b, gg, c: (gg, 0, 0))],
        out_specs=[pl.BlockSpec((1, 1, 1, dk, NCOL), lambda b, gg, c: (b, gg, c, 0, 0)),
                   pl.BlockSpec((1, 1, 1, ns, QTILE), lambda b, gg, c: (b, gg, c, 0, 0)),
                   pl.BlockSpec((1, 1, 1, 8, ns), lambda b, gg, c: (b, gg, c, 0, 0))],
        out_shape=[jax.ShapeDtypeStruct((bsz, g, nt, dk, NCOL), F32),
                   jax.ShapeDtypeStruct((bsz, g, nt, ns, QTILE), F32),
                   jax.ShapeDtypeStruct((bsz, g, nt, 8, ns), F32)],
        scratch_shapes=[pltpu.VMEM((ns, QTILE), F32), pltpu.VMEM((ns, QTILE), jnp.int32),
                        pltpu.VMEM((ns, QTILE), jnp.int32)],
        compiler_params=_params("parallel", "parallel", "parallel"),
        name="nsa_cmp_select",
    )(proj, kcmp, vcmp_t, ovl_t, slopes)


def _nsa_attn_kernel(lst_ref, cnt_ref, q_ref, qf_ref, ks_ref, vs_ref, kw_ref, vw_ref, pos_ref, sel_ref,
                     ocmp_ref, gl_ref, o_ref, ksa_sc, kwa_sc, vst_sc, vwt_sc, gate_sc, m_sc, l_sc, acc_sc):
    b = pl.program_id(0)
    g = pl.program_id(1)
    c = pl.program_id(2)
    nt = pl.num_programs(2)
    npair = ks_ref.shape[1] // KBLK
    dk = acc_sc.shape[0]

    @pl.when(c == 0)
    def _():
        def prep(j, carry):
            rows = pl.ds(pl.multiple_of(j * KBLK, KBLK), KBLK)
            feats = pos_ref[rows, :]
            ksa_sc[rows, :] = ks_ref[0, rows, :] + feats
            kwa_sc[rows, :] = kw_ref[0, rows, :] + feats
            vst_sc[j] = _mx(vs_ref[0, rows, :].astype(F32).T[:dk, :])
            vwt_sc[j] = _mx(vw_ref[0, rows, :].astype(F32).T[:dk, :])
            return carry

        lax.fori_loop(0, npair, prep, 0)

    qt = jnp.concatenate([_queries_t(q_ref[...]), qf_ref[0]], axis=0)
    key_i = lax.broadcasted_iota(jnp.int32, (KBLK, QTILE), 0)
    tq = c * QTILE + lax.broadcasted_iota(jnp.int32, (KBLK, QTILE), 1)
    heads = [slice(h * QTILE, (h + 1) * QTILE) for h in range(NSA_HPG)]

    m_sc[...] = jnp.full(m_sc.shape, NEG, F32)
    l_sc[...] = jnp.zeros(l_sc.shape, F32)
    acc_sc[...] = jnp.zeros(acc_sc.shape, F32)
    step = (b * NSA_KV_GROUPS + g) * nt + c

    def group_body(r, carry):
        blocks, masks, scores = [], [], []
        for u in range(SEL_UNROLL):
            jj = lst_ref[step * npair + r * SEL_UNROLL + u]
            js = jnp.maximum(jj, 0)
            kbase = jnp.where(jj >= 0, js * KBLK, nt * QTILE)
            rows = sel_ref[0, 0, 0, pl.ds(2 * js, 2), :]
            picked = jnp.where(key_i < SLC_LEN, rows[0:1, :], rows[1:2, :]) > 0.5
            blocks.append(js)
            masks.append(picked & (kbase + key_i <= tq))
            scores.append(_dot(ksa_sc[pl.ds(pl.multiple_of(js * KBLK, KBLK), KBLK), :], qt))
        sh = [[jnp.where(masks[u], scores[u][:, cs], NEG) for cs in heads] for u in range(SEL_UNROLL)]
        m_new, l_new = [], []
        for hi, cs in enumerate(heads):
            m_old = m_sc[:, cs]
            mx_ = m_old
            for u in range(SEL_UNROLL):
                mx_ = jnp.maximum(mx_, jnp.max(sh[u][hi], axis=0, keepdims=True))
            alpha = jnp.exp2(m_old - mx_)
            l_new.append(alpha * l_sc[:, cs])
            acc_sc[:, cs] = alpha * acc_sc[:, cs]
            m_sc[:, cs] = mx_
            m_new.append(mx_)
        upd = None
        for u in range(SEL_UNROLL):
            es = []
            for hi in range(NSA_HPG):
                e = jnp.exp2(sh[u][hi] - m_new[hi])
                l_new[hi] = l_new[hi] + jnp.sum(e, axis=0, keepdims=True)
                es.append(_mx(e))
            d = _dot(vst_sc[blocks[u]], jnp.concatenate(es, axis=1))
            upd = d if upd is None else upd + d
        for hi, cs in enumerate(heads):
            l_sc[:, cs] = l_new[hi]
        acc_sc[...] += upd
        return carry

    lax.fori_loop(0, (cnt_ref[step] + SEL_UNROLL - 1) // SEL_UNROLL, group_body, 0)
    o_sel = acc_sc[...] * (1.0 / jnp.maximum(l_sc[...], 1e-30))

    nwin = WINDOW // KBLK + 1
    blk0 = jnp.maximum(c - (nwin - 1), 0)
    s = _dot(kwa_sc[pl.ds(pl.multiple_of(blk0 * KBLK, KBLK), nwin * KBLK), :], qt)
    row = lax.broadcasted_iota(jnp.int32, (nwin * KBLK, QTILE), 0)
    dist = lax.broadcasted_iota(jnp.int32, (nwin * KBLK, QTILE), 1) + (c - blk0) * KBLK - row
    mask = (dist >= 0) & (dist < WINDOW)
    es, rs = [], []
    for cs in heads:
        sh = jnp.where(mask, s[:, cs], NEG)
        e = jnp.exp2(sh - jnp.max(sh, axis=0, keepdims=True))
        rs.append(1.0 / jnp.maximum(jnp.sum(e, axis=0, keepdims=True), 1e-30))
        es.append(_mx(e))
    e_all = jnp.concatenate(es, axis=1)
    acc = jnp.zeros((dk, NCOL), F32)
    for t in range(nwin):
        acc = acc + _dot(vwt_sc[blk0 + t], e_all[t * KBLK:(t + 1) * KBLK, :])
    o_win = acc * jnp.concatenate(rs, axis=1)

    gate_sc[...] = gl_ref[...].T
    ocmp = ocmp_ref[0, 0, 0]
    outs = []
    for h, cs in enumerate(heads):
        gt = jax.nn.sigmoid(gate_sc[pl.ds((g * NSA_HPG + h) * 3, 3), :])
        outs.append(gt[0:1] * ocmp[:, cs] + gt[1:2] * o_sel[:, cs] + gt[2:3] * o_win[:, cs])
    o_ref[...] = jnp.concatenate(outs, axis=0).T


def _nsa_attend(blk_list, blk_count, proj, proj_f, q_feat, pos_feat, sel_t, ocmp_t, bsz):
    g, dk = NSA_KV_GROUPS, NSA_HEAD_DIM
    t = proj.shape[0] // bsz
    nt = t // QTILE
    npair = t // KBLK
    ns = sel_t.shape[3]
    proj3 = proj.reshape(bsz, t, proj.shape[1])
    seg = D_MODEL // LANE
    gate_blk = proj_f.shape[1] // LANE - 1
    tile = lambda b, gg, c, lst, cnt: (b, gg, c, 0, 0)
    rows = lambda b, gg, c, lst, cnt: (b * nt + c, gg)
    kv = lambda k: (lambda b, gg, c, lst, cnt: (b, 0, seg + k * g + gg))
    grid_spec = pltpu.PrefetchScalarGridSpec(
        num_scalar_prefetch=2,
        grid=(bsz, g, nt),
        in_specs=[pl.BlockSpec((QTILE, NSA_HPG * dk), rows),
                  pl.BlockSpec((1, dk, NCOL), lambda b, gg, c, lst, cnt: (gg, 0, 0)),
                  pl.BlockSpec((1, t, LANE), kv(0)),
                  pl.BlockSpec((1, t, LANE), kv(1)),
                  pl.BlockSpec((1, t, LANE), kv(2)),
                  pl.BlockSpec((1, t, LANE), kv(3)),
                  pl.BlockSpec((t, LANE), lambda b, gg, c, lst, cnt: (0, 0)),
                  pl.BlockSpec((1, 1, 1, ns, QTILE), tile),
                  pl.BlockSpec((1, 1, 1, dk, NCOL), tile),
                  pl.BlockSpec((QTILE, LANE), lambda b, gg, c, lst, cnt: (b * nt + c, gate_blk))],
        out_specs=pl.BlockSpec((QTILE, NSA_HPG * dk), rows),
        scratch_shapes=[pltpu.VMEM((t, LANE), MXU_DTYPE), pltpu.VMEM((t, LANE), MXU_DTYPE),
                        pltpu.VMEM((npair, dk, KBLK), MXU_DTYPE), pltpu.VMEM((npair, dk, KBLK), MXU_DTYPE),
                        pltpu.VMEM((LANE, QTILE), F32),
                        pltpu.VMEM((1, NCOL), F32), pltpu.VMEM((1, NCOL), F32),
                        pltpu.VMEM((dk, NCOL), F32)],
    )
    return pl.pallas_call(
        _nsa_attn_kernel,
        grid_spec=grid_spec,
        out_shape=jax.ShapeDtypeStruct((bsz * t, NSA_HEADS * dk), F32),
        compiler_params=_params("parallel", "parallel", "arbitrary"),
        name="nsa_attend",
    )(blk_list, blk_count, proj, q_feat, proj3, proj3, proj3, proj3, pos_feat, sel_t, ocmp_t, proj_f)


def _nsa_mixer(x2, bsz, t, w_in, pe_k, pe_v, wk1, wk2, wv1, wv2):
    g, hpg, dk = NSA_KV_GROUPS, NSA_HPG, NSA_HEAD_DIM
    nt = t // QTILE
    d = w_in.shape[0]
    splits = np.cumsum([D_MODEL] + [NSA_KV_DIM] * 6).tolist()
    wq, wkc, wvc, wks, wvs, wkw, wvw, wgl = jnp.split(w_in, splits, axis=1)

    def lane_groups(w):
        return jnp.pad(w.reshape(d, g, dk), ((0, 0), (0, 0), (0, LANE - dk))).reshape(d, g * LANE)

    w_a = _mx(jnp.concatenate([wq] + [lane_groups(w) for w in (wks, wvs, wkw, wvw)], axis=1))
    scale = jnp.concatenate([jnp.full((1, D_MODEL), dk ** -0.5 * LOG2E, F32),
                             jnp.ones((1, w_a.shape[1] - D_MODEL), F32)], axis=1)
    w_b = _mx(jnp.concatenate([wkc, wvc, jnp.pad(wgl, ((0, 0), (0, LANE - wgl.shape[1])))], axis=1))
    proj = _matmul_scaled(x2, w_a, scale, 512, 1024, MXU_DTYPE)
    proj_f = _matmul(x2, w_b, 512, w_b.shape[1])
    kc = proj_f[:, :NSA_KV_DIM]
    vc = proj_f[:, NSA_KV_DIM:2 * NSA_KV_DIM]

    slopes = 2.0 ** (-8.0 * (jnp.arange(NSA_HEADS, dtype=F32) + 1.0) / NSA_HEADS) * LOG2E
    slopes = jnp.broadcast_to(slopes.reshape(g, 1, hpg, 1), (g, 1, hpg, QTILE)).reshape(g, 1, NCOL)
    s1 = slopes.astype(jnp.bfloat16)
    s2 = (slopes - s1.astype(F32)).astype(jnp.bfloat16)
    s3 = (slopes - s1.astype(F32) - s2.astype(F32)).astype(jnp.bfloat16)
    q_feat = jnp.concatenate([s1 * SLC_LEN, s2 * SLC_LEN, s3 * SLC_LEN, s1, s2, s3]
                             + [jnp.zeros_like(s1)] * (dk - 6), axis=1).astype(MXU_DTYPE)
    kpos = jnp.arange(t, dtype=jnp.int32)
    ka_, kb_ = (kpos // SLC_LEN).astype(jnp.bfloat16), (kpos % SLC_LEN).astype(jnp.bfloat16)
    zero = jnp.zeros_like(ka_)
    pos_feat = jnp.stack([zero] * dk + [ka_, ka_, ka_, kb_, kb_, kb_] + [zero] * (LANE - dk - 6),
                         axis=1).astype(MXU_DTYPE)

    four = lambda a: a.reshape(bsz, t, g, dk)
    k_cmp = _compress(four(kc), pe_k, wk1, wk2)
    v_cmp = _compress(four(vc), pe_v, wv1, wv2)
    kcmp = _mx(k_cmp).transpose(0, 2, 1, 3)
    vcmp_t = _mx(v_cmp).transpose(0, 2, 3, 1)

    ncp = t // CMP_STRIDE
    ns = t // SLC_LEN
    cs = np.arange(ncp) * CMP_STRIDE
    ce = cs + CMP_LEN - 1
    bs = np.arange(ns) * SLC_LEN
    be = bs + SLC_LEN - 1
    ovl_t = ((cs[None, :] <= be[:, None]) & (ce[None, :] >= bs[:, None])).astype(np.float32)
    ovl_t[:, ncp - 1] = 0.0
    ovl_t = jnp.asarray(ovl_t, MXU_DTYPE)

    ocmp_t, sel_t, counts = _nsa_cmp_select(proj, kcmp, vcmp_t, ovl_t, slopes)
    npair = t // KBLK
    picked = counts[:, :, :, 0, :] > 0.5
    jj = jnp.arange(npair)
    own = jj[None, :] == jnp.arange(nt)[:, None]
    active = ((picked[..., 0::2] | picked[..., 1::2]) & (jj[None, :] < jnp.arange(nt)[:, None])) | own
    blk_count = jnp.sum(active, axis=-1).astype(jnp.int32)
    rank = jnp.cumsum(active, axis=-1) - 1
    hit = active[..., None, :] & (rank[..., None, :] == jj[:, None])
    blk_list = jnp.sum(jnp.where(hit, jj, 0), axis=-1)
    blk_list = jnp.where(jj < blk_count[..., None], blk_list, -1).astype(jnp.int32)

    return _nsa_attend(blk_list.reshape(-1), blk_count.reshape(-1), proj, proj_f, q_feat, pos_feat,
                       sel_t, ocmp_t, bsz)


def _log_sigmoid(z):
    return jnp.minimum(z, 0.0) - jnp.log1p(jnp.exp(-jnp.abs(z)))


def _gla_kernel(q_ref, k_ref, v_ref, g_ref, al_ref, w2_ref, b2_ref, hn_ref, o_ref, st_sc):
    @pl.when(pl.program_id(0) == 0)
    def _():
        st_sc[...] = jnp.zeros(st_sc.shape, F32)

    cc = GLA_CHUNK
    r_i = lax.broadcasted_iota(jnp.int32, (cc, cc), 0)
    c_i = lax.broadcasted_iota(jnp.int32, (cc, cc), 1)
    causal = r_i >= c_i
    tri = jnp.where(causal, 1.0, 0.0).astype(MXU_DTYPE)
    ref = cc // 2

    def chunk(ci, carry):
        rows = pl.ds(pl.multiple_of(ci * cc, cc), cc)
        for b in range(q_ref.shape[0]):
            z = _dot(_mx(al_ref[b, rows, :]), w2_ref[...]) + b2_ref[...]
            log_a = _log_sigmoid(z) / GLA_GATE_NORM
            a1 = _mx(log_a)
            r1 = log_a - a1.astype(F32)
            a2 = _mx(r1)
            a3 = _mx(r1 - a2.astype(F32))
            bc_all = _dot(tri, a1) + _dot(tri, a2) + _dot(tri, a3)
            for h in range(GLA_HEADS):
                ks = slice(h * GLA_DK, (h + 1) * GLA_DK)
                vs = slice(h * GLA_DV, (h + 1) * GLA_DV)
                bc = bc_all[:, ks]
                q = q_ref[b, rows, ks] * GLA_DK ** -0.5
                k = k_ref[b, rows, ks]
                v = v_ref[b, rows, vs]
                b_ref = bc[ref:ref + 1, :]
                b_last = bc[cc - 1:cc, :]
                a = _dot_nt(_mx(q * jnp.exp(bc - b_ref)), _mx(k * jnp.exp(b_ref - bc)))
                a = jnp.where(causal, a, 0.0)
                st = st_sc[b * GLA_HEADS + h]
                o = _dot(_mx(a), _mx(v)) + _dot_nt(_mx(q * jnp.exp(bc)), _mx(st))
                kl = k * jnp.exp(b_last - bc)
                st_sc[b * GLA_HEADS + h] = st * jnp.exp(b_last) + _dot(_mx(v.T), _mx(kl))
                o = o * lax.rsqrt(jnp.mean(o * o, -1, keepdims=True) + RMS_EPS)
                gg = g_ref[b, rows, vs]
                o_ref[b, rows, vs] = o * hn_ref[:, vs] * (gg * jax.nn.sigmoid(gg))
        return carry

    lax.fori_loop(0, q_ref.shape[1] // cc, chunk, 0)


def _gla_mixer(x2, bsz, t, w_in, w_gate2, b_gate2, head_norm_g):
    n_main = 2 * GLA_KEY_DIM + 2 * GLA_VAL_DIM
    w_main = _mx(w_in[:, :n_main])
    w_low = jnp.pad(_mx(w_in[:, n_main:]), ((0, 0), (0, LANE - GLA_GATE_RANK)))
    proj = _matmul(x2, w_main, 512, 1024).reshape(bsz, t, n_main)
    a_low = _matmul(x2, w_low, 512, LANE).reshape(bsz, t, LANE)
    w2p = jnp.pad(_mx(w_gate2), ((0, LANE - GLA_GATE_RANK), (0, 0)))
    ts = min(GLA_TSTEP, t)
    out = pl.pallas_call(
        _gla_kernel,
        grid=(t // ts,),
        in_specs=[pl.BlockSpec((bsz, ts, GLA_KEY_DIM), lambda s: (0, s, 0)),
                  pl.BlockSpec((bsz, ts, GLA_KEY_DIM), lambda s: (0, s, 1)),
                  pl.BlockSpec((bsz, ts, GLA_VAL_DIM), lambda s: (0, s, 1)),
                  pl.BlockSpec((bsz, ts, GLA_VAL_DIM), lambda s: (0, s, 2)),
                  pl.BlockSpec((bsz, ts, LANE), lambda s: (0, s, 0)),
                  pl.BlockSpec((LANE, GLA_KEY_DIM), lambda s: (0, 0)),
                  pl.BlockSpec((1, GLA_KEY_DIM), lambda s: (0, 0)),
                  pl.BlockSpec((1, GLA_VAL_DIM), lambda s: (0, 0))],
        out_specs=pl.BlockSpec((bsz, ts, GLA_VAL_DIM), lambda s: (0, s, 0)),
        out_shape=jax.ShapeDtypeStruct((bsz, t, GLA_VAL_DIM), F32),
        scratch_shapes=[pltpu.VMEM((bsz * GLA_HEADS, GLA_DV, GLA_DK), F32)],
        compiler_params=_params("arbitrary"),
        name="gla",
    )(proj, proj, proj, proj, a_low, w2p, b_gate2.reshape(1, -1), head_norm_g.reshape(1, -1))
    return out.reshape(bsz * t, GLA_VAL_DIM)


def _router_kernel(x_ref, wh_ref, wl_ref, o_ref):
    x = x_ref[...]
    xh = _mx(x)
    xl = _mx(x - xh.astype(F32))
    wh = wh_ref[...]
    logits = _dot(xh, wh) + _dot(xl, wh) + _dot(xh, wl_ref[...])
    lane = lax.broadcasted_iota(jnp.int32, logits.shape, 1)
    lg = jnp.where(lane < N_EXPERTS, logits, -jnp.inf)
    m1 = jnp.max(lg, axis=1, keepdims=True)
    i1 = jnp.min(jnp.where(lg == m1, lane, LANE), axis=1, keepdims=True)
    lg2 = jnp.where(lane == i1, -jnp.inf, lg)
    m2 = jnp.max(lg2, axis=1, keepdims=True)
    i2 = jnp.min(jnp.where(lg2 == m2, lane, LANE), axis=1, keepdims=True)
    e2 = jnp.exp(m2 - m1)
    den = 1.0 + e2
    w1 = 1.0 / den
    w2 = e2 / den
    out = jnp.where(lane == N_EXPERTS, i1.astype(F32), 0.0)
    out = jnp.where(lane == N_EXPERTS + 1, i2.astype(F32), out)
    out = jnp.where(lane == N_EXPERTS + 2, w1, out)
    out = jnp.where(lane == N_EXPERTS + 3, w2, out)
    o_ref[...] = out


def _moe_up_kernel(te_ref, x_ref, wg_ref, wu_ref, o_ref, wg_sc, wu_sc):
    i = pl.program_id(1)

    @pl.when((i == 0) | (te_ref[i] != te_ref[jnp.maximum(i - 1, 0)]))
    def _():
        wg_sc[...] = _mx(wg_ref[0])
        wu_sc[...] = _mx(wu_ref[0])

    _swiglu_cols(_mx(x_ref[...]), wg_sc, wu_sc, o_ref)


def _moe_down_kernel(te_ref, h_ref, w_ref, rw_ref, o_ref, w_sc):
    i = pl.program_id(0)

    @pl.when((i == 0) | (te_ref[i] != te_ref[jnp.maximum(i - 1, 0)]))
    def _():
        w_sc[...] = _mx(w_ref[0])

    o_ref[...] = _dot(h_ref[...], w_sc[...]) * rw_ref[...]


def _add_ln_kernel(x_ref, y0_ref, y1_ref, g_ref, b_ref, o_ref):
    h = DN_ALPHA * x_ref[...] + (y0_ref[...] + y1_ref[...])
    o_ref[...] = _layer_norm_rows(h, g_ref[...], b_ref[...])


def _moe_layer(x2, w_router, w_gu, w_down, ln_g, ln_b):
    m, d = x2.shape
    tm = MOE_TM
    wr = jnp.pad(w_router, ((0, 0), (0, LANE - N_EXPERTS)))
    wr_hi = _mx(wr)
    wr_lo = _mx(wr - wr_hi.astype(F32))
    rt = pl.pallas_call(
        _router_kernel,
        grid=(m // 512,),
        in_specs=[pl.BlockSpec((512, d), lambda i: (i, 0)),
                  pl.BlockSpec((d, LANE), lambda i: (0, 0)),
                  pl.BlockSpec((d, LANE), lambda i: (0, 0))],
        out_specs=pl.BlockSpec((512, LANE), lambda i: (i, 0)),
        out_shape=jax.ShapeDtypeStruct((m, LANE), F32),
        compiler_params=_params("parallel"),
        name="moe_router",
    )(x2, wr_hi, wr_lo)
    top_idx = rt[:, N_EXPERTS:N_EXPERTS + 2].astype(jnp.int32)
    top_w = rt[:, N_EXPERTS + 2:N_EXPERTS + 4]

    n_asg = m * TOP_K
    n_rows = n_asg + N_EXPERTS * tm
    n_tiles = n_rows // tm
    e_flat = top_idx.reshape(-1)
    order = jnp.argsort(e_flat, stable=True).astype(jnp.int32)
    slot = jnp.argsort(order).astype(jnp.int32)
    counts = jnp.sum((e_flat[:, None] == jnp.arange(N_EXPERTS)[None, :]).astype(jnp.int32), axis=0)
    tiles_per = (counts + tm - 1) // tm
    tile_end = jnp.cumsum(tiles_per)
    row_start = (tile_end - tiles_per) * tm
    grp_start = jnp.cumsum(counts) - counts
    tile_expert = jnp.minimum(jnp.sum((jnp.arange(n_tiles)[:, None] >= tile_end[None, :]).astype(jnp.int32),
                                      axis=1), N_EXPERTS - 1).astype(jnp.int32)
    pos = (row_start[e_flat] + slot - grp_start[e_flat]).astype(jnp.int32).reshape(m, TOP_K)
    row_e = jnp.repeat(tile_expert, tm)
    row_off = jnp.arange(n_rows, dtype=jnp.int32) - row_start[row_e].astype(jnp.int32)
    row_live = row_off < counts[row_e]
    row_asg = order[jnp.clip(grp_start[row_e].astype(jnp.int32) + row_off, 0, n_asg - 1)]
    row_token = jnp.where(row_live, row_asg // TOP_K, 0)
    row_w = jnp.where(row_live, top_w.reshape(-1)[row_asg], 0.0)

    xs = x2[row_token]
    f = w_gu.shape[2] // 2
    tf = MOE_TF
    nf = f // tf
    h = pl.pallas_call(
        _moe_up_kernel,
        grid_spec=pltpu.PrefetchScalarGridSpec(
            num_scalar_prefetch=1,
            grid=(nf, n_tiles),
            in_specs=[pl.BlockSpec((tm, d), lambda j, i, te: (i, 0)),
                      pl.BlockSpec((1, d, tf), lambda j, i, te: (te[i], 0, j)),
                      pl.BlockSpec((1, d, tf), lambda j, i, te: (te[i], 0, j + nf))],
            out_specs=pl.BlockSpec((tm, tf), lambda j, i, te: (i, j)),
            scratch_shapes=[pltpu.VMEM((d, tf), MXU_DTYPE), pltpu.VMEM((d, tf), MXU_DTYPE)],
        ),
        out_shape=jax.ShapeDtypeStruct((n_rows, f), MXU_DTYPE),
        compiler_params=_params("arbitrary", "arbitrary"),
        name="moe_up",
    )(tile_expert, xs, w_gu, w_gu)
    ys = pl.pallas_call(
        _moe_down_kernel,
        grid_spec=pltpu.PrefetchScalarGridSpec(
            num_scalar_prefetch=1,
            grid=(n_tiles,),
            in_specs=[pl.BlockSpec((tm, f), lambda i, te: (i, 0)),
                      pl.BlockSpec((1, f, d), lambda i, te: (te[i], 0, 0)),
                      pl.BlockSpec((tm, 1), lambda i, te: (i, 0))],
            out_specs=pl.BlockSpec((tm, d), lambda i, te: (i, 0)),
            scratch_shapes=[pltpu.VMEM((f, d), MXU_DTYPE)],
        ),
        out_shape=jax.ShapeDtypeStruct((n_rows, d), F32),
        compiler_params=_params("arbitrary"),
        name="moe_down",
    )(tile_expert, h, w_down, row_w.reshape(n_rows, 1))

    lo_first = top_idx[:, 0:1] < top_idx[:, 1:2]
    p0 = jnp.where(lo_first[:, 0], pos[:, 0], pos[:, 1])
    p1 = jnp.where(lo_first[:, 0], pos[:, 1], pos[:, 0])
    y0 = ys[p0]
    y1 = ys[p1]
    return pl.pallas_call(
        _add_ln_kernel,
        grid=(m // 512,),
        in_specs=[pl.BlockSpec((512, d), lambda i: (i, 0))] * 3
        + [pl.BlockSpec((1, d), lambda i: (0, 0))] * 2,
        out_specs=pl.BlockSpec((512, d), lambda i: (i, 0)),
        out_shape=jax.ShapeDtypeStruct((m, d), F32),
        compiler_params=_params("parallel"),
        name="moe_add_ln",
    )(x2, y0, y1, ln_g.reshape(1, d), ln_b.reshape(1, d))


def kernel(x, l0_w_in, l0_cmp_pe_k, l0_cmp_pe_v, l0_cmp_wk1, l0_cmp_wk2, l0_cmp_wv1, l0_cmp_wv2, l0_w_o, l0_ln1_g, l0_ln1_b, l0_ffn_w_gu, l0_ffn_w_down, l0_ln2_g, l0_ln2_b, l1_w_in, l1_w_gate2, l1_b_gate2, l1_head_norm_g, l1_w_o, l1_ln1_g, l1_ln1_b, l1_router, l1_moe_w_gu, l1_moe_w_down, l1_ln2_g, l1_ln2_b):
    bsz, t, d = x.shape
    x2 = x.reshape(bsz * t, d)

    o = _nsa_mixer(x2, bsz, t, l0_w_in, l0_cmp_pe_k, l0_cmp_pe_v, l0_cmp_wk1, l0_cmp_wk2,
                   l0_cmp_wv1, l0_cmp_wv2)
    x2 = _matmul_res_ln(o, _mx(l0_w_o), x2, l0_ln1_g, l0_ln1_b, 512)
    hmid = _swiglu_up(x2, _mx(l0_ffn_w_gu), 512, FFN_DENSE)
    x2 = _matmul_res_ln(hmid, _mx(l0_ffn_w_down), x2, l0_ln2_g, l0_ln2_b, 512)

    o = _gla_mixer(x2, bsz, t, l1_w_in, l1_w_gate2, l1_b_gate2, l1_head_norm_g)
    x2 = _matmul_res_ln(o, _mx(l1_w_o), x2, l1_ln1_g, l1_ln1_b, 512)
    x2 = _moe_layer(x2, l1_router, l1_moe_w_gu, l1_moe_w_down, l1_ln2_g, l1_ln2_b)
    return x2.reshape(bsz, t, d)
```

```python
import functools

import numpy as np
import jax
import jax.numpy as jnp
from jax import lax
from jax.experimental import pallas as pl
from jax.experimental.pallas import tpu as pltpu

F32 = jnp.float32
MXU_DTYPE = jnp.bfloat16

D_MODEL = 1024
DEPTH = 2
DN_ALPHA = (2 * DEPTH) ** 0.25
LN_EPS = 1e-5
RMS_EPS = 1e-6

NSA_HEADS = 16
NSA_HEAD_DIM = 64
NSA_KV_GROUPS = 4
NSA_HPG = NSA_HEADS // NSA_KV_GROUPS
NSA_KV_DIM = NSA_KV_GROUPS * NSA_HEAD_DIM
CMP_LEN = 32
CMP_STRIDE = 16
CMP_HIDDEN = 256
SLC_LEN = 64
N_SEL = 16
WINDOW = 512
QTILE = 128
KBLK = 128
NCOL = NSA_HPG * QTILE
CMP_CHUNK = 128
SEL_UNROLL = 4
VT_ROWS = 80
LOG2E = 1.4426950408889634

GLA_HEADS = 4
GLA_KEY_DIM = D_MODEL // 2
GLA_VAL_DIM = D_MODEL
GLA_DK = GLA_KEY_DIM // GLA_HEADS
GLA_DV = GLA_VAL_DIM // GLA_HEADS
GLA_GATE_RANK = 16
GLA_GATE_NORM = 16.0
GLA_CHUNK = 64
GLA_TSTEP = 256

FFN_DENSE = 2816
N_EXPERTS = 8
TOP_K = 2
FFN_EXPERT = 3584
MOE_TM = 256
MOE_TF = 1792
SWIGLU_CHUNK = 256

LANE = 128
NEG = -1e30
VMEM_LIMIT = 56 * 1024 * 1024


def _params(*sem):
    return pltpu.CompilerParams(dimension_semantics=sem, vmem_limit_bytes=VMEM_LIMIT)


def _mx(a):
    return a.astype(MXU_DTYPE)


def _dot(a, b):
    return jnp.dot(a, b, preferred_element_type=F32)


def _dot_nt(a, b):
    return lax.dot_general(a, b, (((1,), (1,)), ((), ())), preferred_element_type=F32)


def _mm_kernel(x_ref, w_ref, o_ref):
    o_ref[...] = _dot(_mx(x_ref[...]), w_ref[...]).astype(o_ref.dtype)


def _matmul(x, w, tm, tn, out_dtype=F32):
    m, k = x.shape
    n = w.shape[1]
    return pl.pallas_call(
        _mm_kernel,
        grid=(m // tm, n // tn),
        in_specs=[pl.BlockSpec((tm, k), lambda i, j: (i, 0)),
                  pl.BlockSpec((k, tn), lambda i, j: (0, j))],
        out_specs=pl.BlockSpec((tm, tn), lambda i, j: (i, j)),
        out_shape=jax.ShapeDtypeStruct((m, n), out_dtype),
        compiler_params=_params("parallel", "parallel"),
        name="matmul",
    )(x, w)


def _mm_scaled_kernel(x_ref, w_ref, s_ref, o_ref):
    o_ref[...] = (_dot(_mx(x_ref[...]), w_ref[...]) * s_ref[...]).astype(o_ref.dtype)


def _matmul_scaled(x, w, col_scale, tm, tn, out_dtype):
    m, k = x.shape
    n = w.shape[1]
    return pl.pallas_call(
        _mm_scaled_kernel,
        grid=(m // tm, n // tn),
        in_specs=[pl.BlockSpec((tm, k), lambda i, j: (i, 0)),
                  pl.BlockSpec((k, tn), lambda i, j: (0, j)),
                  pl.BlockSpec((1, tn), lambda i, j: (0, j))],
        out_specs=pl.BlockSpec((tm, tn), lambda i, j: (i, j)),
        out_shape=jax.ShapeDtypeStruct((m, n), out_dtype),
        compiler_params=_params("parallel", "parallel"),
        name="matmul_scaled",
    )(x, w, col_scale)


def _layer_norm_rows(h, g, b):
    mu = jnp.mean(h, -1, keepdims=True)
    d = h - mu
    var = jnp.mean(d * d, -1, keepdims=True)
    return d * lax.rsqrt(var + LN_EPS) * g + b


def _mm_ln_kernel(x_ref, w_ref, r_ref, g_ref, b_ref, o_ref):
    y = _dot(_mx(x_ref[...]), w_ref[...])
    o_ref[...] = _layer_norm_rows(DN_ALPHA * r_ref[...] + y, g_ref[...], b_ref[...])


def _matmul_res_ln(x, w, res, g, b, tm):
    m, k = x.shape
    n = w.shape[1]
    return pl.pallas_call(
        _mm_ln_kernel,
        grid=(m // tm,),
        in_specs=[pl.BlockSpec((tm, k), lambda i: (i, 0)),
                  pl.BlockSpec((k, n), lambda i: (0, 0)),
                  pl.BlockSpec((tm, n), lambda i: (i, 0)),
                  pl.BlockSpec((1, n), lambda i: (0, 0)),
                  pl.BlockSpec((1, n), lambda i: (0, 0))],
        out_specs=pl.BlockSpec((tm, n), lambda i: (i, 0)),
        out_shape=jax.ShapeDtypeStruct((m, n), F32),
        compiler_params=_params("parallel"),
        name="matmul_res_ln",
    )(x, w, res, g.reshape(1, n), b.reshape(1, n))


def _swiglu_cols(x, wg_ref, wu_ref, o_ref):
    n = o_ref.shape[1]
    step = SWIGLU_CHUNK if n % SWIGLU_CHUNK == 0 else LANE
    for j in range(0, n, step):
        a = _dot(x, wg_ref[:, j:j + step])
        b = _dot(x, wu_ref[:, j:j + step])
        o_ref[:, j:j + step] = (a * jax.nn.sigmoid(a) * b).astype(o_ref.dtype)


def _swiglu_up_kernel(x_ref, wg_ref, wu_ref, o_ref):
    _swiglu_cols(_mx(x_ref[...]), wg_ref, wu_ref, o_ref)


def _swiglu_up(x, w_gu, tm, tn):
    m, k = x.shape
    f = w_gu.shape[1] // 2
    nj = f // tn
    return pl.pallas_call(
        _swiglu_up_kernel,
        grid=(m // tm, nj),
        in_specs=[pl.BlockSpec((tm, k), lambda i, j: (i, 0)),
                  pl.BlockSpec((k, tn), lambda i, j: (0, j)),
                  pl.BlockSpec((k, tn), lambda i, j: (0, j + nj))],
        out_specs=pl.BlockSpec((tm, tn), lambda i, j: (i, j)),
        out_shape=jax.ShapeDtypeStruct((m, f), MXU_DTYPE),
        compiler_params=_params("parallel", "parallel"),
        name="swiglu_up",
    )(x, w_gu, w_gu)


def _cmp_up_kernel(c_ref, pea_ref, peb_ref, wa_ref, wb_ref, p_ref, q_ref):
    c = c_ref[...]
    p_ref[...] = _dot(_mx(c + pea_ref[...]), wa_ref[...])
    q_ref[...] = _dot(_mx(c + peb_ref[...]), wb_ref[...])


def _cmp_down_kernel(p_ref, q_ref, w_ref, o_ref):
    h = jax.nn.gelu(p_ref[...] + q_ref[...])
    o_ref[...] = _dot(_mx(h), w_ref[...])


def _compress(a, pe, w1, w2):
    bsz, t, g, dk = a.shape
    nch = t // CMP_STRIDE
    half = CMP_STRIDE * dk
    chunks = a.reshape(bsz, nch, CMP_STRIDE, g, dk).transpose(0, 1, 3, 2, 4).reshape(bsz * nch * g, half)
    rows = chunks.shape[0]
    tm = min(512, rows)
    pe_a = pe[:CMP_STRIDE].reshape(1, half)
    pe_b = pe[CMP_STRIDE:].reshape(1, half)
    w1m = _mx(w1)
    p, q = pl.pallas_call(
        _cmp_up_kernel,
        grid=(rows // tm,),
        in_specs=[pl.BlockSpec((tm, half), lambda i: (i, 0)),
                  pl.BlockSpec((1, half), lambda i: (0, 0)),
                  pl.BlockSpec((1, half), lambda i: (0, 0)),
                  pl.BlockSpec((half, CMP_HIDDEN), lambda i: (0, 0)),
                  pl.BlockSpec((half, CMP_HIDDEN), lambda i: (0, 0))],
        out_specs=[pl.BlockSpec((tm, CMP_HIDDEN), lambda i: (i, 0)),
                   pl.BlockSpec((tm, CMP_HIDDEN), lambda i: (i, 0))],
        out_shape=[jax.ShapeDtypeStruct((rows, CMP_HIDDEN), F32)] * 2,
        compiler_params=_params("parallel"),
        name="cmp_up",
    )(chunks, pe_a, pe_b, w1m[:half], w1m[half:])
    q = q.reshape(bsz, nch, g, CMP_HIDDEN)
    q = jnp.concatenate([q[:, 1:], jnp.zeros_like(q[:, :1])], axis=1).reshape(rows, CMP_HIDDEN)
    w2p = jnp.pad(_mx(w2), ((0, 0), (0, LANE - dk)))
    out = pl.pallas_call(
        _cmp_down_kernel,
        grid=(rows // tm,),
        in_specs=[pl.BlockSpec((tm, CMP_HIDDEN), lambda i: (i, 0)),
                  pl.BlockSpec((tm, CMP_HIDDEN), lambda i: (i, 0)),
                  pl.BlockSpec((CMP_HIDDEN, LANE), lambda i: (0, 0))],
        out_specs=pl.BlockSpec((tm, LANE), lambda i: (i, 0)),
        out_shape=jax.ShapeDtypeStruct((rows, LANE), F32),
        compiler_params=_params("parallel"),
        name="cmp_down",
    )(p, q, w2p)
    out = out[:, :dk].reshape(bsz, nch, g, dk)
    valid = (jnp.arange(nch) < nch - 1)[None, :, None, None]
    return jnp.where(valid, out, 0.0)


def _queries_t(q_blk):
    qt = q_blk.astype(F32).T
    dk = NSA_HEAD_DIM
    return _mx(jnp.concatenate([qt[h * dk:(h + 1) * dk, :] for h in range(NSA_HPG)], axis=1))


def _nsa_cmp_kernel(q_ref, kc_ref, vct_ref, ovl_ref, slope_ref, ocmp_ref, sel_ref, flag_ref,
                    imp_ref, key_ref, cnt_ref):
    c = pl.program_id(2)
    ncp = kc_ref.shape[2]
    ns = ovl_ref.shape[0]
    qt = _queries_t(q_ref[...])
    chunk = min(CMP_CHUNK, ncp)
    n_chunks = (c * (QTILE // CMP_STRIDE) + (QTILE - CMP_LEN) // CMP_STRIDE + chunk) // chunk

    def cmp_branch(rows):
        s = _dot(kc_ref[0, 0, :rows, :], qt)
        n_i = lax.broadcasted_iota(jnp.int32, (rows, NCOL), 0)
        col = lax.broadcasted_iota(jnp.int32, (rows, NCOL), 1)
        tq = c * QTILE + (col & (QTILE - 1))
        dist = tq - (n_i * CMP_STRIDE + (CMP_LEN - 1))
        s = jnp.where(dist >= 0, s - slope_ref[0] * dist.astype(F32), NEG)
        m = jnp.max(s, axis=0, keepdims=True)
        e = jnp.exp2(s - m)
        l = jnp.sum(e, axis=0, keepdims=True)
        r = jnp.where(tq[0:1, :] >= CMP_LEN - 1, 1.0 / jnp.maximum(l, 1e-30), 0.0)
        p = e * r
        ocmp_ref[0, 0, 0] = _dot(vct_ref[0, 0, :, :rows], _mx(p))
        psum = p[:, 0:QTILE]
        for h in range(1, NSA_HPG):
            psum = psum + p[:, h * QTILE:(h + 1) * QTILE]
        hi = _mx(psum)
        lo = _mx(psum - hi.astype(F32))
        ovl = ovl_ref[:, :rows]
        imp_ref[...] = _dot(ovl, hi) + _dot(ovl, lo)

    for k in range(1, ncp // chunk + 1):
        pl.when(n_chunks == k)(functools.partial(cmp_branch, k * chunk))

    j_i = lax.broadcasted_iota(jnp.int32, (ns, QTILE), 0)
    tq2 = c * QTILE + lax.broadcasted_iota(jnp.int32, (ns, QTILE), 1)
    valid = j_i * SLC_LEN <= tq2
    cur = tq2 >> (SLC_LEN.bit_length() - 1)
    forced = valid & ((j_i == 0) | (j_i == cur) | (j_i == cur - 1))
    score = jnp.where(forced, 1e9, jnp.where(valid, imp_ref[...], -1e9))
    bits = lax.bitcast_convert_type(score, jnp.int32)
    key_ref[...] = jnp.where(bits < 0, bits ^ 0x7FFFFFFF, bits)
    cnt_ref[...] = jnp.zeros(cnt_ref.shape, jnp.int32)

    sub = lax.broadcasted_iota(jnp.int32, (8, QTILE), 0)
    n_it = (c * (QTILE // SLC_LEN) + QTILE // SLC_LEN + 7) // 8

    def it_body(it, carry):
        base = pl.multiple_of(it * 8, 8)
        rows8 = key_ref[pl.ds(base, 8), :]
        rb = [jnp.broadcast_to(rows8[il:il + 1, :], (8, QTILE)) for il in range(8)]
        for jt in range(ns // 8):
            js = slice(jt * 8, (jt + 1) * 8)
            thr = key_ref[js, :] - jnp.where(jt > it, 1, 0)
            acc = cnt_ref[js, :]
            for il in range(8):
                acc = acc + jnp.where(rb[il] > thr, 1, 0)
            cnt_ref[js, :] = acc
        corr = jnp.zeros((8, QTILE), jnp.int32)
        for il in range(8):
            corr = corr + jnp.where((rb[il] == rows8) & (sub > il), 1, 0)
        cnt_ref[pl.ds(base, 8), :] += corr
        return carry

    lax.fori_loop(0, n_it, it_body, 0)
    sel = jnp.where(cnt_ref[...] < min(N_SEL, ns), 1.0, 0.0)
    sel_ref[0, 0, 0] = sel
    flag_ref[0, 0, 0] = _dot_nt(jnp.ones((8, QTILE), MXU_DTYPE), _mx(sel))


def _nsa_cmp_select(proj, kcmp, vcmp_t, ovl_t, slopes):
    bsz, g, ncp, dk = kcmp.shape
    nt = proj.shape[0] // (bsz * QTILE)
    ns = ovl_t.shape[0]
    return pl.pallas_call(
        _nsa_cmp_kernel,
        grid=(bsz, g, nt),
        in_specs=[pl.BlockSpec((QTILE, NSA_HPG * dk), lambda b, gg, c: (b * nt + c, gg)),
                  pl.BlockSpec((1, 1, ncp, dk), lambda b, gg, c: (b, gg, 0, 0)),
                  pl.BlockSpec((1, 1, dk, ncp), lambda b, gg, c: (b, gg, 0, 0)),
                  pl.BlockSpec((ns, ncp), lambda b, gg, c: (0, 0)),
                  pl.BlockSpec((1, 1, NCOL), lambda b, gg, c: (gg, 0, 0))],
        out_specs=[pl.BlockSpec((1, 1, 1, dk, NCOL), lambda b, gg, c: (b, gg, c, 0, 0)),
                   pl.BlockSpec((1, 1, 1, ns, QTILE), lambda b, gg, c: (b, gg, c, 0, 0)),
                   pl.BlockSpec((1, 1, 1, 8, ns), lambda b, gg, c: (b, gg, c, 0, 0))],
        out_shape=[jax.ShapeDtypeStruct((bsz, g, nt, dk, NCOL), F32),
                   jax.ShapeDtypeStruct((bsz, g, nt, ns, QTILE), F32),
                   jax.ShapeDtypeStruct((bsz, g, nt, 8, ns), F32)],
        scratch_shapes=[pltpu.VMEM((ns, QTILE), F32), pltpu.VMEM((ns, QTILE), jnp.int32),
                        pltpu.VMEM((ns, QTILE), jnp.int32)],
        compiler_params=_params("parallel", "parallel", "parallel"),
        name="nsa_cmp_select",
    )(proj, kcmp, vcmp_t, ovl_t, slopes)


def _values_t(v_blk):
    vt = v_blk.astype(F32).T[:VT_ROWS, :]
    ones_row = lax.broadcasted_iota(jnp.int32, vt.shape, 0) == NSA_HEAD_DIM
    return _mx(jnp.where(ones_row, 1.0, vt))


def _nsa_attn_kernel(lst_ref, cnt_ref, q_ref, qf_ref, ks_ref, vs_ref, kw_ref, vw_ref, pos_ref, sel_ref,
                     ocmp_ref, gl_ref, o_ref, ksa_sc, kwa_sc, vst_sc, vwt_sc, gate_sc, sa_sc, m_sc, acc_sc):
    b = pl.program_id(0)
    g = pl.program_id(1)
    c = pl.program_id(2)
    nt = pl.num_programs(2)
    npair = ks_ref.shape[1] // KBLK
    dk = NSA_HEAD_DIM

    @pl.when(c == 0)
    def _():
        def prep(j, carry):
            rows = pl.ds(pl.multiple_of(j * KBLK, KBLK), KBLK)
            feats = pos_ref[rows, :]
            ksa_sc[rows, :] = ks_ref[0, rows, :] + feats
            kwa_sc[rows, :] = kw_ref[0, rows, :] + feats
            vst_sc[j] = _values_t(vs_ref[0, rows, :])
            vwt_sc[j] = _values_t(vw_ref[0, rows, :])
            return carry

        lax.fori_loop(0, npair, prep, 0)

    rhs = jnp.concatenate([_queries_t(q_ref[...]), qf_ref[0]], axis=0)
    key_i = lax.broadcasted_iota(jnp.int32, (KBLK, QTILE), 0)
    q_i = lax.broadcasted_iota(jnp.int32, (KBLK, QTILE), 1)
    heads = [slice(h * QTILE, (h + 1) * QTILE) for h in range(NSA_HPG)]

    m_sc[...] = jnp.full(m_sc.shape, NEG, F32)
    acc_sc[...] = jnp.zeros(acc_sc.shape, F32)
    step = (b * NSA_KV_GROUPS + g) * nt + c

    def group_blocks(gi):
        return [lst_ref[step * npair + gi * SEL_UNROLL + u] for u in range(SEL_UNROLL)]

    def group_scores(gi, own_first):
        scores = []
        for u, jj in enumerate(group_blocks(gi)):
            js = jnp.maximum(jj, 0)
            rows = sel_ref[0, 0, 0, pl.ds(2 * js, 2), :]
            if own_first and u == 0:
                picked = jnp.where(key_i < SLC_LEN, rows[0:1, :], rows[1:2, :]) > 0.5
                bias = _mx(jnp.where(picked & (key_i <= q_i), 0.0, NEG))
            else:
                half = _mx(jnp.where(jj >= 0, (rows - 1.0) * -NEG, NEG))
                bias = jnp.concatenate([jnp.broadcast_to(half[0:1, :], (SLC_LEN, QTILE)),
                                        jnp.broadcast_to(half[1:2, :], (SLC_LEN, QTILE))], axis=0)
            keys = ksa_sc[pl.ds(pl.multiple_of(js * KBLK, KBLK), KBLK), :]
            scores.append(_dot(jnp.concatenate([keys, bias], axis=1), rhs))
        return scores

    def softmax_update(gi, scores):
        blocks = [jnp.maximum(jj, 0) for jj in group_blocks(gi)]
        m_new = []
        for cs in heads:
            m_old = m_sc[:, cs]
            mx_ = m_old
            for u in range(SEL_UNROLL):
                mx_ = jnp.maximum(mx_, jnp.max(scores[u][:, cs], axis=0, keepdims=True))
            acc_sc[:, cs] = jnp.exp2(m_old - mx_) * acc_sc[:, cs]
            m_sc[:, cs] = mx_
            m_new.append(mx_)
        upd = None
        for u in range(SEL_UNROLL):
            e = jnp.concatenate([_mx(jnp.exp2(scores[u][:, cs] - m_new[hi])) for hi, cs in enumerate(heads)],
                                axis=1)
            d = _dot(vst_sc[blocks[u]], e)
            upd = d if upd is None else upd + d
        acc_sc[...] += upd

    for u, s in enumerate(group_scores(0, True)):
        sa_sc[u] = s

    def trip(r, carry):
        scores_b = group_scores(2 * r + 1, False)
        softmax_update(2 * r, [sa_sc[u] for u in range(SEL_UNROLL)])
        for u, s in enumerate(group_scores(2 * r + 2, False)):
            sa_sc[u] = s
        softmax_update(2 * r + 1, scores_b)
        return carry

    lax.fori_loop(0, (cnt_ref[step] + 2 * SEL_UNROLL - 1) // (2 * SEL_UNROLL), trip, 0)
    o_sel = acc_sc[:dk, :] * (1.0 / jnp.maximum(acc_sc[dk:dk + 1, :], 1e-30))

    nwin = WINDOW // KBLK + 1
    blk0 = jnp.maximum(c - (nwin - 1), 0)
    row = lax.broadcasted_iota(jnp.int32, (nwin * KBLK, QTILE), 0)
    dist = lax.broadcasted_iota(jnp.int32, (nwin * KBLK, QTILE), 1) + (c - blk0) * KBLK - row
    bias = _mx(jnp.where((dist >= 0) & (dist < WINDOW), 0.0, NEG))
    keys = kwa_sc[pl.ds(pl.multiple_of(blk0 * KBLK, KBLK), nwin * KBLK), :]
    s = _dot(jnp.concatenate([keys, bias], axis=1), rhs)
    e_all = jnp.concatenate([_mx(jnp.exp2(s[:, cs] - jnp.max(s[:, cs], axis=0, keepdims=True)))
                             for cs in heads], axis=1)
    acc = None
    for t in range(nwin):
        d = _dot(vwt_sc[blk0 + t], e_all[t * KBLK:(t + 1) * KBLK, :])
        acc = d if acc is None else acc + d
    o_win = acc[:dk, :] * (1.0 / jnp.maximum(acc[dk:dk + 1, :], 1e-30))

    gate_sc[...] = gl_ref[...].T
    ocmp = ocmp_ref[0, 0, 0]
    outs = []
    for h, cs in enumerate(heads):
        gt = jax.nn.sigmoid(gate_sc[pl.ds((g * NSA_HPG + h) * 3, 3), :])
        outs.append(gt[0:1] * ocmp[:, cs] + gt[1:2] * o_sel[:, cs] + gt[2:3] * o_win[:, cs])
    o_ref[...] = jnp.concatenate(outs, axis=0).T


def _nsa_attend(blk_list, blk_count, proj, proj_f, q_feat, pos_feat, sel_t, ocmp_t, bsz):
    g, dk = NSA_KV_GROUPS, NSA_HEAD_DIM
    t = proj.shape[0] // bsz
    nt = t // QTILE
    npair = t // KBLK
    ns = sel_t.shape[3]
    proj3 = proj.reshape(bsz, t, proj.shape[1])
    seg = D_MODEL // LANE
    gate_blk = proj_f.shape[1] // LANE - 1
    tile = lambda b, gg, c, lst, cnt: (b, gg, c, 0, 0)
    rows = lambda b, gg, c, lst, cnt: (b * nt + c, gg)
    kv = lambda k: (lambda b, gg, c, lst, cnt: (b, 0, seg + k * g + gg))
    grid_spec = pltpu.PrefetchScalarGridSpec(
        num_scalar_prefetch=2,
        grid=(bsz, g, nt),
        in_specs=[pl.BlockSpec((QTILE, NSA_HPG * dk), rows),
                  pl.BlockSpec((1, dk + QTILE, NCOL), lambda b, gg, c, lst, cnt: (gg, 0, 0)),
                  pl.BlockSpec((1, t, LANE), kv(0)),
                  pl.BlockSpec((1, t, LANE), kv(1)),
                  pl.BlockSpec((1, t, LANE), kv(2)),
                  pl.BlockSpec((1, t, LANE), kv(3)),
                  pl.BlockSpec((t, LANE), lambda b, gg, c, lst, cnt: (0, 0)),
                  pl.BlockSpec((1, 1, 1, ns, QTILE), tile),
                  pl.BlockSpec((1, 1, 1, dk, NCOL), tile),
                  pl.BlockSpec((QTILE, LANE), lambda b, gg, c, lst, cnt: (b * nt + c, gate_blk))],
        out_specs=pl.BlockSpec((QTILE, NSA_HPG * dk), rows),
        scratch_shapes=[pltpu.VMEM((t, LANE), MXU_DTYPE), pltpu.VMEM((t, LANE), MXU_DTYPE),
                        pltpu.VMEM((npair, VT_ROWS, KBLK), MXU_DTYPE),
                        pltpu.VMEM((npair, VT_ROWS, KBLK), MXU_DTYPE),
                        pltpu.VMEM((LANE, QTILE), F32),
                        pltpu.VMEM((SEL_UNROLL, KBLK, NCOL), F32),
                        pltpu.VMEM((1, NCOL), F32), pltpu.VMEM((VT_ROWS, NCOL), F32)],
    )
    return pl.pallas_call(
        _nsa_attn_kernel,
        grid_spec=grid_spec,
        out_shape=jax.ShapeDtypeStruct((bsz * t, NSA_HEADS * dk), F32),
        compiler_params=_params("parallel", "parallel", "arbitrary"),
        name="nsa_attend",
    )(blk_list, blk_count, proj, q_feat, proj3, proj3, proj3, proj3, pos_feat, sel_t, ocmp_t, proj_f)


def _nsa_mixer(x2, bsz, t, w_in, pe_k, pe_v, wk1, wk2, wv1, wv2):
    g, hpg, dk = NSA_KV_GROUPS, NSA_HPG, NSA_HEAD_DIM
    nt = t // QTILE
    d = w_in.shape[0]
    splits = np.cumsum([D_MODEL] + [NSA_KV_DIM] * 6).tolist()
    wq, wkc, wvc, wks, wvs, wkw, wvw, wgl = jnp.split(w_in, splits, axis=1)

    def lane_groups(w):
        return jnp.pad(w.reshape(d, g, dk), ((0, 0), (0, 0), (0, LANE - dk))).reshape(d, g * LANE)

    w_a = _mx(jnp.concatenate([wq] + [lane_groups(w) for w in (wks, wvs, wkw, wvw)], axis=1))
    scale = jnp.concatenate([jnp.full((1, D_MODEL), dk ** -0.5 * LOG2E, F32),
                             jnp.ones((1, w_a.shape[1] - D_MODEL), F32)], axis=1)
    w_b = _mx(jnp.concatenate([wkc, wvc, jnp.pad(wgl, ((0, 0), (0, LANE - wgl.shape[1])))], axis=1))
    proj = _matmul_scaled(x2, w_a, scale, 512, 1024, MXU_DTYPE)
    proj_f = _matmul(x2, w_b, 512, w_b.shape[1])
    kc = proj_f[:, :NSA_KV_DIM]
    vc = proj_f[:, NSA_KV_DIM:2 * NSA_KV_DIM]

    slopes = 2.0 ** (-8.0 * (jnp.arange(NSA_HEADS, dtype=F32) + 1.0) / NSA_HEADS) * LOG2E
    slopes = jnp.broadcast_to(slopes.reshape(g, 1, hpg, 1), (g, 1, hpg, QTILE)).reshape(g, 1, NCOL)
    s1 = slopes.astype(jnp.bfloat16)
    s2 = (slopes - s1.astype(F32)).astype(jnp.bfloat16)
    s3 = (slopes - s1.astype(F32) - s2.astype(F32)).astype(jnp.bfloat16)
    eye = jnp.broadcast_to(jnp.tile(jnp.eye(QTILE, dtype=jnp.bfloat16), (1, hpg))[None], (g, QTILE, NCOL))
    q_feat = jnp.concatenate([s1 * SLC_LEN, s2 * SLC_LEN, s3 * SLC_LEN, s1, s2, s3]
                             + [jnp.zeros_like(s1)] * (dk - 6) + [eye], axis=1).astype(MXU_DTYPE)
    kpos = jnp.arange(t, dtype=jnp.int32)
    ka_, kb_ = (kpos // SLC_LEN).astype(jnp.bfloat16), (kpos % SLC_LEN).astype(jnp.bfloat16)
    zero = jnp.zeros_like(ka_)
    pos_feat = jnp.stack([zero] * dk + [ka_, ka_, ka_, kb_, kb_, kb_] + [zero] * (LANE - dk - 6),
                         axis=1).astype(MXU_DTYPE)

    four = lambda a: a.reshape(bsz, t, g, dk)
    k_cmp = _compress(four(kc), pe_k, wk1, wk2)
    v_cmp = _compress(four(vc), pe_v, wv1, wv2)
    kcmp = _mx(k_cmp).transpose(0, 2, 1, 3)
    vcmp_t = _mx(v_cmp).transpose(0, 2, 3, 1)

    ncp = t // CMP_STRIDE
    ns = t // SLC_LEN
    cs = np.arange(ncp) * CMP_STRIDE
    ce = cs + CMP_LEN - 1
    bs = np.arange(ns) * SLC_LEN
    be = bs + SLC_LEN - 1
    ovl_t = ((cs[None, :] <= be[:, None]) & (ce[None, :] >= bs[:, None])).astype(np.float32)
    ovl_t[:, ncp - 1] = 0.0
    ovl_t = jnp.asarray(ovl_t, MXU_DTYPE)

    ocmp_t, sel_t, counts = _nsa_cmp_select(proj, kcmp, vcmp_t, ovl_t, slopes)
    npair = t // KBLK
    picked = counts[:, :, :, 0, :] > 0.5
    jj = jnp.arange(npair)
    tile_i = jnp.arange(nt)[:, None]
    active = (picked[..., 0::2] | picked[..., 1::2]) & (jj[None, :] < tile_i)
    blk_count = 1 + jnp.sum(active, axis=-1).astype(jnp.int32)
    slot = jnp.cumsum(active, axis=-1)
    hit = active[..., None, :] & (slot[..., None, :] == jj[:, None])
    blk_list = jnp.sum(jnp.where(hit, jj, 0), axis=-1)
    blk_list = jnp.where(jj == 0, tile_i, blk_list)
    blk_list = jnp.where(jj < blk_count[..., None], blk_list, -1).astype(jnp.int32)

    blk_list = jnp.pad(blk_list.reshape(-1), (0, 2 * SEL_UNROLL), constant_values=-1)
    return _nsa_attend(blk_list, blk_count.reshape(-1), proj, proj_f, q_feat, pos_feat,
                       sel_t, ocmp_t, bsz)


def _log_sigmoid(z):
    return jnp.minimum(z, 0.0) - jnp.log1p(jnp.exp(-jnp.abs(z)))


def _gla_kernel(q_ref, k_ref, v_ref, g_ref, al_ref, w2_ref, b2_ref, hn_ref, o_ref, st_sc):
    @pl.when(pl.program_id(0) == 0)
    def _():
        st_sc[...] = jnp.zeros(st_sc.shape, F32)

    cc = GLA_CHUNK
    r_i = lax.broadcasted_iota(jnp.int32, (cc, cc), 0)
    c_i = lax.broadcasted_iota(jnp.int32, (cc, cc), 1)
    causal = r_i >= c_i
    tri = jnp.where(causal, 1.0, 0.0).astype(MXU_DTYPE)
    ref = cc // 2

    def chunk(ci, carry):
        rows = pl.ds(pl.multiple_of(ci * cc, cc), cc)
        for b in range(q_ref.shape[0]):
            z = _dot(_mx(al_ref[b, rows, :]), w2_ref[...]) + b2_ref[...]
            log_a = _log_sigmoid(z) / GLA_GATE_NORM
            a1 = _mx(log_a)
            r1 = log_a - a1.astype(F32)
            a2 = _mx(r1)
            a3 = _mx(r1 - a2.astype(F32))
            bc_all = _dot(tri, a1) + _dot(tri, a2) + _dot(tri, a3)
            for h in range(GLA_HEADS):
                ks = slice(h * GLA_DK, (h + 1) * GLA_DK)
                vs = slice(h * GLA_DV, (h + 1) * GLA_DV)
                bc = bc_all[:, ks]
                q = q_ref[b, rows, ks] * GLA_DK ** -0.5
                k = k_ref[b, rows, ks]
                v = v_ref[b, rows, vs]
                b_ref = bc[ref:ref + 1, :]
                b_last = bc[cc - 1:cc, :]
                a = _dot_nt(_mx(q * jnp.exp(bc - b_ref)), _mx(k * jnp.exp(b_ref - bc)))
                a = jnp.where(causal, a, 0.0)
                st = st_sc[b * GLA_HEADS + h]
                o = _dot(_mx(a), _mx(v)) + _dot_nt(_mx(q * jnp.exp(bc)), _mx(st))
                kl = k * jnp.exp(b_last - bc)
                st_sc[b * GLA_HEADS + h] = st * jnp.exp(b_last) + _dot(_mx(v.T), _mx(kl))
                o = o * lax.rsqrt(jnp.mean(o * o, -1, keepdims=True) + RMS_EPS)
                gg = g_ref[b, rows, vs]
                o_ref[b, rows, vs] = o * hn_ref[:, vs] * (gg * jax.nn.sigmoid(gg))
        return carry

    lax.fori_loop(0, q_ref.shape[1] // cc, chunk, 0)


def _gla_mixer(x2, bsz, t, w_in, w_gate2, b_gate2, head_norm_g):
    n_main = 2 * GLA_KEY_DIM + 2 * GLA_VAL_DIM
    w_main = _mx(w_in[:, :n_main])
    w_low = jnp.pad(_mx(w_in[:, n_main:]), ((0, 0), (0, LANE - GLA_GATE_RANK)))
    proj = _matmul(x2, w_main, 512, 1024).reshape(bsz, t, n_main)
    a_low = _matmul(x2, w_low, 512, LANE).reshape(bsz, t, LANE)
    w2p = jnp.pad(_mx(w_gate2), ((0, LANE - GLA_GATE_RANK), (0, 0)))
    ts = min(GLA_TSTEP, t)
    out = pl.pallas_call(
        _gla_kernel,
        grid=(t // ts,),
        in_specs=[pl.BlockSpec((bsz, ts, GLA_KEY_DIM), lambda s: (0, s, 0)),
                  pl.BlockSpec((bsz, ts, GLA_KEY_DIM), lambda s: (0, s, 1)),
                  pl.BlockSpec((bsz, ts, GLA_VAL_DIM), lambda s: (0, s, 1)),
                  pl.BlockSpec((bsz, ts, GLA_VAL_DIM), lambda s: (0, s, 2)),
                  pl.BlockSpec((bsz, ts, LANE), lambda s: (0, s, 0)),
                  pl.BlockSpec((LANE, GLA_KEY_DIM), lambda s: (0, 0)),
                  pl.BlockSpec((1, GLA_KEY_DIM), lambda s: (0, 0)),
                  pl.BlockSpec((1, GLA_VAL_DIM), lambda s: (0, 0))],
        out_specs=pl.BlockSpec((bsz, ts, GLA_VAL_DIM), lambda s: (0, s, 0)),
        out_shape=jax.ShapeDtypeStruct((bsz, t, GLA_VAL_DIM), F32),
        scratch_shapes=[pltpu.VMEM((bsz * GLA_HEADS, GLA_DV, GLA_DK), F32)],
        compiler_params=_params("arbitrary"),
        name="gla",
    )(proj, proj, proj, proj, a_low, w2p, b_gate2.reshape(1, -1), head_norm_g.reshape(1, -1))
    return out.reshape(bsz * t, GLA_VAL_DIM)


def _router_kernel(x_ref, wh_ref, wl_ref, o_ref):
    x = x_ref[...]
    xh = _mx(x)
    xl = _mx(x - xh.astype(F32))
    wh = wh_ref[...]
    logits = _dot(xh, wh) + _dot(xl, wh) + _dot(xh, wl_ref[...])
    lane = lax.broadcasted_iota(jnp.int32, logits.shape, 1)
    lg = jnp.where(lane < N_EXPERTS, logits, -jnp.inf)
    m1 = jnp.max(lg, axis=1, keepdims=True)
    i1 = jnp.min(jnp.where(lg == m1, lane, LANE), axis=1, keepdims=True)
    lg2 = jnp.where(lane == i1, -jnp.inf, lg)
    m2 = jnp.max(lg2, axis=1, keepdims=True)
    i2 = jnp.min(jnp.where(lg2 == m2, lane, LANE), axis=1, keepdims=True)
    e2 = jnp.exp(m2 - m1)
    den = 1.0 + e2
    w1 = 1.0 / den
    w2 = e2 / den
    out = jnp.where(lane == N_EXPERTS, i1.astype(F32), 0.0)
    out = jnp.where(lane == N_EXPERTS + 1, i2.astype(F32), out)
    out = jnp.where(lane == N_EXPERTS + 2, w1, out)
    out = jnp.where(lane == N_EXPERTS + 3, w2, out)
    o_ref[...] = out


def _moe_up_kernel(te_ref, x_ref, wg_ref, wu_ref, o_ref, wg_sc, wu_sc):
    i = pl.program_id(1)

    @pl.when((i == 0) | (te_ref[i] != te_ref[jnp.maximum(i - 1, 0)]))
    def _():
        wg_sc[...] = _mx(wg_ref[0])
        wu_sc[...] = _mx(wu_ref[0])

    _swiglu_cols(_mx(x_ref[...]), wg_sc, wu_sc, o_ref)


def _moe_down_kernel(te_ref, h_ref, w_ref, rw_ref, o_ref, w_sc):
    i = pl.program_id(0)

    @pl.when((i == 0) | (te_ref[i] != te_ref[jnp.maximum(i - 1, 0)]))
    def _():
        w_sc[...] = _mx(w_ref[0])

    o_ref[...] = _dot(h_ref[...], w_sc[...]) * rw_ref[...]


def _add_ln_kernel(x_ref, y0_ref, y1_ref, g_ref, b_ref, o_ref):
    h = DN_ALPHA * x_ref[...] + (y0_ref[...] + y1_ref[...])
    o_ref[...] = _layer_norm_rows(h, g_ref[...], b_ref[...])


def _moe_layer(x2, w_router, w_gu, w_down, ln_g, ln_b):
    m, d = x2.shape
    tm = MOE_TM
    wr = jnp.pad(w_router, ((0, 0), (0, LANE - N_EXPERTS)))
    wr_hi = _mx(wr)
    wr_lo = _mx(wr - wr_hi.astype(F32))
    rt = pl.pallas_call(
        _router_kernel,
        grid=(m // 512,),
        in_specs=[pl.BlockSpec((512, d), lambda i: (i, 0)),
                  pl.BlockSpec((d, LANE), lambda i: (0, 0)),
                  pl.BlockSpec((d, LANE), lambda i: (0, 0))],
        out_specs=pl.BlockSpec((512, LANE), lambda i: (i, 0)),
        out_shape=jax.ShapeDtypeStruct((m, LANE), F32),
        compiler_params=_params("parallel"),
        name="moe_router",
    )(x2, wr_hi, wr_lo)
    top_idx = rt[:, N_EXPERTS:N_EXPERTS + 2].astype(jnp.int32)
    top_w = rt[:, N_EXPERTS + 2:N_EXPERTS + 4]

    n_asg = m * TOP_K
    n_rows = n_asg + N_EXPERTS * tm
    n_tiles = n_rows // tm
    e_flat = top_idx.reshape(-1)
    order = jnp.argsort(e_flat, stable=True).astype(jnp.int32)
    slot = jnp.argsort(order).astype(jnp.int32)
    counts = jnp.sum((e_flat[:, None] == jnp.arange(N_EXPERTS)[None, :]).astype(jnp.int32), axis=0)
    tiles_per = (counts + tm - 1) // tm
    tile_end = jnp.cumsum(tiles_per)
    row_start = (tile_end - tiles_per) * tm
    grp_start = jnp.cumsum(counts) - counts
    tile_expert = jnp.minimum(jnp.sum((jnp.arange(n_tiles)[:, None] >= tile_end[None, :]).astype(jnp.int32),
                                      axis=1), N_EXPERTS - 1).astype(jnp.int32)
    pos = (row_start[e_flat] + slot - grp_start[e_flat]).astype(jnp.int32).reshape(m, TOP_K)
    row_e = jnp.repeat(tile_expert, tm)
    row_off = jnp.arange(n_rows, dtype=jnp.int32) - row_start[row_e].astype(jnp.int32)
    row_live = row_off < counts[row_e]
    row_asg = order[jnp.clip(grp_start[row_e].astype(jnp.int32) + row_off, 0, n_asg - 1)]
    row_token = jnp.where(row_live, row_asg // TOP_K, 0)
    row_w = jnp.where(row_live, top_w.reshape(-1)[row_asg], 0.0)

    xs = x2[row_token]
    f = w_gu.shape[2] // 2
    tf = MOE_TF
    nf = f // tf
    h = pl.pallas_call(
        _moe_up_kernel,
        grid_spec=pltpu.PrefetchScalarGridSpec(
            num_scalar_prefetch=1,
            grid=(nf, n_tiles),
            in_specs=[pl.BlockSpec((tm, d), lambda j, i, te: (i, 0)),
                      pl.BlockSpec((1, d, tf), lambda j, i, te: (te[i], 0, j)),
                      pl.BlockSpec((1, d, tf), lambda j, i, te: (te[i], 0, j + nf))],
            out_specs=pl.BlockSpec((tm, tf), lambda j, i, te: (i, j)),
            scratch_shapes=[pltpu.VMEM((d, tf), MXU_DTYPE), pltpu.VMEM((d, tf), MXU_DTYPE)],
        ),
        out_shape=jax.ShapeDtypeStruct((n_rows, f), MXU_DTYPE),
        compiler_params=_params("arbitrary", "arbitrary"),
        name="moe_up",
    )(tile_expert, xs, w_gu, w_gu)
    ys = pl.pallas_call(
        _moe_down_kernel,
        grid_spec=pltpu.PrefetchScalarGridSpec(
            num_scalar_prefetch=1,
            grid=(n_tiles,),
            in_specs=[pl.BlockSpec((tm, f), lambda i, te: (i, 0)),
                      pl.BlockSpec((1, f, d), lambda i, te: (te[i], 0, 0)),
                      pl.BlockSpec((tm, 1), lambda i, te: (i, 0))],
            out_specs=pl.BlockSpec((tm, d), lambda i, te: (i, 0)),
            scratch_shapes=[pltpu.VMEM((f, d), MXU_DTYPE)],
        ),
        out_shape=jax.ShapeDtypeStruct((n_rows, d), F32),
        compiler_params=_params("arbitrary"),
        name="moe_down",
    )(tile_expert, h, w_down, row_w.reshape(n_rows, 1))

    lo_first = top_idx[:, 0:1] < top_idx[:, 1:2]
    p0 = jnp.where(lo_first[:, 0], pos[:, 0], pos[:, 1])
    p1 = jnp.where(lo_first[:, 0], pos[:, 1], pos[:, 0])
    y0 = ys[p0]
    y1 = ys[p1]
    return pl.pallas_call(
        _add_ln_kernel,
        grid=(m // 512,),
        in_specs=[pl.BlockSpec((512, d), lambda i: (i, 0))] * 3
        + [pl.BlockSpec((1, d), lambda i: (0, 0))] * 2,
        out_specs=pl.BlockSpec((512, d), lambda i: (i, 0)),
        out_shape=jax.ShapeDtypeStruct((m, d), F32),
        compiler_params=_params("parallel"),
        name="moe_add_ln",
    )(x2, y0, y1, ln_g.reshape(1, d), ln_b.reshape(1, d))


def kernel(x, l0_w_in, l0_cmp_pe_k, l0_cmp_pe_v, l0_cmp_wk1, l0_cmp_wk2, l0_cmp_wv1, l0_cmp_wv2, l0_w_o, l0_ln1_g, l0_ln1_b, l0_ffn_w_gu, l0_ffn_w_down, l0_ln2_g, l0_ln2_b, l1_w_in, l1_w_gate2, l1_b_gate2, l1_head_norm_g, l1_w_o, l1_ln1_g, l1_ln1_b, l1_router, l1_moe_w_gu, l1_moe_w_down, l1_ln2_g, l1_ln2_b):
    bsz, t, d = x.shape
    x2 = x.reshape(bsz * t, d)

    o = _nsa_mixer(x2, bsz, t, l0_w_in, l0_cmp_pe_k, l0_cmp_pe_v, l0_cmp_wk1, l0_cmp_wk2,
                   l0_cmp_wv1, l0_cmp_wv2)
    x2 = _matmul_res_ln(o, _mx(l0_w_o), x2, l0_ln1_g, l0_ln1_b, 512)
    hmid = _swiglu_up(x2, _mx(l0_ffn_w_gu), 512, FFN_DENSE)
    x2 = _matmul_res_ln(hmid, _mx(l0_ffn_w_down), x2, l0_ln2_g, l0_ln2_b, 512)

    o = _gla_mixer(x2, bsz, t, l1_w_in, l1_w_gate2, l1_b_gate2, l1_head_norm_g)
    x2 = _matmul_res_ln(o, _mx(l1_w_o), x2, l1_ln1_g, l1_ln1_b, 512)
    x2 = _moe_layer(x2, l1_router, l1_moe_w_gu, l1_moe_w_down, l1_ln2_g, l1_ln2_b)
    return x2.reshape(bsz, t, d)
```

```python
import functools

import numpy as np
import jax
import jax.numpy as jnp
from jax import lax
from jax.experimental import pallas as pl
from jax.experimental.pallas import tpu as pltpu

F32 = jnp.float32
MXU_DTYPE = jnp.bfloat16

D_MODEL = 1024
DEPTH = 2
DN_ALPHA = (2 * DEPTH) ** 0.25
LN_EPS = 1e-5
RMS_EPS = 1e-6

NSA_HEADS = 16
NSA_HEAD_DIM = 64
NSA_KV_GROUPS = 4
NSA_HPG = NSA_HEADS // NSA_KV_GROUPS
NSA_KV_DIM = NSA_KV_GROUPS * NSA_HEAD_DIM
CMP_LEN = 32
CMP_STRIDE = 16
CMP_HIDDEN = 256
SLC_LEN = 64
N_SEL = 16
WINDOW = 512
QTILE = 128
KBLK = 128
NCOL = NSA_HPG * QTILE
CMP_CHUNK = 128
SEL_UNROLL = 6
VT_ROWS = 80
LOG2E = 1.4426950408889634

GLA_HEADS = 4
GLA_KEY_DIM = D_MODEL // 2
GLA_VAL_DIM = D_MODEL
GLA_DK = GLA_KEY_DIM // GLA_HEADS
GLA_DV = GLA_VAL_DIM // GLA_HEADS
GLA_GATE_RANK = 16
GLA_GATE_NORM = 16.0
GLA_CHUNK = 64
GLA_TSTEP = 256

FFN_DENSE = 2816
N_EXPERTS = 8
TOP_K = 2
FFN_EXPERT = 3584
MOE_TM = 256
MOE_TF = 1792
SWIGLU_CHUNK = 256

LANE = 128
NEG = -1e30
VMEM_LIMIT = 56 * 1024 * 1024


def _params(*sem):
    return pltpu.CompilerParams(dimension_semantics=sem, vmem_limit_bytes=VMEM_LIMIT)


def _mx(a):
    return a.astype(MXU_DTYPE)


def _dot(a, b):
    return jnp.dot(a, b, preferred_element_type=F32)


def _dot_nt(a, b):
    return lax.dot_general(a, b, (((1,), (1,)), ((), ())), preferred_element_type=F32)


def _mm_kernel(x_ref, w_ref, o_ref):
    o_ref[...] = _dot(_mx(x_ref[...]), w_ref[...]).astype(o_ref.dtype)


def _matmul(x, w, tm, tn, out_dtype=F32):
    m, k = x.shape
    n = w.shape[1]
    return pl.pallas_call(
        _mm_kernel,
        grid=(m // tm, n // tn),
        in_specs=[pl.BlockSpec((tm, k), lambda i, j: (i, 0)),
                  pl.BlockSpec((k, tn), lambda i, j: (0, j))],
        out_specs=pl.BlockSpec((tm, tn), lambda i, j: (i, j)),
        out_shape=jax.ShapeDtypeStruct((m, n), out_dtype),
        compiler_params=_params("parallel", "parallel"),
        name="matmul",
    )(x, w)


def _mm_scaled_kernel(x_ref, w_ref, s_ref, o_ref):
    o_ref[...] = (_dot(_mx(x_ref[...]), w_ref[...]) * s_ref[...]).astype(o_ref.dtype)


def _matmul_scaled(x, w, col_scale, tm, tn, out_dtype):
    m, k = x.shape
    n = w.shape[1]
    return pl.pallas_call(
        _mm_scaled_kernel,
        grid=(m // tm, n // tn),
        in_specs=[pl.BlockSpec((tm, k), lambda i, j: (i, 0)),
                  pl.BlockSpec((k, tn), lambda i, j: (0, j)),
                  pl.BlockSpec((1, tn), lambda i, j: (0, j))],
        out_specs=pl.BlockSpec((tm, tn), lambda i, j: (i, j)),
        out_shape=jax.ShapeDtypeStruct((m, n), out_dtype),
        compiler_params=_params("parallel", "parallel"),
        name="matmul_scaled",
    )(x, w, col_scale)


def _layer_norm_rows(h, g, b):
    mu = jnp.mean(h, -1, keepdims=True)
    d = h - mu
    var = jnp.mean(d * d, -1, keepdims=True)
    return d * lax.rsqrt(var + LN_EPS) * g + b


def _mm_ln_kernel(x_ref, w_ref, r_ref, g_ref, b_ref, o_ref):
    y = _dot(_mx(x_ref[...]), w_ref[...])
    o_ref[...] = _layer_norm_rows(DN_ALPHA * r_ref[...] + y, g_ref[...], b_ref[...])


def _matmul_res_ln(x, w, res, g, b, tm):
    m, k = x.shape
    n = w.shape[1]
    return pl.pallas_call(
        _mm_ln_kernel,
        grid=(m // tm,),
        in_specs=[pl.BlockSpec((tm, k), lambda i: (i, 0)),
                  pl.BlockSpec((k, n), lambda i: (0, 0)),
                  pl.BlockSpec((tm, n), lambda i: (i, 0)),
                  pl.BlockSpec((1, n), lambda i: (0, 0)),
                  pl.BlockSpec((1, n), lambda i: (0, 0))],
        out_specs=pl.BlockSpec((tm, n), lambda i: (i, 0)),
        out_shape=jax.ShapeDtypeStruct((m, n), F32),
        compiler_params=_params("parallel"),
        name="matmul_res_ln",
    )(x, w, res, g.reshape(1, n), b.reshape(1, n))


def _swiglu_cols(x, wg_ref, wu_ref, o_ref):
    n = o_ref.shape[1]
    step = SWIGLU_CHUNK if n % SWIGLU_CHUNK == 0 else LANE
    for j in range(0, n, step):
        a = _dot(x, wg_ref[:, j:j + step])
        b = _dot(x, wu_ref[:, j:j + step])
        o_ref[:, j:j + step] = (a * jax.nn.sigmoid(a) * b).astype(o_ref.dtype)


def _swiglu_up_kernel(x_ref, wg_ref, wu_ref, o_ref):
    _swiglu_cols(_mx(x_ref[...]), wg_ref, wu_ref, o_ref)


def _swiglu_up(x, w_gu, tm, tn):
    m, k = x.shape
    f = w_gu.shape[1] // 2
    nj = f // tn
    return pl.pallas_call(
        _swiglu_up_kernel,
        grid=(m // tm, nj),
        in_specs=[pl.BlockSpec((tm, k), lambda i, j: (i, 0)),
                  pl.BlockSpec((k, tn), lambda i, j: (0, j)),
                  pl.BlockSpec((k, tn), lambda i, j: (0, j + nj))],
        out_specs=pl.BlockSpec((tm, tn), lambda i, j: (i, j)),
        out_shape=jax.ShapeDtypeStruct((m, f), MXU_DTYPE),
        compiler_params=_params("parallel", "parallel"),
        name="swiglu_up",
    )(x, w_gu, w_gu)


def _cmp_up_kernel(c_ref, pea_ref, peb_ref, wa_ref, wb_ref, p_ref, q_ref):
    c = c_ref[...]
    p_ref[...] = _dot(_mx(c + pea_ref[...]), wa_ref[...])
    q_ref[...] = _dot(_mx(c + peb_ref[...]), wb_ref[...])


def _cmp_down_kernel(p_ref, q_ref, w_ref, o_ref):
    h = jax.nn.gelu(p_ref[...] + q_ref[...])
    o_ref[...] = _dot(_mx(h), w_ref[...])


def _compress(a, pe, w1, w2):
    bsz, t, g, dk = a.shape
    nch = t // CMP_STRIDE
    half = CMP_STRIDE * dk
    chunks = a.reshape(bsz, nch, CMP_STRIDE, g, dk).transpose(0, 1, 3, 2, 4).reshape(bsz * nch * g, half)
    rows = chunks.shape[0]
    tm = min(512, rows)
    pe_a = pe[:CMP_STRIDE].reshape(1, half)
    pe_b = pe[CMP_STRIDE:].reshape(1, half)
    w1m = _mx(w1)
    p, q = pl.pallas_call(
        _cmp_up_kernel,
        grid=(rows // tm,),
        in_specs=[pl.BlockSpec((tm, half), lambda i: (i, 0)),
                  pl.BlockSpec((1, half), lambda i: (0, 0)),
                  pl.BlockSpec((1, half), lambda i: (0, 0)),
                  pl.BlockSpec((half, CMP_HIDDEN), lambda i: (0, 0)),
                  pl.BlockSpec((half, CMP_HIDDEN), lambda i: (0, 0))],
        out_specs=[pl.BlockSpec((tm, CMP_HIDDEN), lambda i: (i, 0)),
                   pl.BlockSpec((tm, CMP_HIDDEN), lambda i: (i, 0))],
        out_shape=[jax.ShapeDtypeStruct((rows, CMP_HIDDEN), F32)] * 2,
        compiler_params=_params("parallel"),
        name="cmp_up",
    )(chunks, pe_a, pe_b, w1m[:half], w1m[half:])
    q = q.reshape(bsz, nch, g, CMP_HIDDEN)
    q = jnp.concatenate([q[:, 1:], jnp.zeros_like(q[:, :1])], axis=1).reshape(rows, CMP_HIDDEN)
    w2p = jnp.pad(_mx(w2), ((0, 0), (0, LANE - dk)))
    out = pl.pallas_call(
        _cmp_down_kernel,
        grid=(rows // tm,),
        in_specs=[pl.BlockSpec((tm, CMP_HIDDEN), lambda i: (i, 0)),
                  pl.BlockSpec((tm, CMP_HIDDEN), lambda i: (i, 0)),
                  pl.BlockSpec((CMP_HIDDEN, LANE), lambda i: (0, 0))],
        out_specs=pl.BlockSpec((tm, LANE), lambda i: (i, 0)),
        out_shape=jax.ShapeDtypeStruct((rows, LANE), F32),
        compiler_params=_params("parallel"),
        name="cmp_down",
    )(p, q, w2p)
    out = out[:, :dk].reshape(bsz, nch, g, dk)
    valid = (jnp.arange(nch) < nch - 1)[None, :, None, None]
    return jnp.where(valid, out, 0.0)


def _queries_t(q_blk):
    qt = q_blk.astype(F32).T
    dk = NSA_HEAD_DIM
    return _mx(jnp.concatenate([qt[h * dk:(h + 1) * dk, :] for h in range(NSA_HPG)], axis=1))


def _nsa_cmp_kernel(q_ref, qf_ref, kc_ref, vct_ref, ovl_ref, ocmp_ref, sel_ref, flag_ref,
                    imp_ref, key_ref, cnt_ref):
    c = pl.program_id(2)
    ncp = kc_ref.shape[2]
    ns = ovl_ref.shape[0]
    dk = NSA_HEAD_DIM
    rhs = jnp.concatenate([_queries_t(q_ref[...]), qf_ref[0]], axis=0)
    chunk = min(CMP_CHUNK, ncp)
    n_chunks = (c * (QTILE // CMP_STRIDE) + (QTILE - CMP_LEN) // CMP_STRIDE + chunk) // chunk

    def cmp_branch(rows):
        n_i = lax.broadcasted_iota(jnp.int32, (rows, QTILE), 0)
        tq = c * QTILE + lax.broadcasted_iota(jnp.int32, (rows, QTILE), 1)
        bias = _mx(jnp.where(n_i * CMP_STRIDE + (CMP_LEN - 1) <= tq, 0.0, NEG))
        s = _dot(jnp.concatenate([kc_ref[0, 0, :rows, :], bias], axis=1), rhs)
        e = jnp.exp2(s - jnp.max(s, axis=0, keepdims=True))
        seen = jnp.concatenate([tq[0:1, :]] * NSA_HPG, axis=1) >= CMP_LEN - 1
        r = jnp.where(seen, 1.0 / jnp.maximum(jnp.sum(e, axis=0, keepdims=True), 1e-30), 0.0)
        ocmp_ref[0, 0, 0] = _dot(vct_ref[0, 0, :, :rows], _mx(e)) * r
        psum = e[:, 0:QTILE] * r[:, 0:QTILE]
        for h in range(1, NSA_HPG):
            psum = psum + e[:, h * QTILE:(h + 1) * QTILE] * r[:, h * QTILE:(h + 1) * QTILE]
        hi = _mx(psum)
        lo = _mx(psum - hi.astype(F32))
        ovl = ovl_ref[:, :rows]
        imp_ref[...] = _dot(ovl, hi) + _dot(ovl, lo)

    for k in range(1, ncp // chunk + 1):
        pl.when(n_chunks == k)(functools.partial(cmp_branch, k * chunk))

    j_i = lax.broadcasted_iota(jnp.int32, (ns, QTILE), 0)
    tq2 = c * QTILE + lax.broadcasted_iota(jnp.int32, (ns, QTILE), 1)
    valid = j_i * SLC_LEN <= tq2
    cur = tq2 >> (SLC_LEN.bit_length() - 1)
    forced = valid & ((j_i == 0) | (j_i == cur) | (j_i == cur - 1))
    score = jnp.where(forced, 1e9, jnp.where(valid, imp_ref[...], -1e9))
    bits = lax.bitcast_convert_type(score, jnp.int32)
    key_ref[...] = jnp.where(bits < 0, bits ^ 0x7FFFFFFF, bits)
    cnt_ref[...] = jnp.zeros(cnt_ref.shape, jnp.int32)

    sub = lax.broadcasted_iota(jnp.int32, (8, QTILE), 0)
    n_it = (c * (QTILE // SLC_LEN) + QTILE // SLC_LEN + 7) // 8

    def it_body(it, carry):
        base = pl.multiple_of(it * 8, 8)
        rows8 = key_ref[pl.ds(base, 8), :]
        rb = [jnp.broadcast_to(rows8[il:il + 1, :], (8, QTILE)) for il in range(8)]
        for jt in range(ns // 8):
            js = slice(jt * 8, (jt + 1) * 8)
            thr = key_ref[js, :] - jnp.where(jt > it, 1, 0)
            acc = cnt_ref[js, :]
            for il in range(8):
                acc = acc + jnp.where(rb[il] > thr, 1, 0)
            cnt_ref[js, :] = acc
        corr = jnp.zeros((8, QTILE), jnp.int32)
        for il in range(8):
            corr = corr + jnp.where((rb[il] == rows8) & (sub > il), 1, 0)
        cnt_ref[pl.ds(base, 8), :] += corr
        return carry

    lax.fori_loop(0, n_it, it_body, 0)
    sel = jnp.where(cnt_ref[...] < min(N_SEL, ns), 1.0, 0.0)
    sel_ref[0, 0, 0] = sel
    flag_ref[0, 0, 0] = _dot_nt(jnp.ones((8, QTILE), MXU_DTYPE), _mx(sel))


def _nsa_cmp_select(proj, q_feat, kcmp, vcmp_t, ovl_t):
    bsz, g, ncp, _ = kcmp.shape
    dk = NSA_HEAD_DIM
    nt = proj.shape[0] // (bsz * QTILE)
    ns = ovl_t.shape[0]
    return pl.pallas_call(
        _nsa_cmp_kernel,
        grid=(bsz, g, nt),
        in_specs=[pl.BlockSpec((QTILE, NSA_HPG * dk), lambda b, gg, c: (b * nt + c, gg)),
                  pl.BlockSpec((1, dk + QTILE, NCOL), lambda b, gg, c: (gg, 0, 0)),
                  pl.BlockSpec((1, 1, ncp, LANE), lambda b, gg, c: (b, gg, 0, 0)),
                  pl.BlockSpec((1, 1, dk, ncp), lambda b, gg, c: (b, gg, 0, 0)),
                  pl.BlockSpec((ns, ncp), lambda b, gg, c: (0, 0))],
        out_specs=[pl.BlockSpec((1, 1, 1, dk, NCOL), lambda b, gg, c: (b, gg, c, 0, 0)),
                   pl.BlockSpec((1, 1, 1, ns, QTILE), lambda b, gg, c: (b, gg, c, 0, 0)),
                   pl.BlockSpec((1, 1, 1, 8, ns), lambda b, gg, c: (b, gg, c, 0, 0))],
        out_shape=[jax.ShapeDtypeStruct((bsz, g, nt, dk, NCOL), F32),
                   jax.ShapeDtypeStruct((bsz, g, nt, ns, QTILE), F32),
                   jax.ShapeDtypeStruct((bsz, g, nt, 8, ns), F32)],
        scratch_shapes=[pltpu.VMEM((ns, QTILE), F32), pltpu.VMEM((ns, QTILE), jnp.int32),
                        pltpu.VMEM((ns, QTILE), jnp.int32)],
        compiler_params=_params("parallel", "parallel", "parallel"),
        name="nsa_cmp_select",
    )(proj, q_feat, kcmp, vcmp_t, ovl_t)


def _list_stride(npair):
    trip = 2 * SEL_UNROLL
    return -(-npair // trip) * trip + SEL_UNROLL


def _values_t(v_blk):
    vt = v_blk.astype(F32).T[:VT_ROWS, :]
    ones_row = lax.broadcasted_iota(jnp.int32, vt.shape, 0) == NSA_HEAD_DIM
    return _mx(jnp.where(ones_row, 1.0, vt))


def _nsa_attn_kernel(lst_ref, cnt_ref, q_ref, qf_ref, ks_ref, vs_ref, kw_ref, vw_ref, pos_ref, sel_ref,
                     ocmp_ref, gl_ref, o_ref, ksa_sc, kwa_sc, vst_sc, vwt_sc, gate_sc, sa_sc, ow_sc, m_sc, acc_sc):
    b = pl.program_id(0)
    g = pl.program_id(1)
    c = pl.program_id(2)
    nt = pl.num_programs(2)
    npair = ks_ref.shape[1] // KBLK
    dk = NSA_HEAD_DIM

    @pl.when(c == 0)
    def _():
        def prep(j, carry):
            rows = pl.ds(pl.multiple_of(j * KBLK, KBLK), KBLK)
            feats = pos_ref[rows, :]
            ksa_sc[rows, :] = ks_ref[0, rows, :] + feats
            kwa_sc[rows, :] = kw_ref[0, rows, :] + feats
            vst_sc[j] = _values_t(vs_ref[0, rows, :])
            vwt_sc[j] = _values_t(vw_ref[0, rows, :])
            return carry

        lax.fori_loop(0, npair, prep, 0)

    rhs = jnp.concatenate([_queries_t(q_ref[...]), qf_ref[0]], axis=0)
    key_i = lax.broadcasted_iota(jnp.int32, (KBLK, QTILE), 0)
    q_i = lax.broadcasted_iota(jnp.int32, (KBLK, QTILE), 1)
    heads = [slice(h * QTILE, (h + 1) * QTILE) for h in range(NSA_HPG)]

    m_sc[...] = jnp.full(m_sc.shape, NEG, F32)
    acc_sc[...] = jnp.zeros(acc_sc.shape, F32)
    step = (b * NSA_KV_GROUPS + g) * nt + c

    def group_blocks(gi):
        return [lst_ref[step * _list_stride(npair) + gi * SEL_UNROLL + u] for u in range(SEL_UNROLL)]

    def group_scores(gi, own_first):
        scores = []
        for u, jj in enumerate(group_blocks(gi)):
            js = jnp.maximum(jj, 0)
            rows = sel_ref[0, 0, 0, pl.ds(2 * js, 2), :]
            if own_first and u == 0:
                picked = jnp.where(key_i < SLC_LEN, rows[0:1, :], rows[1:2, :]) > 0.5
                bias = _mx(jnp.where(picked & (key_i <= q_i), 0.0, NEG))
            else:
                half = _mx(jnp.where(jj >= 0, (rows - 1.0) * -NEG, NEG))
                bias = jnp.concatenate([jnp.broadcast_to(half[0:1, :], (SLC_LEN, QTILE)),
                                        jnp.broadcast_to(half[1:2, :], (SLC_LEN, QTILE))], axis=0)
            keys = ksa_sc[pl.ds(pl.multiple_of(js * KBLK, KBLK), KBLK), :]
            scores.append(_dot(jnp.concatenate([keys, bias], axis=1), rhs))
        return scores

    def softmax_update(gi, scores):
        blocks = [jnp.maximum(jj, 0) for jj in group_blocks(gi)]
        m_new = []
        for cs in heads:
            m_old = m_sc[:, cs]
            mx_ = m_old
            for u in range(SEL_UNROLL):
                mx_ = jnp.maximum(mx_, jnp.max(scores[u][:, cs], axis=0, keepdims=True))
            acc_sc[:, cs] = jnp.exp2(m_old - mx_) * acc_sc[:, cs]
            m_sc[:, cs] = mx_
            m_new.append(mx_)
        upd = None
        for u in range(SEL_UNROLL):
            e = jnp.concatenate([_mx(jnp.exp2(scores[u][:, cs] - m_new[hi])) for hi, cs in enumerate(heads)],
                                axis=1)
            d = _dot(vst_sc[blocks[u]], e)
            upd = d if upd is None else upd + d
        acc_sc[...] += upd

    for u, s in enumerate(group_scores(0, True)):
        sa_sc[u] = s

    nwin = WINDOW // KBLK + 1
    blk0 = jnp.maximum(c - (nwin - 1), 0)
    row = lax.broadcasted_iota(jnp.int32, (nwin * KBLK, QTILE), 0)
    dist = lax.broadcasted_iota(jnp.int32, (nwin * KBLK, QTILE), 1) + (c - blk0) * KBLK - row
    bias = _mx(jnp.where((dist >= 0) & (dist < WINDOW), 0.0, NEG))
    keys = kwa_sc[pl.ds(pl.multiple_of(blk0 * KBLK, KBLK), nwin * KBLK), :]
    s = _dot(jnp.concatenate([keys, bias], axis=1), rhs)
    e_all = jnp.concatenate([_mx(jnp.exp2(s[:, cs] - jnp.max(s[:, cs], axis=0, keepdims=True)))
                             for cs in heads], axis=1)
    acc = None
    for t in range(nwin):
        d = _dot(vwt_sc[blk0 + t], e_all[t * KBLK:(t + 1) * KBLK, :])
        acc = d if acc is None else acc + d
    ow_sc[...] = acc[:dk, :] * (1.0 / jnp.maximum(acc[dk:dk + 1, :], 1e-30))

    def trip(r, carry):
        scores_b = group_scores(2 * r + 1, False)
        softmax_update(2 * r, [sa_sc[u] for u in range(SEL_UNROLL)])
        for u, s in enumerate(group_scores(2 * r + 2, False)):
            sa_sc[u] = s
        softmax_update(2 * r + 1, scores_b)
        return carry

    lax.fori_loop(0, (cnt_ref[step] + 2 * SEL_UNROLL - 1) // (2 * SEL_UNROLL), trip, 0)
    o_sel = acc_sc[:dk, :] * (1.0 / jnp.maximum(acc_sc[dk:dk + 1, :], 1e-30))
    o_win = ow_sc[...]

    gate_sc[...] = gl_ref[...].T
    ocmp = ocmp_ref[0, 0, 0]
    outs = []
    for h, cs in enumerate(heads):
        gt = jax.nn.sigmoid(gate_sc[pl.ds((g * NSA_HPG + h) * 3, 3), :])
        outs.append(gt[0:1] * ocmp[:, cs] + gt[1:2] * o_sel[:, cs] + gt[2:3] * o_win[:, cs])
    o_ref[...] = jnp.concatenate(outs, axis=0).T


def _nsa_attend(blk_list, blk_count, proj, proj_f, q_feat, pos_feat, sel_t, ocmp_t, bsz):
    g, dk = NSA_KV_GROUPS, NSA_HEAD_DIM
    t = proj.shape[0] // bsz
    nt = t // QTILE
    npair = t // KBLK
    ns = sel_t.shape[3]
    proj3 = proj.reshape(bsz, t, proj.shape[1])
    seg = D_MODEL // LANE
    gate_blk = proj_f.shape[1] // LANE - 1
    tile = lambda b, gg, c, lst, cnt: (b, gg, c, 0, 0)
    rows = lambda b, gg, c, lst, cnt: (b * nt + c, gg)
    kv = lambda k: (lambda b, gg, c, lst, cnt: (b, 0, seg + k * g + gg))
    grid_spec = pltpu.PrefetchScalarGridSpec(
        num_scalar_prefetch=2,
        grid=(bsz, g, nt),
        in_specs=[pl.BlockSpec((QTILE, NSA_HPG * dk), rows),
                  pl.BlockSpec((1, dk + QTILE, NCOL), lambda b, gg, c, lst, cnt: (gg, 0, 0)),
                  pl.BlockSpec((1, t, LANE), kv(0)),
                  pl.BlockSpec((1, t, LANE), kv(1)),
                  pl.BlockSpec((1, t, LANE), kv(2)),
                  pl.BlockSpec((1, t, LANE), kv(3)),
                  pl.BlockSpec((t, LANE), lambda b, gg, c, lst, cnt: (0, 0)),
                  pl.BlockSpec((1, 1, 1, ns, QTILE), tile),
                  pl.BlockSpec((1, 1, 1, dk, NCOL), tile),
                  pl.BlockSpec((QTILE, LANE), lambda b, gg, c, lst, cnt: (b * nt + c, gate_blk))],
        out_specs=pl.BlockSpec((QTILE, NSA_HPG * dk), rows),
        scratch_shapes=[pltpu.VMEM((t, LANE), MXU_DTYPE), pltpu.VMEM((t, LANE), MXU_DTYPE),
                        pltpu.VMEM((npair, VT_ROWS, KBLK), MXU_DTYPE),
                        pltpu.VMEM((npair, VT_ROWS, KBLK), MXU_DTYPE),
                        pltpu.VMEM((LANE, QTILE), F32),
                        pltpu.VMEM((SEL_UNROLL, KBLK, NCOL), F32),
                        pltpu.VMEM((dk, NCOL), F32),
                        pltpu.VMEM((1, NCOL), F32), pltpu.VMEM((VT_ROWS, NCOL), F32)],
    )
    return pl.pallas_call(
        _nsa_attn_kernel,
        grid_spec=grid_spec,
        out_shape=jax.ShapeDtypeStruct((bsz * t, NSA_HEADS * dk), F32),
        compiler_params=_params("parallel", "parallel", "arbitrary"),
        name="nsa_attend",
    )(blk_list, blk_count, proj, q_feat, proj3, proj3, proj3, proj3, pos_feat, sel_t, ocmp_t, proj_f)


def _nsa_mixer(x2, bsz, t, w_in, pe_k, pe_v, wk1, wk2, wv1, wv2):
    g, hpg, dk = NSA_KV_GROUPS, NSA_HPG, NSA_HEAD_DIM
    nt = t // QTILE
    d = w_in.shape[0]
    splits = np.cumsum([D_MODEL] + [NSA_KV_DIM] * 6).tolist()
    wq, wkc, wvc, wks, wvs, wkw, wvw, wgl = jnp.split(w_in, splits, axis=1)

    def lane_groups(w):
        return jnp.pad(w.reshape(d, g, dk), ((0, 0), (0, 0), (0, LANE - dk))).reshape(d, g * LANE)

    w_a = _mx(jnp.concatenate([wq] + [lane_groups(w) for w in (wks, wvs, wkw, wvw)], axis=1))
    scale = jnp.concatenate([jnp.full((1, D_MODEL), dk ** -0.5 * LOG2E, F32),
                             jnp.ones((1, w_a.shape[1] - D_MODEL), F32)], axis=1)
    w_b = _mx(jnp.concatenate([wkc, wvc, jnp.pad(wgl, ((0, 0), (0, LANE - wgl.shape[1])))], axis=1))
    proj = _matmul_scaled(x2, w_a, scale, 512, 1024, MXU_DTYPE)
    proj_f = _matmul(x2, w_b, 512, w_b.shape[1])
    kc = proj_f[:, :NSA_KV_DIM]
    vc = proj_f[:, NSA_KV_DIM:2 * NSA_KV_DIM]

    slopes = 2.0 ** (-8.0 * (jnp.arange(NSA_HEADS, dtype=F32) + 1.0) / NSA_HEADS) * LOG2E
    slopes = jnp.broadcast_to(slopes.reshape(g, 1, hpg, 1), (g, 1, hpg, QTILE)).reshape(g, 1, NCOL)
    s1 = slopes.astype(jnp.bfloat16)
    s2 = (slopes - s1.astype(F32)).astype(jnp.bfloat16)
    s3 = (slopes - s1.astype(F32) - s2.astype(F32)).astype(jnp.bfloat16)
    eye = jnp.broadcast_to(jnp.tile(jnp.eye(QTILE, dtype=jnp.bfloat16), (1, hpg))[None], (g, QTILE, NCOL))
    q_feat = jnp.concatenate([s1 * SLC_LEN, s2 * SLC_LEN, s3 * SLC_LEN, s1, s2, s3]
                             + [jnp.zeros_like(s1)] * (dk - 6) + [eye], axis=1).astype(MXU_DTYPE)
    def pos_features(kpos):
        ka_, kb_ = (kpos // SLC_LEN).astype(jnp.bfloat16), (kpos % SLC_LEN).astype(jnp.bfloat16)
        zero = jnp.zeros_like(ka_)
        return jnp.stack([ka_, ka_, ka_, kb_, kb_, kb_] + [zero] * (LANE - dk - 6), axis=1).astype(MXU_DTYPE)

    pos_feat = jnp.pad(pos_features(jnp.arange(t, dtype=jnp.int32)), ((0, 0), (dk, 0)))

    ncp = t // CMP_STRIDE
    four = lambda a: a.reshape(bsz, t, g, dk)
    k_cmp = _compress(four(kc), pe_k, wk1, wk2)
    v_cmp = _compress(four(vc), pe_v, wv1, wv2)
    cmp_feat = pos_features(jnp.arange(ncp, dtype=jnp.int32) * CMP_STRIDE + (CMP_LEN - 1))
    kcmp = jnp.concatenate([_mx(k_cmp).transpose(0, 2, 1, 3),
                            jnp.broadcast_to(cmp_feat[None, None], (bsz, g, ncp, LANE - dk))], axis=-1)
    vcmp_t = _mx(v_cmp).transpose(0, 2, 3, 1)

    ns = t // SLC_LEN
    cs = np.arange(ncp) * CMP_STRIDE
    ce = cs + CMP_LEN - 1
    bs = np.arange(ns) * SLC_LEN
    be = bs + SLC_LEN - 1
    ovl_t = ((cs[None, :] <= be[:, None]) & (ce[None, :] >= bs[:, None])).astype(np.float32)
    ovl_t[:, ncp - 1] = 0.0
    ovl_t = jnp.asarray(ovl_t, MXU_DTYPE)

    ocmp_t, sel_t, counts = _nsa_cmp_select(proj, q_feat, kcmp, vcmp_t, ovl_t)
    npair = t // KBLK
    picked = counts[:, :, :, 0, :] > 0.5
    jj = jnp.arange(npair)
    tile_i = jnp.arange(nt)[:, None]
    active = (picked[..., 0::2] | picked[..., 1::2]) & (jj[None, :] < tile_i)
    blk_count = 1 + jnp.sum(active, axis=-1).astype(jnp.int32)
    slot = jnp.cumsum(active, axis=-1)
    hit = active[..., None, :] & (slot[..., None, :] == jj[:, None])
    blk_list = jnp.sum(jnp.where(hit, jj, 0), axis=-1)
    blk_list = jnp.where(jj == 0, tile_i, blk_list)
    blk_list = jnp.where(jj < blk_count[..., None], blk_list, -1).astype(jnp.int32)

    blk_list = jnp.pad(blk_list, ((0, 0), (0, 0), (0, 0), (0, _list_stride(npair) - npair)),
                       constant_values=-1).reshape(-1)
    return _nsa_attend(blk_list, blk_count.reshape(-1), proj, proj_f, q_feat, pos_feat,
                       sel_t, ocmp_t, bsz)


def _log_sigmoid(z):
    return jnp.minimum(z, 0.0) - jnp.log1p(jnp.exp(-jnp.abs(z)))


def _gla_kernel(q_ref, k_ref, v_ref, g_ref, al_ref, w2_ref, b2_ref, hn_ref, o_ref, st_sc):
    @pl.when(pl.program_id(0) == 0)
    def _():
        st_sc[...] = jnp.zeros(st_sc.shape, F32)

    cc = GLA_CHUNK
    r_i = lax.broadcasted_iota(jnp.int32, (cc, cc), 0)
    c_i = lax.broadcasted_iota(jnp.int32, (cc, cc), 1)
    causal = r_i >= c_i
    tri = jnp.where(causal, 1.0, 0.0).astype(MXU_DTYPE)
    ref = cc // 2

    def chunk(ci, carry):
        rows = pl.ds(pl.multiple_of(ci * cc, cc), cc)
        for b in range(q_ref.shape[0]):
            z = _dot(_mx(al_ref[b, rows, :]), w2_ref[...]) + b2_ref[...]
            log_a = _log_sigmoid(z) / GLA_GATE_NORM
            a1 = _mx(log_a)
            r1 = log_a - a1.astype(F32)
            a2 = _mx(r1)
            a3 = _mx(r1 - a2.astype(F32))
            bc_all = _dot(tri, a1) + _dot(tri, a2) + _dot(tri, a3)
            for h in range(GLA_HEADS):
                ks = slice(h * GLA_DK, (h + 1) * GLA_DK)
                vs = slice(h * GLA_DV, (h + 1) * GLA_DV)
                bc = bc_all[:, ks]
                q = q_ref[b, rows, ks] * GLA_DK ** -0.5
                k = k_ref[b, rows, ks]
                v = v_ref[b, rows, vs]
                b_ref = bc[ref:ref + 1, :]
                b_last = bc[cc - 1:cc, :]
                a = _dot_nt(_mx(q * jnp.exp(bc - b_ref)), _mx(k * jnp.exp(b_ref - bc)))
                a = jnp.where(causal, a, 0.0)
                st = st_sc[b * GLA_HEADS + h]
                o = _dot(_mx(a), _mx(v)) + _dot_nt(_mx(q * jnp.exp(bc)), _mx(st))
                kl = k * jnp.exp(b_last - bc)
                st_sc[b * GLA_HEADS + h] = st * jnp.exp(b_last) + _dot(_mx(v.T), _mx(kl))
                o = o * lax.rsqrt(jnp.mean(o * o, -1, keepdims=True) + RMS_EPS)
                gg = g_ref[b, rows, vs]
                o_ref[b, rows, vs] = o * hn_ref[:, vs] * (gg * jax.nn.sigmoid(gg))
        return carry

    lax.fori_loop(0, q_ref.shape[1] // cc, chunk, 0)


def _gla_mixer(x2, bsz, t, w_in, w_gate2, b_gate2, head_norm_g):
    n_main = 2 * GLA_KEY_DIM + 2 * GLA_VAL_DIM
    w_main = _mx(w_in[:, :n_main])
    w_low = jnp.pad(_mx(w_in[:, n_main:]), ((0, 0), (0, LANE - GLA_GATE_RANK)))
    proj = _matmul(x2, w_main, 512, 1024).reshape(bsz, t, n_main)
    a_low = _matmul(x2, w_low, 512, LANE).reshape(bsz, t, LANE)
    w2p = jnp.pad(_mx(w_gate2), ((0, LANE - GLA_GATE_RANK), (0, 0)))
    ts = min(GLA_TSTEP, t)
    out = pl.pallas_call(
        _gla_kernel,
        grid=(t // ts,),
        in_specs=[pl.BlockSpec((bsz, ts, GLA_KEY_DIM), lambda s: (0, s, 0)),
                  pl.BlockSpec((bsz, ts, GLA_KEY_DIM), lambda s: (0, s, 1)),
                  pl.BlockSpec((bsz, ts, GLA_VAL_DIM), lambda s: (0, s, 1)),
                  pl.BlockSpec((bsz, ts, GLA_VAL_DIM), lambda s: (0, s, 2)),
                  pl.BlockSpec((bsz, ts, LANE), lambda s: (0, s, 0)),
                  pl.BlockSpec((LANE, GLA_KEY_DIM), lambda s: (0, 0)),
                  pl.BlockSpec((1, GLA_KEY_DIM), lambda s: (0, 0)),
                  pl.BlockSpec((1, GLA_VAL_DIM), lambda s: (0, 0))],
        out_specs=pl.BlockSpec((bsz, ts, GLA_VAL_DIM), lambda s: (0, s, 0)),
        out_shape=jax.ShapeDtypeStruct((bsz, t, GLA_VAL_DIM), F32),
        scratch_shapes=[pltpu.VMEM((bsz * GLA_HEADS, GLA_DV, GLA_DK), F32)],
        compiler_params=_params("arbitrary"),
        name="gla",
    )(proj, proj, proj, proj, a_low, w2p, b_gate2.reshape(1, -1), head_norm_g.reshape(1, -1))
    return out.reshape(bsz * t, GLA_VAL_DIM)


def _router_kernel(x_ref, wh_ref, wl_ref, o_ref):
    x = x_ref[...]
    xh = _mx(x)
    xl = _mx(x - xh.astype(F32))
    wh = wh_ref[...]
    logits = _dot(xh, wh) + _dot(xl, wh) + _dot(xh, wl_ref[...])
    lane = lax.broadcasted_iota(jnp.int32, logits.shape, 1)
    lg = jnp.where(lane < N_EXPERTS, logits, -jnp.inf)
    m1 = jnp.max(lg, axis=1, keepdims=True)
    i1 = jnp.min(jnp.where(lg == m1, lane, LANE), axis=1, keepdims=True)
    lg2 = jnp.where(lane == i1, -jnp.inf, lg)
    m2 = jnp.max(lg2, axis=1, keepdims=True)
    i2 = jnp.min(jnp.where(lg2 == m2, lane, LANE), axis=1, keepdims=True)
    e2 = jnp.exp(m2 - m1)
    den = 1.0 + e2
    w1 = 1.0 / den
    w2 = e2 / den
    out = jnp.where(lane == N_EXPERTS, i1.astype(F32), 0.0)
    out = jnp.where(lane == N_EXPERTS + 1, i2.astype(F32), out)
    out = jnp.where(lane == N_EXPERTS + 2, w1, out)
    out = jnp.where(lane == N_EXPERTS + 3, w2, out)
    o_ref[...] = out


def _moe_up_kernel(te_ref, x_ref, wg_ref, wu_ref, o_ref, wg_sc, wu_sc):
    i = pl.program_id(1)

    @pl.when((i == 0) | (te_ref[i] != te_ref[jnp.maximum(i - 1, 0)]))
    def _():
        wg_sc[...] = _mx(wg_ref[0])
        wu_sc[...] = _mx(wu_ref[0])

    _swiglu_cols(_mx(x_ref[...]), wg_sc, wu_sc, o_ref)


def _moe_down_kernel(te_ref, h_ref, w_ref, rw_ref, o_ref, w_sc):
    i = pl.program_id(0)

    @pl.when((i == 0) | (te_ref[i] != te_ref[jnp.maximum(i - 1, 0)]))
    def _():
        w_sc[...] = _mx(w_ref[0])

    o_ref[...] = _dot(h_ref[...], w_sc[...]) * rw_ref[...]


def _add_ln_kernel(x_ref, y0_ref, y1_ref, g_ref, b_ref, o_ref):
    h = DN_ALPHA * x_ref[...] + (y0_ref[...] + y1_ref[...])
    o_ref[...] = _layer_norm_rows(h, g_ref[...], b_ref[...])


def _moe_layer(x2, w_router, w_gu, w_down, ln_g, ln_b):
    m, d = x2.shape
    tm = MOE_TM
    wr = jnp.pad(w_router, ((0, 0), (0, LANE - N_EXPERTS)))
    wr_hi = _mx(wr)
    wr_lo = _mx(wr - wr_hi.astype(F32))
    rt = pl.pallas_call(
        _router_kernel,
        grid=(m // 512,),
        in_specs=[pl.BlockSpec((512, d), lambda i: (i, 0)),
                  pl.BlockSpec((d, LANE), lambda i: (0, 0)),
                  pl.BlockSpec((d, LANE), lambda i: (0, 0))],
        out_specs=pl.BlockSpec((512, LANE), lambda i: (i, 0)),
        out_shape=jax.ShapeDtypeStruct((m, LANE), F32),
        compiler_params=_params("parallel"),
        name="moe_router",
    )(x2, wr_hi, wr_lo)
    top_idx = rt[:, N_EXPERTS:N_EXPERTS + 2].astype(jnp.int32)
    top_w = rt[:, N_EXPERTS + 2:N_EXPERTS + 4]

    n_asg = m * TOP_K
    n_rows = n_asg + N_EXPERTS * tm
    n_tiles = n_rows // tm
    e_flat = top_idx.reshape(-1)
    order = jnp.argsort(e_flat, stable=True).astype(jnp.int32)
    slot = jnp.argsort(order).astype(jnp.int32)
    counts = jnp.sum((e_flat[:, None] == jnp.arange(N_EXPERTS)[None, :]).astype(jnp.int32), axis=0)
    tiles_per = (counts + tm - 1) // tm
    tile_end = jnp.cumsum(tiles_per)
    row_start = (tile_end - tiles_per) * tm
    grp_start = jnp.cumsum(counts) - counts
    tile_expert = jnp.minimum(jnp.sum((jnp.arange(n_tiles)[:, None] >= tile_end[None, :]).astype(jnp.int32),
                                      axis=1), N_EXPERTS - 1).astype(jnp.int32)
    pos = (row_start[e_flat] + slot - grp_start[e_flat]).astype(jnp.int32).reshape(m, TOP_K)
    row_e = jnp.repeat(tile_expert, tm)
    row_off = jnp.arange(n_rows, dtype=jnp.int32) - row_start[row_e].astype(jnp.int32)
    row_live = row_off < counts[row_e]
    row_asg = order[jnp.clip(grp_start[row_e].astype(jnp.int32) + row_off, 0, n_asg - 1)]
    row_token = jnp.where(row_live, row_asg // TOP_K, 0)
    row_w = jnp.where(row_live, top_w.reshape(-1)[row_asg], 0.0)

    xs = x2[row_token]
    f = w_gu.shape[2] // 2
    tf = MOE_TF
    nf = f // tf
    h = pl.pallas_call(
        _moe_up_kernel,
        grid_spec=pltpu.PrefetchScalarGridSpec(
            num_scalar_prefetch=1,
            grid=(nf, n_tiles),
            in_specs=[pl.BlockSpec((tm, d), lambda j, i, te: (i, 0)),
                      pl.BlockSpec((1, d, tf), lambda j, i, te: (te[i], 0, j)),
                      pl.BlockSpec((1, d, tf), lambda j, i, te: (te[i], 0, j + nf))],
            out_specs=pl.BlockSpec((tm, tf), lambda j, i, te: (i, j)),
            scratch_shapes=[pltpu.VMEM((d, tf), MXU_DTYPE), pltpu.VMEM((d, tf), MXU_DTYPE)],
        ),
        out_shape=jax.ShapeDtypeStruct((n_rows, f), MXU_DTYPE),
        compiler_params=_params("arbitrary", "arbitrary"),
        name="moe_up",
    )(tile_expert, xs, w_gu, w_gu)
    ys = pl.pallas_call(
        _moe_down_kernel,
        grid_spec=pltpu.PrefetchScalarGridSpec(
            num_scalar_prefetch=1,
            grid=(n_tiles,),
            in_specs=[pl.BlockSpec((tm, f), lambda i, te: (i, 0)),
                      pl.BlockSpec((1, f, d), lambda i, te: (te[i], 0, 0)),
                      pl.BlockSpec((tm, 1), lambda i, te: (i, 0))],
            out_specs=pl.BlockSpec((tm, d), lambda i, te: (i, 0)),
            scratch_shapes=[pltpu.VMEM((f, d), MXU_DTYPE)],
        ),
        out_shape=jax.ShapeDtypeStruct((n_rows, d), F32),
        compiler_params=_params("arbitrary"),
        name="moe_down",
    )(tile_expert, h, w_down, row_w.reshape(n_rows, 1))

    lo_first = top_idx[:, 0:1] < top_idx[:, 1:2]
    p0 = jnp.where(lo_first[:, 0], pos[:, 0], pos[:, 1])
    p1 = jnp.where(lo_first[:, 0], pos[:, 1], pos[:, 0])
    y0 = ys[p0]
    y1 = ys[p1]
    return pl.pallas_call(
        _add_ln_kernel,
        grid=(m // 512,),
        in_specs=[pl.BlockSpec((512, d), lambda i: (i, 0))] * 3
        + [pl.BlockSpec((1, d), lambda i: (0, 0))] * 2,
        out_specs=pl.BlockSpec((512, d), lambda i: (i, 0)),
        out_shape=jax.ShapeDtypeStruct((m, d), F32),
        compiler_params=_params("parallel"),
        name="moe_add_ln",
    )(x2, y0, y1, ln_g.reshape(1, d), ln_b.reshape(1, d))


def kernel(x, l0_w_in, l0_cmp_pe_k, l0_cmp_pe_v, l0_cmp_wk1, l0_cmp_wk2, l0_cmp_wv1, l0_cmp_wv2, l0_w_o, l0_ln1_g, l0_ln1_b, l0_ffn_w_gu, l0_ffn_w_down, l0_ln2_g, l0_ln2_b, l1_w_in, l1_w_gate2, l1_b_gate2, l1_head_norm_g, l1_w_o, l1_ln1_g, l1_ln1_b, l1_router, l1_moe_w_gu, l1_moe_w_down, l1_ln2_g, l1_ln2_b):
    bsz, t, d = x.shape
    x2 = x.reshape(bsz * t, d)

    o = _nsa_mixer(x2, bsz, t, l0_w_in, l0_cmp_pe_k, l0_cmp_pe_v, l0_cmp_wk1, l0_cmp_wk2,
                   l0_cmp_wv1, l0_cmp_wv2)
    x2 = _matmul_res_ln(o, _mx(l0_w_o), x2, l0_ln1_g, l0_ln1_b, 512)
    hmid = _swiglu_up(x2, _mx(l0_ffn_w_gu), 512, FFN_DENSE)
    x2 = _matmul_res_ln(hmid, _mx(l0_ffn_w_down), x2, l0_ln2_g, l0_ln2_b, 512)

    o = _gla_mixer(x2, bsz, t, l1_w_in, l1_w_gate2, l1_b_gate2, l1_head_norm_g)
    x2 = _matmul_res_ln(o, _mx(l1_w_o), x2, l1_ln1_g, l1_ln1_b, 512)
    x2 = _moe_layer(x2, l1_router, l1_moe_w_gu, l1_moe_w_down, l1_ln2_g, l1_ln2_b)
    return x2.reshape(bsz, t, d)
```

```python
import functools

import numpy as np
import jax
import jax.numpy as jnp
from jax import lax
from jax.experimental import pallas as pl
from jax.experimental.pallas import tpu as pltpu

F32 = jnp.float32
MXU_DTYPE = jnp.bfloat16

D_MODEL = 1024
DEPTH = 2
DN_ALPHA = (2 * DEPTH) ** 0.25
LN_EPS = 1e-5
RMS_EPS = 1e-6

NSA_HEADS = 16
NSA_HEAD_DIM = 64
NSA_KV_GROUPS = 4
NSA_HPG = NSA_HEADS // NSA_KV_GROUPS
NSA_KV_DIM = NSA_KV_GROUPS * NSA_HEAD_DIM
CMP_LEN = 32
CMP_STRIDE = 16
CMP_HIDDEN = 256
SLC_LEN = 64
N_SEL = 16
WINDOW = 512
QTILE = 128
KBLK = 128
NCOL = NSA_HPG * QTILE
CMP_CHUNK = 128
SEL_UNROLL = 6
VT_ROWS = 80
LOG2E = 1.4426950408889634

GLA_HEADS = 4
GLA_KEY_DIM = D_MODEL // 2
GLA_VAL_DIM = D_MODEL
GLA_DK = GLA_KEY_DIM // GLA_HEADS
GLA_DV = GLA_VAL_DIM // GLA_HEADS
GLA_GATE_RANK = 16
GLA_GATE_NORM = 16.0
GLA_CHUNK = 64
GLA_TSTEP = 256

FFN_DENSE = 2816
N_EXPERTS = 8
TOP_K = 2
FFN_EXPERT = 3584
MOE_TM = 256
MOE_TF = 1792
SWIGLU_CHUNK = 256
MM_CHUNK = 512

LANE = 128
NEG = -1e30
VMEM_LIMIT = 56 * 1024 * 1024


def _params(*sem):
    return pltpu.CompilerParams(dimension_semantics=sem, vmem_limit_bytes=VMEM_LIMIT)


def _mx(a):
    return a.astype(MXU_DTYPE)


def _dot(a, b):
    return jnp.dot(a, b, preferred_element_type=F32)


def _dot_nt(a, b):
    return lax.dot_general(a, b, (((1,), (1,)), ((), ())), preferred_element_type=F32)


def _col_chunks(n):
    step = MM_CHUNK if n % MM_CHUNK == 0 else n
    return [slice(j, j + step) for j in range(0, n, step)]


def _mm_kernel(x_ref, w_ref, o_ref):
    x = _mx(x_ref[...])
    for cs in _col_chunks(o_ref.shape[1]):
        o_ref[:, cs] = _dot(x, w_ref[:, cs]).astype(o_ref.dtype)


def _matmul(x, w, tm, tn, out_dtype=F32):
    m, k = x.shape
    n = w.shape[1]
    return pl.pallas_call(
        _mm_kernel,
        grid=(m // tm, n // tn),
        in_specs=[pl.BlockSpec((tm, k), lambda i, j: (i, 0)),
                  pl.BlockSpec((k, tn), lambda i, j: (0, j))],
        out_specs=pl.BlockSpec((tm, tn), lambda i, j: (i, j)),
        out_shape=jax.ShapeDtypeStruct((m, n), out_dtype),
        compiler_params=_params("parallel", "parallel"),
        name="matmul",
    )(x, w)


def _mm_scaled_kernel(x_ref, w_ref, s_ref, o_ref):
    x = _mx(x_ref[...])
    for cs in _col_chunks(o_ref.shape[1]):
        o_ref[:, cs] = (_dot(x, w_ref[:, cs]) * s_ref[:, cs]).astype(o_ref.dtype)


def _matmul_scaled(x, w, col_scale, tm, tn, out_dtype):
    m, k = x.shape
    n = w.shape[1]
    return pl.pallas_call(
        _mm_scaled_kernel,
        grid=(m // tm, n // tn),
        in_specs=[pl.BlockSpec((tm, k), lambda i, j: (i, 0)),
                  pl.BlockSpec((k, tn), lambda i, j: (0, j)),
                  pl.BlockSpec((1, tn), lambda i, j: (0, j))],
        out_specs=pl.BlockSpec((tm, tn), lambda i, j: (i, j)),
        out_shape=jax.ShapeDtypeStruct((m, n), out_dtype),
        compiler_params=_params("parallel", "parallel"),
        name="matmul_scaled",
    )(x, w, col_scale)


def _layer_norm_rows(h, g, b):
    mu = jnp.mean(h, -1, keepdims=True)
    d = h - mu
    var = jnp.mean(d * d, -1, keepdims=True)
    return d * lax.rsqrt(var + LN_EPS) * g + b


def _mm_ln_kernel(x_ref, w_ref, r_ref, g_ref, b_ref, o_ref):
    y = _dot(_mx(x_ref[...]), w_ref[...])
    o_ref[...] = _layer_norm_rows(DN_ALPHA * r_ref[...] + y, g_ref[...], b_ref[...])


def _matmul_res_ln(x, w, res, g, b, tm):
    m, k = x.shape
    n = w.shape[1]
    return pl.pallas_call(
        _mm_ln_kernel,
        grid=(m // tm,),
        in_specs=[pl.BlockSpec((tm, k), lambda i: (i, 0)),
                  pl.BlockSpec((k, n), lambda i: (0, 0)),
                  pl.BlockSpec((tm, n), lambda i: (i, 0)),
                  pl.BlockSpec((1, n), lambda i: (0, 0)),
                  pl.BlockSpec((1, n), lambda i: (0, 0))],
        out_specs=pl.BlockSpec((tm, n), lambda i: (i, 0)),
        out_shape=jax.ShapeDtypeStruct((m, n), F32),
        compiler_params=_params("parallel"),
        name="matmul_res_ln",
    )(x, w, res, g.reshape(1, n), b.reshape(1, n))


def _swiglu_cols(x, wg_ref, wu_ref, o_ref):
    n = o_ref.shape[1]
    step = SWIGLU_CHUNK if n % SWIGLU_CHUNK == 0 else LANE
    for j in range(0, n, step):
        a = _dot(x, wg_ref[:, j:j + step])
        b = _dot(x, wu_ref[:, j:j + step])
        o_ref[:, j:j + step] = (a * jax.nn.sigmoid(a) * b).astype(o_ref.dtype)


def _swiglu_up_kernel(x_ref, wg_ref, wu_ref, o_ref):
    _swiglu_cols(_mx(x_ref[...]), wg_ref, wu_ref, o_ref)


def _swiglu_up(x, w_gu, tm, tn):
    m, k = x.shape
    f = w_gu.shape[1] // 2
    nj = f // tn
    return pl.pallas_call(
        _swiglu_up_kernel,
        grid=(m // tm, nj),
        in_specs=[pl.BlockSpec((tm, k), lambda i, j: (i, 0)),
                  pl.BlockSpec((k, tn), lambda i, j: (0, j)),
                  pl.BlockSpec((k, tn), lambda i, j: (0, j + nj))],
        out_specs=pl.BlockSpec((tm, tn), lambda i, j: (i, j)),
        out_shape=jax.ShapeDtypeStruct((m, f), MXU_DTYPE),
        compiler_params=_params("parallel", "parallel"),
        name="swiglu_up",
    )(x, w_gu, w_gu)


def _cmp_up_kernel(c_ref, pea_ref, peb_ref, wa_ref, wb_ref, p_ref, q_ref):
    c = c_ref[...]
    p_ref[...] = _dot(_mx(c + pea_ref[...]), wa_ref[...])
    q_ref[...] = _dot(_mx(c + peb_ref[...]), wb_ref[...])


def _cmp_down_kernel(p_ref, q_ref, w_ref, o_ref):
    h = jax.nn.gelu(p_ref[...] + q_ref[...])
    o_ref[...] = _dot(_mx(h), w_ref[...])


def _compress(a, pe, w1, w2):
    bsz, t, g, dk = a.shape
    nch = t // CMP_STRIDE
    half = CMP_STRIDE * dk
    chunks = a.reshape(bsz, nch, CMP_STRIDE, g, dk).transpose(0, 1, 3, 2, 4).reshape(bsz * nch * g, half)
    rows = chunks.shape[0]
    tm = min(512, rows)
    pe_a = pe[:CMP_STRIDE].reshape(1, half)
    pe_b = pe[CMP_STRIDE:].reshape(1, half)
    w1m = _mx(w1)
    p, q = pl.pallas_call(
        _cmp_up_kernel,
        grid=(rows // tm,),
        in_specs=[pl.BlockSpec((tm, half), lambda i: (i, 0)),
                  pl.BlockSpec((1, half), lambda i: (0, 0)),
                  pl.BlockSpec((1, half), lambda i: (0, 0)),
                  pl.BlockSpec((half, CMP_HIDDEN), lambda i: (0, 0)),
                  pl.BlockSpec((half, CMP_HIDDEN), lambda i: (0, 0))],
        out_specs=[pl.BlockSpec((tm, CMP_HIDDEN), lambda i: (i, 0)),
                   pl.BlockSpec((tm, CMP_HIDDEN), lambda i: (i, 0))],
        out_shape=[jax.ShapeDtypeStruct((rows, CMP_HIDDEN), F32)] * 2,
        compiler_params=_params("parallel"),
        name="cmp_up",
    )(chunks, pe_a, pe_b, w1m[:half], w1m[half:])
    q = q.reshape(bsz, nch, g, CMP_HIDDEN)
    q = jnp.concatenate([q[:, 1:], jnp.zeros_like(q[:, :1])], axis=1).reshape(rows, CMP_HIDDEN)
    w2p = jnp.pad(_mx(w2), ((0, 0), (0, LANE - dk)))
    out = pl.pallas_call(
        _cmp_down_kernel,
        grid=(rows // tm,),
        in_specs=[pl.BlockSpec((tm, CMP_HIDDEN), lambda i: (i, 0)),
                  pl.BlockSpec((tm, CMP_HIDDEN), lambda i: (i, 0)),
                  pl.BlockSpec((CMP_HIDDEN, LANE), lambda i: (0, 0))],
        out_specs=pl.BlockSpec((tm, LANE), lambda i: (i, 0)),
        out_shape=jax.ShapeDtypeStruct((rows, LANE), F32),
        compiler_params=_params("parallel"),
        name="cmp_down",
    )(p, q, w2p)
    out = out[:, :dk].reshape(bsz, nch, g, dk)
    valid = (jnp.arange(nch) < nch - 1)[None, :, None, None]
    return jnp.where(valid, out, 0.0)


def _queries_t(q_blk):
    qt = q_blk.astype(F32).T
    dk = NSA_HEAD_DIM
    return _mx(jnp.concatenate([qt[h * dk:(h + 1) * dk, :] for h in range(NSA_HPG)], axis=1))


def _nsa_cmp_kernel(q_ref, qf_ref, kc_ref, vct_ref, ovl_ref, ocmp_ref, sel_ref, flag_ref,
                    imp_ref, key_ref, cnt_ref):
    c = pl.program_id(2)
    ncp = kc_ref.shape[2]
    ns = ovl_ref.shape[0]
    dk = NSA_HEAD_DIM
    rhs = jnp.concatenate([_queries_t(q_ref[...]), qf_ref[0]], axis=0)
    chunk = min(CMP_CHUNK, ncp)
    n_chunks = (c * (QTILE // CMP_STRIDE) + (QTILE - CMP_LEN) // CMP_STRIDE + chunk) // chunk

    def cmp_branch(rows):
        n_i = lax.broadcasted_iota(jnp.int32, (rows, QTILE), 0)
        tq = c * QTILE + lax.broadcasted_iota(jnp.int32, (rows, QTILE), 1)
        bias = _mx(jnp.where(n_i * CMP_STRIDE + (CMP_LEN - 1) <= tq, 0.0, NEG))
        s = _dot(jnp.concatenate([kc_ref[0, 0, :rows, :], bias], axis=1), rhs)
        e = jnp.exp2(s - jnp.max(s, axis=0, keepdims=True))
        seen = jnp.concatenate([tq[0:1, :]] * NSA_HPG, axis=1) >= CMP_LEN - 1
        r = jnp.where(seen, 1.0 / jnp.maximum(jnp.sum(e, axis=0, keepdims=True), 1e-30), 0.0)
        ocmp_ref[0, 0, 0] = _dot(vct_ref[0, 0, :, :rows], _mx(e)) * r
        psum = e[:, 0:QTILE] * r[:, 0:QTILE]
        for h in range(1, NSA_HPG):
            psum = psum + e[:, h * QTILE:(h + 1) * QTILE] * r[:, h * QTILE:(h + 1) * QTILE]
        hi = _mx(psum)
        lo = _mx(psum - hi.astype(F32))
        ovl = ovl_ref[:, :rows]
        imp_ref[...] = _dot(ovl, hi) + _dot(ovl, lo)

    for k in range(1, ncp // chunk + 1):
        pl.when(n_chunks == k)(functools.partial(cmp_branch, k * chunk))

    j_i = lax.broadcasted_iota(jnp.int32, (ns, QTILE), 0)
    tq2 = c * QTILE + lax.broadcasted_iota(jnp.int32, (ns, QTILE), 1)
    valid = j_i * SLC_LEN <= tq2
    cur = tq2 >> (SLC_LEN.bit_length() - 1)
    forced = valid & ((j_i == 0) | (j_i == cur) | (j_i == cur - 1))
    score = jnp.where(forced, 1e9, jnp.where(valid, imp_ref[...], -1e9))
    bits = lax.bitcast_convert_type(score, jnp.int32)
    key_ref[...] = jnp.where(bits < 0, bits ^ 0x7FFFFFFF, bits)
    cnt_ref[...] = jnp.zeros(cnt_ref.shape, jnp.int32)

    sub = lax.broadcasted_iota(jnp.int32, (8, QTILE), 0)
    n_it = (c * (QTILE // SLC_LEN) + QTILE // SLC_LEN + 7) // 8

    def it_body(it, carry):
        base = pl.multiple_of(it * 8, 8)
        rows8 = key_ref[pl.ds(base, 8), :]
        rb = [jnp.broadcast_to(rows8[il:il + 1, :], (8, QTILE)) for il in range(8)]
        for jt in range(ns // 8):
            js = slice(jt * 8, (jt + 1) * 8)
            thr = key_ref[js, :] - jnp.where(jt > it, 1, 0)
            acc = cnt_ref[js, :]
            for il in range(8):
                acc = acc + jnp.where(rb[il] > thr, 1, 0)
            cnt_ref[js, :] = acc
        corr = jnp.zeros((8, QTILE), jnp.int32)
        for il in range(8):
            corr = corr + jnp.where((rb[il] == rows8) & (sub > il), 1, 0)
        cnt_ref[pl.ds(base, 8), :] += corr
        return carry

    lax.fori_loop(0, n_it, it_body, 0)
    sel = jnp.where(cnt_ref[...] < min(N_SEL, ns), 1.0, 0.0)
    sel_ref[0, 0, 0] = sel
    flag_ref[0, 0, 0] = _dot_nt(jnp.ones((8, QTILE), MXU_DTYPE), _mx(sel))


def _nsa_cmp_select(proj, q_feat, kcmp, vcmp_t, ovl_t):
    bsz, g, ncp, _ = kcmp.shape
    dk = NSA_HEAD_DIM
    nt = proj.shape[0] // (bsz * QTILE)
    ns = ovl_t.shape[0]
    return pl.pallas_call(
        _nsa_cmp_kernel,
        grid=(bsz, g, nt),
        in_specs=[pl.BlockSpec((QTILE, NSA_HPG * dk), lambda b, gg, c: (b * nt + c, gg)),
                  pl.BlockSpec((1, dk + QTILE, NCOL), lambda b, gg, c: (gg, 0, 0)),
                  pl.BlockSpec((1, 1, ncp, LANE), lambda b, gg, c: (b, gg, 0, 0)),
                  pl.BlockSpec((1, 1, dk, ncp), lambda b, gg, c: (b, gg, 0, 0)),
                  pl.BlockSpec((ns, ncp), lambda b, gg, c: (0, 0))],
        out_specs=[pl.BlockSpec((1, 1, 1, dk, NCOL), lambda b, gg, c: (b, gg, c, 0, 0)),
                   pl.BlockSpec((1, 1, 1, ns, QTILE), lambda b, gg, c: (b, gg, c, 0, 0)),
                   pl.BlockSpec((1, 1, 1, 8, ns), lambda b, gg, c: (b, gg, c, 0, 0))],
        out_shape=[jax.ShapeDtypeStruct((bsz, g, nt, dk, NCOL), F32),
                   jax.ShapeDtypeStruct((bsz, g, nt, ns, QTILE), F32),
                   jax.ShapeDtypeStruct((bsz, g, nt, 8, ns), F32)],
        scratch_shapes=[pltpu.VMEM((ns, QTILE), F32), pltpu.VMEM((ns, QTILE), jnp.int32),
                        pltpu.VMEM((ns, QTILE), jnp.int32)],
        compiler_params=_params("parallel", "parallel", "parallel"),
        name="nsa_cmp_select",
    )(proj, q_feat, kcmp, vcmp_t, ovl_t)


def _list_stride(npair):
    trip = 2 * SEL_UNROLL
    return -(-npair // trip) * trip + SEL_UNROLL


def _values_t(v_blk):
    vt = v_blk.astype(F32).T[:VT_ROWS, :]
    ones_row = lax.broadcasted_iota(jnp.int32, vt.shape, 0) == NSA_HEAD_DIM
    return _mx(jnp.where(ones_row, 1.0, vt))


def _nsa_attn_kernel(lst_ref, cnt_ref, q_ref, qf_ref, ks_ref, vs_ref, kw_ref, vw_ref, pos_ref, sel_ref,
                     ocmp_ref, gl_ref, o_ref, ksa_sc, kwa_sc, vst_sc, vwt_sc, gate_sc, sa_sc, ow_sc, m_sc, acc_sc):
    b = pl.program_id(0)
    g = pl.program_id(1)
    c = pl.program_id(2)
    nt = pl.num_programs(2)
    npair = ks_ref.shape[1] // KBLK
    dk = NSA_HEAD_DIM

    @pl.when(c == 0)
    def _():
        def prep(j, carry):
            rows = pl.ds(pl.multiple_of(j * KBLK, KBLK), KBLK)
            feats = pos_ref[rows, :]
            ksa_sc[rows, :] = ks_ref[0, rows, :] + feats
            kwa_sc[rows, :] = kw_ref[0, rows, :] + feats
            vst_sc[j] = _values_t(vs_ref[0, rows, :])
            vwt_sc[j] = _values_t(vw_ref[0, rows, :])
            return carry

        lax.fori_loop(0, npair, prep, 0)

    rhs = jnp.concatenate([_queries_t(q_ref[...]), qf_ref[0]], axis=0)
    key_i = lax.broadcasted_iota(jnp.int32, (KBLK, QTILE), 0)
    q_i = lax.broadcasted_iota(jnp.int32, (KBLK, QTILE), 1)
    heads = [slice(h * QTILE, (h + 1) * QTILE) for h in range(NSA_HPG)]

    m_sc[...] = jnp.full(m_sc.shape, NEG, F32)
    acc_sc[...] = jnp.zeros(acc_sc.shape, F32)
    step = (b * NSA_KV_GROUPS + g) * nt + c

    def group_blocks(gi):
        return [lst_ref[step * _list_stride(npair) + gi * SEL_UNROLL + u] for u in range(SEL_UNROLL)]

    def group_scores(gi, own_first):
        scores = []
        for u, jj in enumerate(group_blocks(gi)):
            js = jnp.maximum(jj, 0)
            rows = sel_ref[0, 0, 0, pl.ds(2 * js, 2), :]
            if own_first and u == 0:
                picked = jnp.where(key_i < SLC_LEN, rows[0:1, :], rows[1:2, :]) > 0.5
                bias = _mx(jnp.where(picked & (key_i <= q_i), 0.0, NEG))
            else:
                half = _mx(jnp.where(jj >= 0, (rows - 1.0) * -NEG, NEG))
                bias = jnp.concatenate([jnp.broadcast_to(half[0:1, :], (SLC_LEN, QTILE)),
                                        jnp.broadcast_to(half[1:2, :], (SLC_LEN, QTILE))], axis=0)
            keys = ksa_sc[pl.ds(pl.multiple_of(js * KBLK, KBLK), KBLK), :]
            scores.append(_dot(jnp.concatenate([keys, bias], axis=1), rhs))
        return scores

    def softmax_update(gi, scores):
        blocks = [jnp.maximum(jj, 0) for jj in group_blocks(gi)]
        m_new = []
        for cs in heads:
            m_old = m_sc[:, cs]
            mx_ = m_old
            for u in range(SEL_UNROLL):
                mx_ = jnp.maximum(mx_, jnp.max(scores[u][:, cs], axis=0, keepdims=True))
            acc_sc[:, cs] = jnp.exp2(m_old - mx_) * acc_sc[:, cs]
            m_sc[:, cs] = mx_
            m_new.append(mx_)
        upd = None
        for u in range(SEL_UNROLL):
            e = jnp.concatenate([_mx(jnp.exp2(scores[u][:, cs] - m_new[hi])) for hi, cs in enumerate(heads)],
                                axis=1)
            d = _dot(vst_sc[blocks[u]], e)
            upd = d if upd is None else upd + d
        acc_sc[...] += upd

    for u, s in enumerate(group_scores(0, True)):
        sa_sc[u] = s

    nwin = WINDOW // KBLK + 1
    blk0 = jnp.maximum(c - (nwin - 1), 0)
    row = lax.broadcasted_iota(jnp.int32, (nwin * KBLK, QTILE), 0)
    dist = lax.broadcasted_iota(jnp.int32, (nwin * KBLK, QTILE), 1) + (c - blk0) * KBLK - row
    bias = _mx(jnp.where((dist >= 0) & (dist < WINDOW), 0.0, NEG))
    keys = kwa_sc[pl.ds(pl.multiple_of(blk0 * KBLK, KBLK), nwin * KBLK), :]
    s = _dot(jnp.concatenate([keys, bias], axis=1), rhs)
    e_all = jnp.concatenate([_mx(jnp.exp2(s[:, cs] - jnp.max(s[:, cs], axis=0, keepdims=True)))
                             for cs in heads], axis=1)
    acc = None
    for t in range(nwin):
        d = _dot(vwt_sc[blk0 + t], e_all[t * KBLK:(t + 1) * KBLK, :])
        acc = d if acc is None else acc + d
    ow_sc[...] = acc[:dk, :] * (1.0 / jnp.maximum(acc[dk:dk + 1, :], 1e-30))

    def trip(r, carry):
        scores_b = group_scores(2 * r + 1, False)
        softmax_update(2 * r, [sa_sc[u] for u in range(SEL_UNROLL)])
        for u, s in enumerate(group_scores(2 * r + 2, False)):
            sa_sc[u] = s
        softmax_update(2 * r + 1, scores_b)
        return carry

    trip(0, 0)
    lax.fori_loop(1, (cnt_ref[step] + 2 * SEL_UNROLL - 1) // (2 * SEL_UNROLL), trip, 0)
    o_sel = acc_sc[:dk, :] * (1.0 / jnp.maximum(acc_sc[dk:dk + 1, :], 1e-30))
    o_win = ow_sc[...]

    gate_sc[...] = gl_ref[...].T
    ocmp = ocmp_ref[0, 0, 0]
    outs = []
    for h, cs in enumerate(heads):
        gt = jax.nn.sigmoid(gate_sc[pl.ds((g * NSA_HPG + h) * 3, 3), :])
        outs.append(gt[0:1] * ocmp[:, cs] + gt[1:2] * o_sel[:, cs] + gt[2:3] * o_win[:, cs])
    o_ref[...] = jnp.concatenate(outs, axis=0).T


def _nsa_attend(blk_list, blk_count, proj, proj_f, q_feat, pos_feat, sel_t, ocmp_t, bsz):
    g, dk = NSA_KV_GROUPS, NSA_HEAD_DIM
    t = proj.shape[0] // bsz
    nt = t // QTILE
    npair = t // KBLK
    ns = sel_t.shape[3]
    proj3 = proj.reshape(bsz, t, proj.shape[1])
    seg = D_MODEL // LANE
    gate_blk = proj_f.shape[1] // LANE - 1
    tile = lambda b, gg, c, lst, cnt: (b, gg, c, 0, 0)
    rows = lambda b, gg, c, lst, cnt: (b * nt + c, gg)
    kv = lambda k: (lambda b, gg, c, lst, cnt: (b, 0, seg + k * g + gg))
    grid_spec = pltpu.PrefetchScalarGridSpec(
        num_scalar_prefetch=2,
        grid=(bsz, g, nt),
        in_specs=[pl.BlockSpec((QTILE, NSA_HPG * dk), rows),
                  pl.BlockSpec((1, dk + QTILE, NCOL), lambda b, gg, c, lst, cnt: (gg, 0, 0)),
                  pl.BlockSpec((1, t, LANE), kv(0)),
                  pl.BlockSpec((1, t, LANE), kv(1)),
                  pl.BlockSpec((1, t, LANE), kv(2)),
                  pl.BlockSpec((1, t, LANE), kv(3)),
                  pl.BlockSpec((t, LANE), lambda b, gg, c, lst, cnt: (0, 0)),
                  pl.BlockSpec((1, 1, 1, ns, QTILE), tile),
                  pl.BlockSpec((1, 1, 1, dk, NCOL), tile),
                  pl.BlockSpec((QTILE, LANE), lambda b, gg, c, lst, cnt: (b * nt + c, gate_blk))],
        out_specs=pl.BlockSpec((QTILE, NSA_HPG * dk), rows),
        scratch_shapes=[pltpu.VMEM((t, LANE), MXU_DTYPE), pltpu.VMEM((t, LANE), MXU_DTYPE),
                        pltpu.VMEM((npair, VT_ROWS, KBLK), MXU_DTYPE),
                        pltpu.VMEM((npair, VT_ROWS, KBLK), MXU_DTYPE),
                        pltpu.VMEM((LANE, QTILE), F32),
                        pltpu.VMEM((SEL_UNROLL, KBLK, NCOL), F32),
                        pltpu.VMEM((dk, NCOL), F32),
                        pltpu.VMEM((1, NCOL), F32), pltpu.VMEM((VT_ROWS, NCOL), F32)],
    )
    return pl.pallas_call(
        _nsa_attn_kernel,
        grid_spec=grid_spec,
        out_shape=jax.ShapeDtypeStruct((bsz * t, NSA_HEADS * dk), F32),
        compiler_params=_params("parallel", "parallel", "arbitrary"),
        name="nsa_attend",
    )(blk_list, blk_count, proj, q_feat, proj3, proj3, proj3, proj3, pos_feat, sel_t, ocmp_t, proj_f)


def _nsa_mixer(x2, bsz, t, w_in, pe_k, pe_v, wk1, wk2, wv1, wv2):
    g, hpg, dk = NSA_KV_GROUPS, NSA_HPG, NSA_HEAD_DIM
    nt = t // QTILE
    d = w_in.shape[0]
    splits = np.cumsum([D_MODEL] + [NSA_KV_DIM] * 6).tolist()
    wq, wkc, wvc, wks, wvs, wkw, wvw, wgl = jnp.split(w_in, splits, axis=1)

    def lane_groups(w):
        return jnp.pad(w.reshape(d, g, dk), ((0, 0), (0, 0), (0, LANE - dk))).reshape(d, g * LANE)

    w_a = _mx(jnp.concatenate([wq] + [lane_groups(w) for w in (wks, wvs, wkw, wvw)], axis=1))
    scale = jnp.concatenate([jnp.full((1, D_MODEL), dk ** -0.5 * LOG2E, F32),
                             jnp.ones((1, w_a.shape[1] - D_MODEL), F32)], axis=1)
    w_b = _mx(jnp.concatenate([wkc, wvc, jnp.pad(wgl, ((0, 0), (0, LANE - wgl.shape[1])))], axis=1))
    proj = _matmul_scaled(x2, w_a, scale, 512, w_a.shape[1], MXU_DTYPE)
    proj_f = _matmul(x2, w_b, 512, w_b.shape[1])
    kc = proj_f[:, :NSA_KV_DIM]
    vc = proj_f[:, NSA_KV_DIM:2 * NSA_KV_DIM]

    slopes = 2.0 ** (-8.0 * (jnp.arange(NSA_HEADS, dtype=F32) + 1.0) / NSA_HEADS) * LOG2E
    slopes = jnp.broadcast_to(slopes.reshape(g, 1, hpg, 1), (g, 1, hpg, QTILE)).reshape(g, 1, NCOL)
    s1 = slopes.astype(jnp.bfloat16)
    s2 = (slopes - s1.astype(F32)).astype(jnp.bfloat16)
    s3 = (slopes - s1.astype(F32) - s2.astype(F32)).astype(jnp.bfloat16)
    eye = jnp.broadcast_to(jnp.tile(jnp.eye(QTILE, dtype=jnp.bfloat16), (1, hpg))[None], (g, QTILE, NCOL))
    q_feat = jnp.concatenate([s1 * SLC_LEN, s2 * SLC_LEN, s3 * SLC_LEN, s1, s2, s3]
                             + [jnp.zeros_like(s1)] * (dk - 6) + [eye], axis=1).astype(MXU_DTYPE)
    def pos_features(kpos):
        ka_, kb_ = (kpos // SLC_LEN).astype(jnp.bfloat16), (kpos % SLC_LEN).astype(jnp.bfloat16)
        zero = jnp.zeros_like(ka_)
        return jnp.stack([ka_, ka_, ka_, kb_, kb_, kb_] + [zero] * (LANE - dk - 6), axis=1).astype(MXU_DTYPE)

    pos_feat = jnp.pad(pos_features(jnp.arange(t, dtype=jnp.int32)), ((0, 0), (dk, 0)))

    ncp = t // CMP_STRIDE
    four = lambda a: a.reshape(bsz, t, g, dk)
    k_cmp = _compress(four(kc), pe_k, wk1, wk2)
    v_cmp = _compress(four(vc), pe_v, wv1, wv2)
    cmp_feat = pos_features(jnp.arange(ncp, dtype=jnp.int32) * CMP_STRIDE + (CMP_LEN - 1))
    kcmp = jnp.concatenate([_mx(k_cmp).transpose(0, 2, 1, 3),
                            jnp.broadcast_to(cmp_feat[None, None], (bsz, g, ncp, LANE - dk))], axis=-1)
    vcmp_t = _mx(v_cmp).transpose(0, 2, 3, 1)

    ns = t // SLC_LEN
    cs = np.arange(ncp) * CMP_STRIDE
    ce = cs + CMP_LEN - 1
    bs = np.arange(ns) * SLC_LEN
    be = bs + SLC_LEN - 1
    ovl_t = ((cs[None, :] <= be[:, None]) & (ce[None, :] >= bs[:, None])).astype(np.float32)
    ovl_t[:, ncp - 1] = 0.0
    ovl_t = jnp.asarray(ovl_t, MXU_DTYPE)

    ocmp_t, sel_t, counts = _nsa_cmp_select(proj, q_feat, kcmp, vcmp_t, ovl_t)
    npair = t // KBLK
    picked = counts[:, :, :, 0, :] > 0.5
    jj = jnp.arange(npair)
    tile_i = jnp.arange(nt)[:, None]
    active = (picked[..., 0::2] | picked[..., 1::2]) & (jj[None, :] < tile_i)
    blk_count = 1 + jnp.sum(active, axis=-1).astype(jnp.int32)
    slot = jnp.cumsum(active, axis=-1)
    hit = active[..., None, :] & (slot[..., None, :] == jj[:, None])
    blk_list = jnp.sum(jnp.where(hit, jj, 0), axis=-1)
    blk_list = jnp.where(jj == 0, tile_i, blk_list)
    blk_list = jnp.where(jj < blk_count[..., None], blk_list, -1).astype(jnp.int32)

    blk_list = jnp.pad(blk_list, ((0, 0), (0, 0), (0, 0), (0, _list_stride(npair) - npair)),
                       constant_values=-1).reshape(-1)
    return _nsa_attend(blk_list, blk_count.reshape(-1), proj, proj_f, q_feat, pos_feat,
                       sel_t, ocmp_t, bsz)


def _log_sigmoid(z):
    return jnp.minimum(z, 0.0) - jnp.log1p(jnp.exp(-jnp.abs(z)))


def _gla_kernel(q_ref, k_ref, v_ref, g_ref, al_ref, w2_ref, b2_ref, hn_ref, o_ref, st_sc):
    @pl.when(pl.program_id(0) == 0)
    def _():
        st_sc[...] = jnp.zeros(st_sc.shape, F32)

    cc = GLA_CHUNK
    r_i = lax.broadcasted_iota(jnp.int32, (cc, cc), 0)
    c_i = lax.broadcasted_iota(jnp.int32, (cc, cc), 1)
    causal = r_i >= c_i
    tri = jnp.where(causal, 1.0, 0.0).astype(MXU_DTYPE)
    ref = cc // 2

    def chunk(ci, carry):
        rows = pl.ds(pl.multiple_of(ci * cc, cc), cc)
        for b in range(q_ref.shape[0]):
            z = _dot(_mx(al_ref[b, rows, :]), w2_ref[...]) + b2_ref[...]
            log_a = _log_sigmoid(z) / GLA_GATE_NORM
            a1 = _mx(log_a)
            r1 = log_a - a1.astype(F32)
            a2 = _mx(r1)
            a3 = _mx(r1 - a2.astype(F32))
            bc_all = _dot(tri, a1) + _dot(tri, a2) + _dot(tri, a3)
            for h in range(GLA_HEADS):
                ks = slice(h * GLA_DK, (h + 1) * GLA_DK)
                vs = slice(h * GLA_DV, (h + 1) * GLA_DV)
                bc = bc_all[:, ks]
                q = q_ref[b, rows, ks] * GLA_DK ** -0.5
                k = k_ref[b, rows, ks]
                v = v_ref[b, rows, vs]
                b_ref = bc[ref:ref + 1, :]
                b_last = bc[cc - 1:cc, :]
                a = _dot_nt(_mx(q * jnp.exp(bc - b_ref)), _mx(k * jnp.exp(b_ref - bc)))
                a = jnp.where(causal, a, 0.0)
                st = st_sc[b * GLA_HEADS + h]
                o = _dot(_mx(a), _mx(v)) + _dot_nt(_mx(q * jnp.exp(bc)), _mx(st))
                kl = k * jnp.exp(b_last - bc)
                st_sc[b * GLA_HEADS + h] = st * jnp.exp(b_last) + _dot(_mx(v.T), _mx(kl))
                o = o * lax.rsqrt(jnp.mean(o * o, -1, keepdims=True) + RMS_EPS)
                gg = g_ref[b, rows, vs]
                o_ref[b, rows, vs] = o * hn_ref[:, vs] * (gg * jax.nn.sigmoid(gg))
        return carry

    lax.fori_loop(0, q_ref.shape[1] // cc, chunk, 0)


def _gla_mixer(x2, bsz, t, w_in, w_gate2, b_gate2, head_norm_g):
    n_main = 2 * GLA_KEY_DIM + 2 * GLA_VAL_DIM
    w_main = _mx(w_in[:, :n_main])
    w_low = jnp.pad(_mx(w_in[:, n_main:]), ((0, 0), (0, LANE - GLA_GATE_RANK)))
    proj = _matmul(x2, w_main, 512, n_main).reshape(bsz, t, n_main)
    a_low = _matmul(x2, w_low, 512, LANE).reshape(bsz, t, LANE)
    w2p = jnp.pad(_mx(w_gate2), ((0, LANE - GLA_GATE_RANK), (0, 0)))
    ts = min(GLA_TSTEP, t)
    out = pl.pallas_call(
        _gla_kernel,
        grid=(t // ts,),
        in_specs=[pl.BlockSpec((bsz, ts, GLA_KEY_DIM), lambda s: (0, s, 0)),
                  pl.BlockSpec((bsz, ts, GLA_KEY_DIM), lambda s: (0, s, 1)),
                  pl.BlockSpec((bsz, ts, GLA_VAL_DIM), lambda s: (0, s, 1)),
                  pl.BlockSpec((bsz, ts, GLA_VAL_DIM), lambda s: (0, s, 2)),
                  pl.BlockSpec((bsz, ts, LANE), lambda s: (0, s, 0)),
                  pl.BlockSpec((LANE, GLA_KEY_DIM), lambda s: (0, 0)),
                  pl.BlockSpec((1, GLA_KEY_DIM), lambda s: (0, 0)),
                  pl.BlockSpec((1, GLA_VAL_DIM), lambda s: (0, 0))],
        out_specs=pl.BlockSpec((bsz, ts, GLA_VAL_DIM), lambda s: (0, s, 0)),
        out_shape=jax.ShapeDtypeStruct((bsz, t, GLA_VAL_DIM), F32),
        scratch_shapes=[pltpu.VMEM((bsz * GLA_HEADS, GLA_DV, GLA_DK), F32)],
        compiler_params=_params("arbitrary"),
        name="gla",
    )(proj, proj, proj, proj, a_low, w2p, b_gate2.reshape(1, -1), head_norm_g.reshape(1, -1))
    return out.reshape(bsz * t, GLA_VAL_DIM)


def _router_kernel(x_ref, wh_ref, wl_ref, o_ref):
    x = x_ref[...]
    xh = _mx(x)
    xl = _mx(x - xh.astype(F32))
    wh = wh_ref[...]
    logits = _dot(xh, wh) + _dot(xl, wh) + _dot(xh, wl_ref[...])
    lane = lax.broadcasted_iota(jnp.int32, logits.shape, 1)
    lg = jnp.where(lane < N_EXPERTS, logits, -jnp.inf)
    m1 = jnp.max(lg, axis=1, keepdims=True)
    i1 = jnp.min(jnp.where(lg == m1, lane, LANE), axis=1, keepdims=True)
    lg2 = jnp.where(lane == i1, -jnp.inf, lg)
    m2 = jnp.max(lg2, axis=1, keepdims=True)
    i2 = jnp.min(jnp.where(lg2 == m2, lane, LANE), axis=1, keepdims=True)
    e2 = jnp.exp(m2 - m1)
    den = 1.0 + e2
    w1 = 1.0 / den
    w2 = e2 / den
    out = jnp.where(lane == N_EXPERTS, i1.astype(F32), 0.0)
    out = jnp.where(lane == N_EXPERTS + 1, i2.astype(F32), out)
    out = jnp.where(lane == N_EXPERTS + 2, w1, out)
    out = jnp.where(lane == N_EXPERTS + 3, w2, out)
    o_ref[...] = out


def _moe_up_kernel(te_ref, x_ref, wg_ref, wu_ref, o_ref, wg_sc, wu_sc):
    i = pl.program_id(1)

    @pl.when((i == 0) | (te_ref[i] != te_ref[jnp.maximum(i - 1, 0)]))
    def _():
        wg_sc[...] = _mx(wg_ref[0])
        wu_sc[...] = _mx(wu_ref[0])

    _swiglu_cols(_mx(x_ref[...]), wg_sc, wu_sc, o_ref)


def _moe_down_kernel(te_ref, h_ref, w_ref, rw_ref, o_ref, w_sc):
    i = pl.program_id(0)

    @pl.when((i == 0) | (te_ref[i] != te_ref[jnp.maximum(i - 1, 0)]))
    def _():
        w_sc[...] = _mx(w_ref[0])

    o_ref[...] = _dot(h_ref[...], w_sc[...]) * rw_ref[...]


def _add_ln_kernel(x_ref, y0_ref, y1_ref, g_ref, b_ref, o_ref):
    h = DN_ALPHA * x_ref[...] + (y0_ref[...] + y1_ref[...])
    o_ref[...] = _layer_norm_rows(h, g_ref[...], b_ref[...])


def _moe_layer(x2, w_router, w_gu, w_down, ln_g, ln_b):
    m, d = x2.shape
    tm = MOE_TM
    wr = jnp.pad(w_router, ((0, 0), (0, LANE - N_EXPERTS)))
    wr_hi = _mx(wr)
    wr_lo = _mx(wr - wr_hi.astype(F32))
    rt = pl.pallas_call(
        _router_kernel,
        grid=(m // 512,),
        in_specs=[pl.BlockSpec((512, d), lambda i: (i, 0)),
                  pl.BlockSpec((d, LANE), lambda i: (0, 0)),
                  pl.BlockSpec((d, LANE), lambda i: (0, 0))],
        out_specs=pl.BlockSpec((512, LANE), lambda i: (i, 0)),
        out_shape=jax.ShapeDtypeStruct((m, LANE), F32),
        compiler_params=_params("parallel"),
        name="moe_router",
    )(x2, wr_hi, wr_lo)
    top_idx = rt[:, N_EXPERTS:N_EXPERTS + 2].astype(jnp.int32)
    top_w = rt[:, N_EXPERTS + 2:N_EXPERTS + 4]

    n_asg = m * TOP_K
    n_rows = n_asg + N_EXPERTS * tm
    n_tiles = n_rows // tm
    e_flat = top_idx.reshape(-1)
    order = jnp.argsort(e_flat, stable=True).astype(jnp.int32)
    slot = jnp.argsort(order).astype(jnp.int32)
    counts = jnp.sum((e_flat[:, None] == jnp.arange(N_EXPERTS)[None, :]).astype(jnp.int32), axis=0)
    tiles_per = (counts + tm - 1) // tm
    tile_end = jnp.cumsum(tiles_per)
    row_start = (tile_end - tiles_per) * tm
    grp_start = jnp.cumsum(counts) - counts
    tile_expert = jnp.minimum(jnp.sum((jnp.arange(n_tiles)[:, None] >= tile_end[None, :]).astype(jnp.int32),
                                      axis=1), N_EXPERTS - 1).astype(jnp.int32)
    pos = (row_start[e_flat] + slot - grp_start[e_flat]).astype(jnp.int32).reshape(m, TOP_K)
    row_e = jnp.repeat(tile_expert, tm)
    row_off = jnp.arange(n_rows, dtype=jnp.int32) - row_start[row_e].astype(jnp.int32)
    row_live = row_off < counts[row_e]
    row_asg = order[jnp.clip(grp_start[row_e].astype(jnp.int32) + row_off, 0, n_asg - 1)]
    row_token = jnp.where(row_live, row_asg // TOP_K, 0)
    row_w = jnp.where(row_live, top_w.reshape(-1)[row_asg], 0.0)

    xs = x2[row_token]
    f = w_gu.shape[2] // 2
    tf = MOE_TF
    nf = f // tf
    h = pl.pallas_call(
        _moe_up_kernel,
        grid_spec=pltpu.PrefetchScalarGridSpec(
            num_scalar_prefetch=1,
            grid=(nf, n_tiles),
            in_specs=[pl.BlockSpec((tm, d), lambda j, i, te: (i, 0)),
                      pl.BlockSpec((1, d, tf), lambda j, i, te: (te[i], 0, j)),
                      pl.BlockSpec((1, d, tf), lambda j, i, te: (te[i], 0, j + nf))],
            out_specs=pl.BlockSpec((tm, tf), lambda j, i, te: (i, j)),
            scratch_shapes=[pltpu.VMEM((d, tf), MXU_DTYPE), pltpu.VMEM((d, tf), MXU_DTYPE)],
        ),
        out_shape=jax.ShapeDtypeStruct((n_rows, f), MXU_DTYPE),
        compiler_params=_params("arbitrary", "arbitrary"),
        name="moe_up",
    )(tile_expert, xs, w_gu, w_gu)
    ys = pl.pallas_call(
        _moe_down_kernel,
        grid_spec=pltpu.PrefetchScalarGridSpec(
            num_scalar_prefetch=1,
            grid=(n_tiles,),
            in_specs=[pl.BlockSpec((tm, f), lambda i, te: (i, 0)),
                      pl.BlockSpec((1, f, d), lambda i, te: (te[i], 0, 0)),
                      pl.BlockSpec((tm, 1), lambda i, te: (i, 0))],
            out_specs=pl.BlockSpec((tm, d), lambda i, te: (i, 0)),
            scratch_shapes=[pltpu.VMEM((f, d), MXU_DTYPE)],
        ),
        out_shape=jax.ShapeDtypeStruct((n_rows, d), F32),
        compiler_params=_params("arbitrary"),
        name="moe_down",
    )(tile_expert, h, w_down, row_w.reshape(n_rows, 1))

    lo_first = top_idx[:, 0:1] < top_idx[:, 1:2]
    p0 = jnp.where(lo_first[:, 0], pos[:, 0], pos[:, 1])
    p1 = jnp.where(lo_first[:, 0], pos[:, 1], pos[:, 0])
    y0 = ys[p0]
    y1 = ys[p1]
    return pl.pallas_call(
        _add_ln_kernel,
        grid=(m // 512,),
        in_specs=[pl.BlockSpec((512, d), lambda i: (i, 0))] * 3
        + [pl.BlockSpec((1, d), lambda i: (0, 0))] * 2,
        out_specs=pl.BlockSpec((512, d), lambda i: (i, 0)),
        out_shape=jax.ShapeDtypeStruct((m, d), F32),
        compiler_params=_params("parallel"),
        name="moe_add_ln",
    )(x2, y0, y1, ln_g.reshape(1, d), ln_b.reshape(1, d))


def kernel(x, l0_w_in, l0_cmp_pe_k, l0_cmp_pe_v, l0_cmp_wk1, l0_cmp_wk2, l0_cmp_wv1, l0_cmp_wv2, l0_w_o, l0_ln1_g, l0_ln1_b, l0_ffn_w_gu, l0_ffn_w_down, l0_ln2_g, l0_ln2_b, l1_w_in, l1_w_gate2, l1_b_gate2, l1_head_norm_g, l1_w_o, l1_ln1_g, l1_ln1_b, l1_router, l1_moe_w_gu, l1_moe_w_down, l1_ln2_g, l1_ln2_b):
    bsz, t, d = x.shape
    x2 = x.reshape(bsz * t, d)

    o = _nsa_mixer(x2, bsz, t, l0_w_in, l0_cmp_pe_k, l0_cmp_pe_v, l0_cmp_wk1, l0_cmp_wk2,
                   l0_cmp_wv1, l0_cmp_wv2)
    x2 = _matmul_res_ln(o, _mx(l0_w_o), x2, l0_ln1_g, l0_ln1_b, 512)
    hmid = _swiglu_up(x2, _mx(l0_ffn_w_gu), 512, FFN_DENSE)
    x2 = _matmul_res_ln(hmid, _mx(l0_ffn_w_down), x2, l0_ln2_g, l0_ln2_b, 512)

    o = _gla_mixer(x2, bsz, t, l1_w_in, l1_w_gate2, l1_b_gate2, l1_head_norm_g)
    x2 = _matmul_res_ln(o, _mx(l1_w_o), x2, l1_ln1_g, l1_ln1_b, 512)
    x2 = _moe_layer(x2, l1_router, l1_moe_w_gu, l1_moe_w_down, l1_ln2_g, l1_ln2_b)
    return x2.reshape(bsz, t, d)
```

```python
import functools

import numpy as np
import jax
import jax.numpy as jnp
from jax import lax
from jax.experimental import pallas as pl
from jax.experimental.pallas import tpu as pltpu

F32 = jnp.float32
MXU_DTYPE = jnp.bfloat16

D_MODEL = 1024
DEPTH = 2
DN_ALPHA = (2 * DEPTH) ** 0.25
LN_EPS = 1e-5
RMS_EPS = 1e-6

NSA_HEADS = 16
NSA_HEAD_DIM = 64
NSA_KV_GROUPS = 4
NSA_HPG = NSA_HEADS // NSA_KV_GROUPS
NSA_KV_DIM = NSA_KV_GROUPS * NSA_HEAD_DIM
CMP_LEN = 32
CMP_STRIDE = 16
CMP_HIDDEN = 256
SLC_LEN = 64
N_SEL = 16
WINDOW = 512
QTILE = 128
KBLK = 128
NCOL = NSA_HPG * QTILE
CMP_CHUNK = 128
SEL_UNROLL = 6
VT_ROWS = 80
LOG2E = 1.4426950408889634

GLA_HEADS = 4
GLA_KEY_DIM = D_MODEL // 2
GLA_VAL_DIM = D_MODEL
GLA_DK = GLA_KEY_DIM // GLA_HEADS
GLA_DV = GLA_VAL_DIM // GLA_HEADS
GLA_GATE_RANK = 16
GLA_GATE_NORM = 16.0
GLA_CHUNK = 64
GLA_TSTEP = 256

FFN_DENSE = 2816
N_EXPERTS = 8
TOP_K = 2
FFN_EXPERT = 3584
MOE_TM = 512
MOE_TF = 1792
SWIGLU_CHUNK = 256
MM_CHUNK = 512

LANE = 128
NEG = -1e30
VMEM_LIMIT = 56 * 1024 * 1024


def _params(*sem):
    return pltpu.CompilerParams(dimension_semantics=sem, vmem_limit_bytes=VMEM_LIMIT)


def _mx(a):
    return a.astype(MXU_DTYPE)


def _dot(a, b):
    return jnp.dot(a, b, preferred_element_type=F32)


def _dot_nt(a, b):
    return lax.dot_general(a, b, (((1,), (1,)), ((), ())), preferred_element_type=F32)


def _col_chunks(n):
    step = MM_CHUNK if n % MM_CHUNK == 0 else n
    return [slice(j, j + step) for j in range(0, n, step)]


def _mm_kernel(x_ref, w_ref, o_ref):
    x = _mx(x_ref[...])
    for cs in _col_chunks(o_ref.shape[1]):
        o_ref[:, cs] = _dot(x, w_ref[:, cs]).astype(o_ref.dtype)


def _matmul(x, w, tm, tn, out_dtype=F32):
    m, k = x.shape
    n = w.shape[1]
    return pl.pallas_call(
        _mm_kernel,
        grid=(m // tm, n // tn),
        in_specs=[pl.BlockSpec((tm, k), lambda i, j: (i, 0)),
                  pl.BlockSpec((k, tn), lambda i, j: (0, j))],
        out_specs=pl.BlockSpec((tm, tn), lambda i, j: (i, j)),
        out_shape=jax.ShapeDtypeStruct((m, n), out_dtype),
        compiler_params=_params("parallel", "parallel"),
        name="matmul",
    )(x, w)


def _mm_scaled_kernel(x_ref, w_ref, s_ref, o_ref):
    x = _mx(x_ref[...])
    for cs in _col_chunks(o_ref.shape[1]):
        o_ref[:, cs] = (_dot(x, w_ref[:, cs]) * s_ref[:, cs]).astype(o_ref.dtype)


def _matmul_scaled(x, w, col_scale, tm, tn, out_dtype):
    m, k = x.shape
    n = w.shape[1]
    return pl.pallas_call(
        _mm_scaled_kernel,
        grid=(m // tm, n // tn),
        in_specs=[pl.BlockSpec((tm, k), lambda i, j: (i, 0)),
                  pl.BlockSpec((k, tn), lambda i, j: (0, j)),
                  pl.BlockSpec((1, tn), lambda i, j: (0, j))],
        out_specs=pl.BlockSpec((tm, tn), lambda i, j: (i, j)),
        out_shape=jax.ShapeDtypeStruct((m, n), out_dtype),
        compiler_params=_params("parallel", "parallel"),
        name="matmul_scaled",
    )(x, w, col_scale)


def _layer_norm_rows(h, g, b):
    mu = jnp.mean(h, -1, keepdims=True)
    d = h - mu
    var = jnp.mean(d * d, -1, keepdims=True)
    return d * lax.rsqrt(var + LN_EPS) * g + b


def _mm_ln_kernel(x_ref, w_ref, r_ref, g_ref, b_ref, o_ref):
    y = _dot(_mx(x_ref[...]), w_ref[...])
    o_ref[...] = _layer_norm_rows(DN_ALPHA * r_ref[...] + y, g_ref[...], b_ref[...])


def _matmul_res_ln(x, w, res, g, b, tm):
    m, k = x.shape
    n = w.shape[1]
    return pl.pallas_call(
        _mm_ln_kernel,
        grid=(m // tm,),
        in_specs=[pl.BlockSpec((tm, k), lambda i: (i, 0)),
                  pl.BlockSpec((k, n), lambda i: (0, 0)),
                  pl.BlockSpec((tm, n), lambda i: (i, 0)),
                  pl.BlockSpec((1, n), lambda i: (0, 0)),
                  pl.BlockSpec((1, n), lambda i: (0, 0))],
        out_specs=pl.BlockSpec((tm, n), lambda i: (i, 0)),
        out_shape=jax.ShapeDtypeStruct((m, n), F32),
        compiler_params=_params("parallel"),
        name="matmul_res_ln",
    )(x, w, res, g.reshape(1, n), b.reshape(1, n))


def _swiglu_cols(x, wg_ref, wu_ref, o_ref):
    n = o_ref.shape[1]
    step = SWIGLU_CHUNK if n % SWIGLU_CHUNK == 0 else LANE
    for j in range(0, n, step):
        a = _dot(x, wg_ref[:, j:j + step])
        b = _dot(x, wu_ref[:, j:j + step])
        o_ref[:, j:j + step] = (a * jax.nn.sigmoid(a) * b).astype(o_ref.dtype)


def _swiglu_up_kernel(x_ref, wg_ref, wu_ref, o_ref):
    _swiglu_cols(_mx(x_ref[...]), wg_ref, wu_ref, o_ref)


def _swiglu_up(x, w_gu, tm, tn):
    m, k = x.shape
    f = w_gu.shape[1] // 2
    nj = f // tn
    return pl.pallas_call(
        _swiglu_up_kernel,
        grid=(m // tm, nj),
        in_specs=[pl.BlockSpec((tm, k), lambda i, j: (i, 0)),
                  pl.BlockSpec((k, tn), lambda i, j: (0, j)),
                  pl.BlockSpec((k, tn), lambda i, j: (0, j + nj))],
        out_specs=pl.BlockSpec((tm, tn), lambda i, j: (i, j)),
        out_shape=jax.ShapeDtypeStruct((m, f), MXU_DTYPE),
        compiler_params=_params("parallel", "parallel"),
        name="swiglu_up",
    )(x, w_gu, w_gu)


def _cmp_up_kernel(c_ref, pea_ref, peb_ref, wa_ref, wb_ref, p_ref, q_ref):
    c = c_ref[...]
    p_ref[...] = _dot(_mx(c + pea_ref[...]), wa_ref[...])
    q_ref[...] = _dot(_mx(c + peb_ref[...]), wb_ref[...])


def _cmp_down_kernel(p_ref, q_ref, w_ref, o_ref):
    h = jax.nn.gelu(p_ref[...] + q_ref[...])
    o_ref[...] = _dot(_mx(h), w_ref[...])


def _compress(a, pe, w1, w2):
    bsz, t, g, dk = a.shape
    nch = t // CMP_STRIDE
    half = CMP_STRIDE * dk
    chunks = a.reshape(bsz, nch, CMP_STRIDE, g, dk).transpose(0, 1, 3, 2, 4).reshape(bsz * nch * g, half)
    rows = chunks.shape[0]
    tm = min(512, rows)
    pe_a = pe[:CMP_STRIDE].reshape(1, half)
    pe_b = pe[CMP_STRIDE:].reshape(1, half)
    w1m = _mx(w1)
    p, q = pl.pallas_call(
        _cmp_up_kernel,
        grid=(rows // tm,),
        in_specs=[pl.BlockSpec((tm, half), lambda i: (i, 0)),
                  pl.BlockSpec((1, half), lambda i: (0, 0)),
                  pl.BlockSpec((1, half), lambda i: (0, 0)),
                  pl.BlockSpec((half, CMP_HIDDEN), lambda i: (0, 0)),
                  pl.BlockSpec((half, CMP_HIDDEN), lambda i: (0, 0))],
        out_specs=[pl.BlockSpec((tm, CMP_HIDDEN), lambda i: (i, 0)),
                   pl.BlockSpec((tm, CMP_HIDDEN), lambda i: (i, 0))],
        out_shape=[jax.ShapeDtypeStruct((rows, CMP_HIDDEN), F32)] * 2,
        compiler_params=_params("parallel"),
        name="cmp_up",
    )(chunks, pe_a, pe_b, w1m[:half], w1m[half:])
    q = q.reshape(bsz, nch, g, CMP_HIDDEN)
    q = jnp.concatenate([q[:, 1:], jnp.zeros_like(q[:, :1])], axis=1).reshape(rows, CMP_HIDDEN)
    w2p = jnp.pad(_mx(w2), ((0, 0), (0, LANE - dk)))
    out = pl.pallas_call(
        _cmp_down_kernel,
        grid=(rows // tm,),
        in_specs=[pl.BlockSpec((tm, CMP_HIDDEN), lambda i: (i, 0)),
                  pl.BlockSpec((tm, CMP_HIDDEN), lambda i: (i, 0)),
                  pl.BlockSpec((CMP_HIDDEN, LANE), lambda i: (0, 0))],
        out_specs=pl.BlockSpec((tm, LANE), lambda i: (i, 0)),
        out_shape=jax.ShapeDtypeStruct((rows, LANE), F32),
        compiler_params=_params("parallel"),
        name="cmp_down",
    )(p, q, w2p)
    out = out[:, :dk].reshape(bsz, nch, g, dk)
    valid = (jnp.arange(nch) < nch - 1)[None, :, None, None]
    return jnp.where(valid, out, 0.0)


def _queries_t(q_blk):
    qt = q_blk.astype(F32).T
    dk = NSA_HEAD_DIM
    return _mx(jnp.concatenate([qt[h * dk:(h + 1) * dk, :] for h in range(NSA_HPG)], axis=1))


def _nsa_cmp_kernel(q_ref, qf_ref, kc_ref, vct_ref, ovl_ref, ocmp_ref, sel_ref, flag_ref,
                    imp_ref, key_ref, cnt_ref):
    c = pl.program_id(2)
    ncp = kc_ref.shape[2]
    ns = ovl_ref.shape[0]
    dk = NSA_HEAD_DIM
    rhs = jnp.concatenate([_queries_t(q_ref[...]), qf_ref[0]], axis=0)
    chunk = min(CMP_CHUNK, ncp)
    n_chunks = (c * (QTILE // CMP_STRIDE) + (QTILE - CMP_LEN) // CMP_STRIDE + chunk) // chunk

    def cmp_branch(rows):
        n_i = lax.broadcasted_iota(jnp.int32, (rows, QTILE), 0)
        tq = c * QTILE + lax.broadcasted_iota(jnp.int32, (rows, QTILE), 1)
        bias = _mx(jnp.where(n_i * CMP_STRIDE + (CMP_LEN - 1) <= tq, 0.0, NEG))
        s = _dot(jnp.concatenate([kc_ref[0, 0, :rows, :], bias], axis=1), rhs)
        e = jnp.exp2(s - jnp.max(s, axis=0, keepdims=True))
        seen = jnp.concatenate([tq[0:1, :]] * NSA_HPG, axis=1) >= CMP_LEN - 1
        r = jnp.where(seen, 1.0 / jnp.maximum(jnp.sum(e, axis=0, keepdims=True), 1e-30), 0.0)
        ocmp_ref[0, 0, 0] = _dot(vct_ref[0, 0, :, :rows], _mx(e)) * r
        psum = e[:, 0:QTILE] * r[:, 0:QTILE]
        for h in range(1, NSA_HPG):
            psum = psum + e[:, h * QTILE:(h + 1) * QTILE] * r[:, h * QTILE:(h + 1) * QTILE]
        hi = _mx(psum)
        lo = _mx(psum - hi.astype(F32))
        ovl = ovl_ref[:, :rows]
        imp_ref[...] = _dot(ovl, hi) + _dot(ovl, lo)

    for k in range(1, ncp // chunk + 1):
        pl.when(n_chunks == k)(functools.partial(cmp_branch, k * chunk))

    j_i = lax.broadcasted_iota(jnp.int32, (ns, QTILE), 0)
    tq2 = c * QTILE + lax.broadcasted_iota(jnp.int32, (ns, QTILE), 1)
    valid = j_i * SLC_LEN <= tq2
    cur = tq2 >> (SLC_LEN.bit_length() - 1)
    forced = valid & ((j_i == 0) | (j_i == cur) | (j_i == cur - 1))
    score = jnp.where(forced, 1e9, jnp.where(valid, imp_ref[...], -1e9))
    bits = lax.bitcast_convert_type(score, jnp.int32)
    key_ref[...] = jnp.where(bits < 0, bits ^ 0x7FFFFFFF, bits)
    cnt_ref[...] = jnp.zeros(cnt_ref.shape, jnp.int32)

    sub = lax.broadcasted_iota(jnp.int32, (8, QTILE), 0)
    n_it = (c * (QTILE // SLC_LEN) + QTILE // SLC_LEN + 7) // 8

    def rank_blocks(n_jt):
        def it_body(it, carry):
            base = pl.multiple_of(it * 8, 8)
            rows8 = key_ref[pl.ds(base, 8), :]
            rb = [jnp.broadcast_to(rows8[il:il + 1, :], (8, QTILE)) for il in range(8)]
            for jt in range(n_jt):
                js = slice(jt * 8, (jt + 1) * 8)
                thr = key_ref[js, :] - jnp.where(jt > it, 1, 0)
                acc = cnt_ref[js, :]
                for il in range(8):
                    acc = acc + jnp.where(rb[il] > thr, 1, 0)
                cnt_ref[js, :] = acc
            corr = jnp.zeros((8, QTILE), jnp.int32)
            for il in range(8):
                corr = corr + jnp.where((rb[il] == rows8) & (sub > il), 1, 0)
            cnt_ref[pl.ds(base, 8), :] += corr
            return carry

        lax.fori_loop(0, n_it, it_body, 0)

    sizes = sorted({min(ns // 8, max(1, ns // 32) * k) for k in range(1, 5)})
    for lo, hi in zip([0] + sizes[:-1], sizes):
        pl.when((n_it > lo) & (n_it <= hi))(functools.partial(rank_blocks, hi))
    sel = jnp.where((cnt_ref[...] < min(N_SEL, ns)) & valid, 1.0, 0.0)
    sel_ref[0, 0, 0] = sel
    flag_ref[0, 0, 0] = _dot_nt(jnp.ones((8, QTILE), MXU_DTYPE), _mx(sel))


def _nsa_cmp_select(proj, q_feat, kcmp, vcmp_t, ovl_t):
    bsz, g, ncp, _ = kcmp.shape
    dk = NSA_HEAD_DIM
    nt = proj.shape[0] // (bsz * QTILE)
    ns = ovl_t.shape[0]
    return pl.pallas_call(
        _nsa_cmp_kernel,
        grid=(bsz, g, nt),
        in_specs=[pl.BlockSpec((QTILE, NSA_HPG * dk), lambda b, gg, c: (b * nt + c, gg)),
                  pl.BlockSpec((1, dk + QTILE, NCOL), lambda b, gg, c: (gg, 0, 0)),
                  pl.BlockSpec((1, 1, ncp, LANE), lambda b, gg, c: (b, gg, 0, 0)),
                  pl.BlockSpec((1, 1, dk, ncp), lambda b, gg, c: (b, gg, 0, 0)),
                  pl.BlockSpec((ns, ncp), lambda b, gg, c: (0, 0))],
        out_specs=[pl.BlockSpec((1, 1, 1, dk, NCOL), lambda b, gg, c: (b, gg, c, 0, 0)),
                   pl.BlockSpec((1, 1, 1, ns, QTILE), lambda b, gg, c: (b, gg, c, 0, 0)),
                   pl.BlockSpec((1, 1, 1, 8, ns), lambda b, gg, c: (b, gg, c, 0, 0))],
        out_shape=[jax.ShapeDtypeStruct((bsz, g, nt, dk, NCOL), F32),
                   jax.ShapeDtypeStruct((bsz, g, nt, ns, QTILE), F32),
                   jax.ShapeDtypeStruct((bsz, g, nt, 8, ns), F32)],
        scratch_shapes=[pltpu.VMEM((ns, QTILE), F32), pltpu.VMEM((ns, QTILE), jnp.int32),
                        pltpu.VMEM((ns, QTILE), jnp.int32)],
        compiler_params=_params("parallel", "parallel", "parallel"),
        name="nsa_cmp_select",
    )(proj, q_feat, kcmp, vcmp_t, ovl_t)


def _list_stride(npair):
    trip = 2 * SEL_UNROLL
    return -(-npair // trip) * trip + SEL_UNROLL


def _values_t(v_blk):
    vt = v_blk.astype(F32).T[:VT_ROWS, :]
    ones_row = lax.broadcasted_iota(jnp.int32, vt.shape, 0) == NSA_HEAD_DIM
    return _mx(jnp.where(ones_row, 1.0, vt))


def _nsa_attn_kernel(lst_ref, cnt_ref, q_ref, qf_ref, ks_ref, vs_ref, kw_ref, vw_ref, pos_ref, sel_ref,
                     ocmp_ref, gl_ref, o_ref, ksa_sc, kwa_sc, vst_sc, vwt_sc, gate_sc, sa_sc, ow_sc, m_sc, acc_sc):
    b = pl.program_id(0)
    g = pl.program_id(1)
    c = pl.program_id(2)
    nt = pl.num_programs(2)
    npair = ks_ref.shape[1] // KBLK
    dk = NSA_HEAD_DIM

    @pl.when(c == 0)
    def _():
        def prep(j, carry):
            rows = pl.ds(pl.multiple_of(j * KBLK, KBLK), KBLK)
            feats = pos_ref[rows, :]
            ksa_sc[rows, :] = ks_ref[0, rows, :] + feats
            kwa_sc[rows, :] = kw_ref[0, rows, :] + feats
            vst_sc[j] = _values_t(vs_ref[0, rows, :])
            vwt_sc[j] = _values_t(vw_ref[0, rows, :])
            return carry

        lax.fori_loop(0, npair, prep, 0)

    rhs = jnp.concatenate([_queries_t(q_ref[...]), qf_ref[0]], axis=0)
    key_i = lax.broadcasted_iota(jnp.int32, (KBLK, QTILE), 0)
    q_i = lax.broadcasted_iota(jnp.int32, (KBLK, QTILE), 1)
    heads = [slice(h * QTILE, (h + 1) * QTILE) for h in range(NSA_HPG)]

    m_sc[...] = jnp.full(m_sc.shape, NEG, F32)
    acc_sc[...] = jnp.zeros(acc_sc.shape, F32)
    step = (b * NSA_KV_GROUPS + g) * nt + c

    def group_blocks(gi):
        return [lst_ref[step * _list_stride(npair) + gi * SEL_UNROLL + u] for u in range(SEL_UNROLL)]

    def group_scores(gi, own_first):
        scores = []
        for u, jj in enumerate(group_blocks(gi)):
            js = jnp.maximum(jj, 0)
            rows = sel_ref[0, 0, 0, pl.ds(2 * js, 2), :]
            if own_first and u == 0:
                picked = jnp.where(key_i < SLC_LEN, rows[0:1, :], rows[1:2, :]) > 0.5
                bias = _mx(jnp.where(picked & (key_i <= q_i), 0.0, NEG))
            else:
                half = _mx(jnp.where(jj >= 0, (rows - 1.0) * -NEG, NEG))
                bias = jnp.concatenate([jnp.broadcast_to(half[0:1, :], (SLC_LEN, QTILE)),
                                        jnp.broadcast_to(half[1:2, :], (SLC_LEN, QTILE))], axis=0)
            keys = ksa_sc[pl.ds(pl.multiple_of(js * KBLK, KBLK), KBLK), :]
            scores.append(_dot(jnp.concatenate([keys, bias], axis=1), rhs))
        return scores

    def softmax_update(gi, scores):
        blocks = [jnp.maximum(jj, 0) for jj in group_blocks(gi)]
        m_new = []
        for cs in heads:
            m_old = m_sc[:, cs]
            mx_ = m_old
            for u in range(SEL_UNROLL):
                mx_ = jnp.maximum(mx_, jnp.max(scores[u][:, cs], axis=0, keepdims=True))
            acc_sc[:, cs] = jnp.exp2(m_old - mx_) * acc_sc[:, cs]
            m_sc[:, cs] = mx_
            m_new.append(mx_)
        upd = None
        for u in range(SEL_UNROLL):
            e = jnp.concatenate([_mx(jnp.exp2(scores[u][:, cs] - m_new[hi])) for hi, cs in enumerate(heads)],
                                axis=1)
            d = _dot(vst_sc[blocks[u]], e)
            upd = d if upd is None else upd + d
        acc_sc[...] += upd

    for u, s in enumerate(group_scores(0, True)):
        sa_sc[u] = s

    nwin = WINDOW // KBLK + 1
    blk0 = jnp.maximum(c - (nwin - 1), 0)
    row = lax.broadcasted_iota(jnp.int32, (nwin * KBLK, QTILE), 0)
    dist = lax.broadcasted_iota(jnp.int32, (nwin * KBLK, QTILE), 1) + (c - blk0) * KBLK - row
    bias = _mx(jnp.where((dist >= 0) & (dist < WINDOW), 0.0, NEG))
    keys = kwa_sc[pl.ds(pl.multiple_of(blk0 * KBLK, KBLK), nwin * KBLK), :]
    s = _dot(jnp.concatenate([keys, bias], axis=1), rhs)
    e_all = jnp.concatenate([_mx(jnp.exp2(s[:, cs] - jnp.max(s[:, cs], axis=0, keepdims=True)))
                             for cs in heads], axis=1)
    acc = None
    for t in range(nwin):
        d = _dot(vwt_sc[blk0 + t], e_all[t * KBLK:(t + 1) * KBLK, :])
        acc = d if acc is None else acc + d
    ow_sc[...] = acc[:dk, :] * (1.0 / jnp.maximum(acc[dk:dk + 1, :], 1e-30))

    def trip(r, carry):
        scores_b = group_scores(2 * r + 1, False)
        softmax_update(2 * r, [sa_sc[u] for u in range(SEL_UNROLL)])
        for u, s in enumerate(group_scores(2 * r + 2, False)):
            sa_sc[u] = s
        softmax_update(2 * r + 1, scores_b)
        return carry

    trip(0, 0)
    lax.fori_loop(1, (cnt_ref[step] + 2 * SEL_UNROLL - 1) // (2 * SEL_UNROLL), trip, 0)
    o_sel = acc_sc[:dk, :] * (1.0 / jnp.maximum(acc_sc[dk:dk + 1, :], 1e-30))
    o_win = ow_sc[...]

    gate_sc[...] = gl_ref[...].T
    ocmp = ocmp_ref[0, 0, 0]
    outs = []
    for h, cs in enumerate(heads):
        gt = jax.nn.sigmoid(gate_sc[pl.ds((g * NSA_HPG + h) * 3, 3), :])
        outs.append(gt[0:1] * ocmp[:, cs] + gt[1:2] * o_sel[:, cs] + gt[2:3] * o_win[:, cs])
    o_ref[...] = jnp.concatenate(outs, axis=0).T.astype(o_ref.dtype)


def _nsa_attend(blk_list, blk_count, proj, proj_f, q_feat, pos_feat, sel_t, ocmp_t, bsz):
    g, dk = NSA_KV_GROUPS, NSA_HEAD_DIM
    t = proj.shape[0] // bsz
    nt = t // QTILE
    npair = t // KBLK
    ns = sel_t.shape[3]
    proj3 = proj.reshape(bsz, t, proj.shape[1])
    seg = D_MODEL // LANE
    gate_blk = proj_f.shape[1] // LANE - 1
    tile = lambda b, gg, c, lst, cnt: (b, gg, c, 0, 0)
    rows = lambda b, gg, c, lst, cnt: (b * nt + c, gg)
    kv = lambda k: (lambda b, gg, c, lst, cnt: (b, 0, seg + k * g + gg))
    grid_spec = pltpu.PrefetchScalarGridSpec(
        num_scalar_prefetch=2,
        grid=(bsz, g, nt),
        in_specs=[pl.BlockSpec((QTILE, NSA_HPG * dk), rows),
                  pl.BlockSpec((1, dk + QTILE, NCOL), lambda b, gg, c, lst, cnt: (gg, 0, 0)),
                  pl.BlockSpec((1, t, LANE), kv(0)),
                  pl.BlockSpec((1, t, LANE), kv(1)),
                  pl.BlockSpec((1, t, LANE), kv(2)),
                  pl.BlockSpec((1, t, LANE), kv(3)),
                  pl.BlockSpec((t, LANE), lambda b, gg, c, lst, cnt: (0, 0)),
                  pl.BlockSpec((1, 1, 1, ns, QTILE), tile),
                  pl.BlockSpec((1, 1, 1, dk, NCOL), tile),
                  pl.BlockSpec((QTILE, LANE), lambda b, gg, c, lst, cnt: (b * nt + c, gate_blk))],
        out_specs=pl.BlockSpec((QTILE, NSA_HPG * dk), rows),
        scratch_shapes=[pltpu.VMEM((t, LANE), MXU_DTYPE), pltpu.VMEM((t, LANE), MXU_DTYPE),
                        pltpu.VMEM((npair, VT_ROWS, KBLK), MXU_DTYPE),
                        pltpu.VMEM((npair, VT_ROWS, KBLK), MXU_DTYPE),
                        pltpu.VMEM((LANE, QTILE), F32),
                        pltpu.VMEM((SEL_UNROLL, KBLK, NCOL), F32),
                        pltpu.VMEM((dk, NCOL), F32),
                        pltpu.VMEM((1, NCOL), F32), pltpu.VMEM((VT_ROWS, NCOL), F32)],
    )
    return pl.pallas_call(
        _nsa_attn_kernel,
        grid_spec=grid_spec,
        out_shape=jax.ShapeDtypeStruct((bsz * t, NSA_HEADS * dk), MXU_DTYPE),
        compiler_params=_params("parallel", "parallel", "arbitrary"),
        name="nsa_attend",
    )(blk_list, blk_count, proj, q_feat, proj3, proj3, proj3, proj3, pos_feat, sel_t, ocmp_t, proj_f)


def _nsa_mixer(x2, bsz, t, w_in, pe_k, pe_v, wk1, wk2, wv1, wv2):
    g, hpg, dk = NSA_KV_GROUPS, NSA_HPG, NSA_HEAD_DIM
    nt = t // QTILE
    d = w_in.shape[0]
    splits = np.cumsum([D_MODEL] + [NSA_KV_DIM] * 6).tolist()
    wq, wkc, wvc, wks, wvs, wkw, wvw, wgl = jnp.split(w_in, splits, axis=1)

    def lane_groups(w):
        return jnp.pad(w.reshape(d, g, dk), ((0, 0), (0, 0), (0, LANE - dk))).reshape(d, g * LANE)

    w_a = _mx(jnp.concatenate([wq] + [lane_groups(w) for w in (wks, wvs, wkw, wvw)], axis=1))
    scale = jnp.concatenate([jnp.full((1, D_MODEL), dk ** -0.5 * LOG2E, F32),
                             jnp.ones((1, w_a.shape[1] - D_MODEL), F32)], axis=1)
    w_b = _mx(jnp.concatenate([wkc, wvc, jnp.pad(wgl, ((0, 0), (0, LANE - wgl.shape[1])))], axis=1))
    proj = _matmul_scaled(x2, w_a, scale, 512, w_a.shape[1], MXU_DTYPE)
    proj_f = _matmul(x2, w_b, 512, w_b.shape[1])
    kc = proj_f[:, :NSA_KV_DIM]
    vc = proj_f[:, NSA_KV_DIM:2 * NSA_KV_DIM]

    slopes = 2.0 ** (-8.0 * (jnp.arange(NSA_HEADS, dtype=F32) + 1.0) / NSA_HEADS) * LOG2E
    slopes = jnp.broadcast_to(slopes.reshape(g, 1, hpg, 1), (g, 1, hpg, QTILE)).reshape(g, 1, NCOL)
    s1 = slopes.astype(jnp.bfloat16)
    s2 = (slopes - s1.astype(F32)).astype(jnp.bfloat16)
    s3 = (slopes - s1.astype(F32) - s2.astype(F32)).astype(jnp.bfloat16)
    tail = np.zeros((g, dk - 6 + QTILE, NCOL), np.float32)
    tail[:, dk - 6:, :] = np.tile(np.eye(QTILE, dtype=np.float32), (1, hpg))
    q_feat = jnp.concatenate([s1 * SLC_LEN, s2 * SLC_LEN, s3 * SLC_LEN, s1, s2, s3, jnp.asarray(tail, jnp.bfloat16)],
                             axis=1).astype(MXU_DTYPE)

    def pos_features(kpos):
        feats = np.zeros((kpos.shape[0], LANE - dk), np.float32)
        feats[:, 0:3] = (kpos // SLC_LEN)[:, None]
        feats[:, 3:6] = (kpos % SLC_LEN)[:, None]
        return feats

    pos_feat = jnp.asarray(np.pad(pos_features(np.arange(t)), ((0, 0), (dk, 0))), MXU_DTYPE)

    ncp = t // CMP_STRIDE
    four = lambda a: a.reshape(bsz, t, g, dk)
    k_cmp = _compress(four(kc), pe_k, wk1, wk2)
    v_cmp = _compress(four(vc), pe_v, wv1, wv2)
    cmp_feat = jnp.asarray(pos_features(np.arange(ncp) * CMP_STRIDE + (CMP_LEN - 1)), MXU_DTYPE)
    kcmp = jnp.concatenate([_mx(k_cmp).transpose(0, 2, 1, 3),
                            jnp.broadcast_to(cmp_feat[None, None], (bsz, g, ncp, LANE - dk))], axis=-1)
    vcmp_t = _mx(v_cmp).transpose(0, 2, 3, 1)

    ns = t // SLC_LEN
    cs = np.arange(ncp) * CMP_STRIDE
    ce = cs + CMP_LEN - 1
    bs = np.arange(ns) * SLC_LEN
    be = bs + SLC_LEN - 1
    ovl_t = ((cs[None, :] <= be[:, None]) & (ce[None, :] >= bs[:, None])).astype(np.float32)
    ovl_t[:, ncp - 1] = 0.0
    ovl_t = jnp.asarray(ovl_t, MXU_DTYPE)

    ocmp_t, sel_t, counts = _nsa_cmp_select(proj, q_feat, kcmp, vcmp_t, ovl_t)
    npair = t // KBLK
    picked = counts[:, :, :, 0, :] > 0.5
    jj = jnp.arange(npair)
    tile_i = jnp.arange(nt)[:, None]
    active = (picked[..., 0::2] | picked[..., 1::2]) & (jj[None, :] < tile_i)
    blk_count = 1 + jnp.sum(active, axis=-1).astype(jnp.int32)
    slot = jnp.cumsum(active, axis=-1)
    hit = active[..., None, :] & (slot[..., None, :] == jj[:, None])
    blk_list = jnp.sum(jnp.where(hit, jj, 0), axis=-1)
    blk_list = jnp.where(jj == 0, tile_i, blk_list)
    blk_list = jnp.where(jj < blk_count[..., None], blk_list, -1).astype(jnp.int32)

    blk_list = jnp.pad(blk_list, ((0, 0), (0, 0), (0, 0), (0, _list_stride(npair) - npair)),
                       constant_values=-1).reshape(-1)
    return _nsa_attend(blk_list, blk_count.reshape(-1), proj, proj_f, q_feat, pos_feat,
                       sel_t, ocmp_t, bsz)


def _log_sigmoid(z):
    return jnp.minimum(z, 0.0) - jnp.log1p(jnp.exp(-jnp.abs(z)))


def _gla_kernel(q_ref, k_ref, v_ref, g_ref, al_ref, w2_ref, b2_ref, hn_ref, o_ref, st_sc):
    @pl.when(pl.program_id(0) == 0)
    def _():
        st_sc[...] = jnp.zeros(st_sc.shape, F32)

    cc = GLA_CHUNK
    r_i = lax.broadcasted_iota(jnp.int32, (cc, cc), 0)
    c_i = lax.broadcasted_iota(jnp.int32, (cc, cc), 1)
    causal = r_i >= c_i
    tri = jnp.where(causal, 1.0, 0.0).astype(MXU_DTYPE)
    ref = cc // 2

    def chunk(ci, carry):
        rows = pl.ds(pl.multiple_of(ci * cc, cc), cc)
        for b in range(q_ref.shape[0]):
            z = _dot(_mx(al_ref[b, rows, :]), w2_ref[...]) + b2_ref[...]
            log_a = _log_sigmoid(z) / GLA_GATE_NORM
            a1 = _mx(log_a)
            r1 = log_a - a1.astype(F32)
            a2 = _mx(r1)
            a3 = _mx(r1 - a2.astype(F32))
            bc_all = _dot(tri, a1) + _dot(tri, a2) + _dot(tri, a3)
            for h in range(GLA_HEADS):
                ks = slice(h * GLA_DK, (h + 1) * GLA_DK)
                vs = slice(h * GLA_DV, (h + 1) * GLA_DV)
                bc = bc_all[:, ks]
                q = q_ref[b, rows, ks] * GLA_DK ** -0.5
                k = k_ref[b, rows, ks]
                v = v_ref[b, rows, vs]
                b_ref = bc[ref:ref + 1, :]
                b_last = bc[cc - 1:cc, :]
                a = _dot_nt(_mx(q * jnp.exp(bc - b_ref)), _mx(k * jnp.exp(b_ref - bc)))
                a = jnp.where(causal, a, 0.0)
                st = st_sc[b * GLA_HEADS + h]
                o = _dot(_mx(a), _mx(v)) + _dot_nt(_mx(q * jnp.exp(bc)), _mx(st))
                kl = k * jnp.exp(b_last - bc)
                st_sc[b * GLA_HEADS + h] = st * jnp.exp(b_last) + _dot(_mx(v.T), _mx(kl))
                o = o * lax.rsqrt(jnp.mean(o * o, -1, keepdims=True) + RMS_EPS)
                gg = g_ref[b, rows, vs]
                o_ref[b, rows, vs] = (o * hn_ref[:, vs] * (gg * jax.nn.sigmoid(gg))).astype(o_ref.dtype)
        return carry

    lax.fori_loop(0, q_ref.shape[1] // cc, chunk, 0)


def _gla_mixer(x2, bsz, t, w_in, w_gate2, b_gate2, head_norm_g):
    n_main = 2 * GLA_KEY_DIM + 2 * GLA_VAL_DIM
    w_main = _mx(w_in[:, :n_main])
    w_low = jnp.pad(_mx(w_in[:, n_main:]), ((0, 0), (0, LANE - GLA_GATE_RANK)))
    proj = _matmul(x2, w_main, 512, n_main).reshape(bsz, t, n_main)
    a_low = _matmul(x2, w_low, 512, LANE).reshape(bsz, t, LANE)
    w2p = jnp.pad(_mx(w_gate2), ((0, LANE - GLA_GATE_RANK), (0, 0)))
    ts = min(GLA_TSTEP, t)
    out = pl.pallas_call(
        _gla_kernel,
        grid=(t // ts,),
        in_specs=[pl.BlockSpec((bsz, ts, GLA_KEY_DIM), lambda s: (0, s, 0)),
                  pl.BlockSpec((bsz, ts, GLA_KEY_DIM), lambda s: (0, s, 1)),
                  pl.BlockSpec((bsz, ts, GLA_VAL_DIM), lambda s: (0, s, 1)),
                  pl.BlockSpec((bsz, ts, GLA_VAL_DIM), lambda s: (0, s, 2)),
                  pl.BlockSpec((bsz, ts, LANE), lambda s: (0, s, 0)),
                  pl.BlockSpec((LANE, GLA_KEY_DIM), lambda s: (0, 0)),
                  pl.BlockSpec((1, GLA_KEY_DIM), lambda s: (0, 0)),
                  pl.BlockSpec((1, GLA_VAL_DIM), lambda s: (0, 0))],
        out_specs=pl.BlockSpec((bsz, ts, GLA_VAL_DIM), lambda s: (0, s, 0)),
        out_shape=jax.ShapeDtypeStruct((bsz, t, GLA_VAL_DIM), MXU_DTYPE),
        scratch_shapes=[pltpu.VMEM((bsz * GLA_HEADS, GLA_DV, GLA_DK), F32)],
        compiler_params=_params("arbitrary"),
        name="gla",
    )(proj, proj, proj, proj, a_low, w2p, b_gate2.reshape(1, -1), head_norm_g.reshape(1, -1))
    return out.reshape(bsz * t, GLA_VAL_DIM)


def _router_kernel(x_ref, wh_ref, wl_ref, o_ref):
    x = x_ref[...]
    xh = _mx(x)
    xl = _mx(x - xh.astype(F32))
    wh = wh_ref[...]
    logits = _dot(xh, wh) + _dot(xl, wh) + _dot(xh, wl_ref[...])
    lane = lax.broadcasted_iota(jnp.int32, logits.shape, 1)
    lg = jnp.where(lane < N_EXPERTS, logits, -jnp.inf)
    m1 = jnp.max(lg, axis=1, keepdims=True)
    i1 = jnp.min(jnp.where(lg == m1, lane, LANE), axis=1, keepdims=True)
    lg2 = jnp.where(lane == i1, -jnp.inf, lg)
    m2 = jnp.max(lg2, axis=1, keepdims=True)
    i2 = jnp.min(jnp.where(lg2 == m2, lane, LANE), axis=1, keepdims=True)
    e2 = jnp.exp(m2 - m1)
    den = 1.0 + e2
    w1 = 1.0 / den
    w2 = e2 / den
    out = jnp.where(lane == N_EXPERTS, i1.astype(F32), 0.0)
    out = jnp.where(lane == N_EXPERTS + 1, i2.astype(F32), out)
    out = jnp.where(lane == N_EXPERTS + 2, w1, out)
    out = jnp.where(lane == N_EXPERTS + 3, w2, out)
    o_ref[...] = out


def _moe_up_kernel(te_ref, x_ref, wg_ref, wu_ref, o_ref, wg_sc, wu_sc):
    i = pl.program_id(1)

    @pl.when((i == 0) | (te_ref[i] != te_ref[jnp.maximum(i - 1, 0)]))
    def _():
        wg_sc[...] = _mx(wg_ref[0])
        wu_sc[...] = _mx(wu_ref[0])

    _swiglu_cols(_mx(x_ref[...]), wg_sc, wu_sc, o_ref)


def _moe_down_kernel(te_ref, h_ref, w_ref, o_ref, w_sc):
    i = pl.program_id(0)

    @pl.when((i == 0) | (te_ref[i] != te_ref[jnp.maximum(i - 1, 0)]))
    def _():
        w_sc[...] = _mx(w_ref[0])

    h = h_ref[...]
    for cs in _col_chunks(o_ref.shape[1]):
        o_ref[:, cs] = _dot(h, w_sc[:, cs])


def _add_ln_kernel(x_ref, rt_ref, y0_ref, y1_ref, g_ref, b_ref, o_ref):
    rt = rt_ref[...]
    w0 = rt[:, N_EXPERTS + 2:N_EXPERTS + 3]
    w1 = rt[:, N_EXPERTS + 3:N_EXPERTS + 4]
    h = DN_ALPHA * x_ref[...] + (w0 * y0_ref[...] + w1 * y1_ref[...])
    o_ref[...] = _layer_norm_rows(h, g_ref[...], b_ref[...])


def _moe_layer(x2, w_router, w_gu, w_down, ln_g, ln_b):
    m, d = x2.shape
    tm = MOE_TM
    wr = jnp.pad(w_router, ((0, 0), (0, LANE - N_EXPERTS)))
    wr_hi = _mx(wr)
    wr_lo = _mx(wr - wr_hi.astype(F32))
    rt = pl.pallas_call(
        _router_kernel,
        grid=(m // 512,),
        in_specs=[pl.BlockSpec((512, d), lambda i: (i, 0)),
                  pl.BlockSpec((d, LANE), lambda i: (0, 0)),
                  pl.BlockSpec((d, LANE), lambda i: (0, 0))],
        out_specs=pl.BlockSpec((512, LANE), lambda i: (i, 0)),
        out_shape=jax.ShapeDtypeStruct((m, LANE), F32),
        compiler_params=_params("parallel"),
        name="moe_router",
    )(x2, wr_hi, wr_lo)
    top_idx = rt[:, N_EXPERTS:N_EXPERTS + 2].astype(jnp.int32)

    n_asg = m * TOP_K
    n_rows = n_asg + N_EXPERTS * tm
    n_tiles = n_rows // tm
    e_flat = top_idx.reshape(-1)
    order = jnp.argsort(e_flat, stable=True).astype(jnp.int32)
    slot = jnp.argsort(order).astype(jnp.int32)
    counts = jnp.sum((e_flat[:, None] == jnp.arange(N_EXPERTS)[None, :]).astype(jnp.int32), axis=0)
    tiles_per = (counts + tm - 1) // tm
    tile_end = jnp.cumsum(tiles_per)
    row_start = (tile_end - tiles_per) * tm
    grp_start = jnp.cumsum(counts) - counts
    tile_expert = jnp.minimum(jnp.sum((jnp.arange(n_tiles)[:, None] >= tile_end[None, :]).astype(jnp.int32),
                                      axis=1), N_EXPERTS - 1).astype(jnp.int32)
    pos = (row_start[e_flat] + slot - grp_start[e_flat]).astype(jnp.int32).reshape(m, TOP_K)
    row_e = jnp.repeat(tile_expert, tm)
    row_src = (grp_start - row_start)[row_e].astype(jnp.int32) + jnp.arange(n_rows, dtype=jnp.int32)
    row_token = order[jnp.clip(row_src, 0, n_asg - 1)] // TOP_K

    xs = x2[row_token]
    f = w_gu.shape[2] // 2
    tf = MOE_TF
    nf = f // tf
    h = pl.pallas_call(
        _moe_up_kernel,
        grid_spec=pltpu.PrefetchScalarGridSpec(
            num_scalar_prefetch=1,
            grid=(nf, n_tiles),
            in_specs=[pl.BlockSpec((tm, d), lambda j, i, te: (i, 0)),
                      pl.BlockSpec((1, d, tf), lambda j, i, te: (te[i], 0, j)),
                      pl.BlockSpec((1, d, tf), lambda j, i, te: (te[i], 0, j + nf))],
            out_specs=pl.BlockSpec((tm, tf), lambda j, i, te: (i, j)),
            scratch_shapes=[pltpu.VMEM((d, tf), MXU_DTYPE), pltpu.VMEM((d, tf), MXU_DTYPE)],
        ),
        out_shape=jax.ShapeDtypeStruct((n_rows, f), MXU_DTYPE),
        compiler_params=_params("arbitrary", "arbitrary"),
        name="moe_up",
    )(tile_expert, xs, w_gu, w_gu)
    ys = pl.pallas_call(
        _moe_down_kernel,
        grid_spec=pltpu.PrefetchScalarGridSpec(
            num_scalar_prefetch=1,
            grid=(n_tiles,),
            in_specs=[pl.BlockSpec((tm, f), lambda i, te: (i, 0)),
                      pl.BlockSpec((1, f, d), lambda i, te: (te[i], 0, 0))],
            out_specs=pl.BlockSpec((tm, d), lambda i, te: (i, 0)),
            scratch_shapes=[pltpu.VMEM((f, d), MXU_DTYPE)],
        ),
        out_shape=jax.ShapeDtypeStruct((n_rows, d), F32),
        compiler_params=_params("arbitrary"),
        name="moe_down",
    )(tile_expert, h, w_down)

    y0 = ys[pos[:, 0]]
    y1 = ys[pos[:, 1]]
    return pl.pallas_call(
        _add_ln_kernel,
        grid=(m // 512,),
        in_specs=[pl.BlockSpec((512, d), lambda i: (i, 0)), pl.BlockSpec((512, LANE), lambda i: (i, 0)),
                  pl.BlockSpec((512, d), lambda i: (i, 0)), pl.BlockSpec((512, d), lambda i: (i, 0))]
        + [pl.BlockSpec((1, d), lambda i: (0, 0))] * 2,
        out_specs=pl.BlockSpec((512, d), lambda i: (i, 0)),
        out_shape=jax.ShapeDtypeStruct((m, d), F32),
        compiler_params=_params("parallel"),
        name="moe_add_ln",
    )(x2, rt, y0, y1, ln_g.reshape(1, d), ln_b.reshape(1, d))


def kernel(x, l0_w_in, l0_cmp_pe_k, l0_cmp_pe_v, l0_cmp_wk1, l0_cmp_wk2, l0_cmp_wv1, l0_cmp_wv2, l0_w_o, l0_ln1_g, l0_ln1_b, l0_ffn_w_gu, l0_ffn_w_down, l0_ln2_g, l0_ln2_b, l1_w_in, l1_w_gate2, l1_b_gate2, l1_head_norm_g, l1_w_o, l1_ln1_g, l1_ln1_b, l1_router, l1_moe_w_gu, l1_moe_w_down, l1_ln2_g, l1_ln2_b):
    bsz, t, d = x.shape
    x2 = x.reshape(bsz * t, d)

    o = _nsa_mixer(x2, bsz, t, l0_w_in, l0_cmp_pe_k, l0_cmp_pe_v, l0_cmp_wk1, l0_cmp_wk2,
                   l0_cmp_wv1, l0_cmp_wv2)
    x2 = _matmul_res_ln(o, _mx(l0_w_o), x2, l0_ln1_g, l0_ln1_b, 512)
    hmid = _swiglu_up(x2, _mx(l0_ffn_w_gu), 512, FFN_DENSE)
    x2 = _matmul_res_ln(hmid, _mx(l0_ffn_w_down), x2, l0_ln2_g, l0_ln2_b, 512)

    o = _gla_mixer(x2, bsz, t, l1_w_in, l1_w_gate2, l1_b_gate2, l1_head_norm_g)
    x2 = _matmul_res_ln(o, _mx(l1_w_o), x2, l1_ln1_g, l1_ln1_b, 512)
    x2 = _moe_layer(x2, l1_router, l1_moe_w_gu, l1_moe_w_down, l1_ln2_g, l1_ln2_b)
    return x2.reshape(bsz, t, d)
```

```python
import functools

import numpy as np
import jax
import jax.numpy as jnp
from jax import lax
from jax.experimental import pallas as pl
from jax.experimental.pallas import tpu as pltpu

F32 = jnp.float32
MXU_DTYPE = jnp.bfloat16

D_MODEL = 1024
DEPTH = 2
DN_ALPHA = (2 * DEPTH) ** 0.25
LN_EPS = 1e-5
RMS_EPS = 1e-6

NSA_HEADS = 16
NSA_HEAD_DIM = 64
NSA_KV_GROUPS = 4
NSA_HPG = NSA_HEADS // NSA_KV_GROUPS
NSA_KV_DIM = NSA_KV_GROUPS * NSA_HEAD_DIM
CMP_LEN = 32
CMP_STRIDE = 16
CMP_HIDDEN = 256
SLC_LEN = 64
N_SEL = 16
WINDOW = 512
QTILE = 128
KBLK = 128
NCOL = NSA_HPG * QTILE
CMP_CHUNK = 128
SEL_UNROLL = 6
VT_ROWS = 80
LOG2E = 1.4426950408889634

GLA_HEADS = 4
GLA_KEY_DIM = D_MODEL // 2
GLA_VAL_DIM = D_MODEL
GLA_DK = GLA_KEY_DIM // GLA_HEADS
GLA_DV = GLA_VAL_DIM // GLA_HEADS
GLA_GATE_RANK = 16
GLA_GATE_NORM = 16.0
GLA_CHUNK = 64
GLA_TSTEP = 256

FFN_DENSE = 2816
N_EXPERTS = 8
TOP_K = 2
FFN_EXPERT = 3584
MOE_TM = 512
MOE_TF = 1792
SWIGLU_CHUNK = 256
MM_CHUNK = 512

LANE = 128
NEG = -1e30
VMEM_LIMIT = 56 * 1024 * 1024


def _params(*sem):
    return pltpu.CompilerParams(dimension_semantics=sem, vmem_limit_bytes=VMEM_LIMIT)


def _mx(a):
    return a.astype(MXU_DTYPE)


def _dot(a, b):
    return jnp.dot(a, b, preferred_element_type=F32)


def _dot_nt(a, b):
    return lax.dot_general(a, b, (((1,), (1,)), ((), ())), preferred_element_type=F32)


def _col_chunks(n):
    step = next((s for s in (MM_CHUNK, MM_CHUNK + LANE, MM_CHUNK - LANE) if n % s == 0), n)
    return [slice(j, j + step) for j in range(0, n, step)]


def _mm_kernel(x_ref, w_ref, o_ref):
    x = _mx(x_ref[...])
    for cs in _col_chunks(o_ref.shape[1]):
        o_ref[:, cs] = _dot(x, w_ref[:, cs]).astype(o_ref.dtype)


def _matmul(x, w, tm, tn, out_dtype=F32):
    m, k = x.shape
    n = w.shape[1]
    return pl.pallas_call(
        _mm_kernel,
        grid=(m // tm, n // tn),
        in_specs=[pl.BlockSpec((tm, k), lambda i, j: (i, 0)),
                  pl.BlockSpec((k, tn), lambda i, j: (0, j))],
        out_specs=pl.BlockSpec((tm, tn), lambda i, j: (i, j)),
        out_shape=jax.ShapeDtypeStruct((m, n), out_dtype),
        compiler_params=_params("parallel", "parallel"),
        name="matmul",
    )(x, w)


def _mm_scaled_kernel(x_ref, w_ref, s_ref, o_ref):
    x = _mx(x_ref[...])
    for cs in _col_chunks(o_ref.shape[1]):
        o_ref[:, cs] = (_dot(x, w_ref[:, cs]) * s_ref[:, cs]).astype(o_ref.dtype)


def _matmul_scaled(x, w, col_scale, tm, tn, out_dtype):
    m, k = x.shape
    n = w.shape[1]
    return pl.pallas_call(
        _mm_scaled_kernel,
        grid=(m // tm, n // tn),
        in_specs=[pl.BlockSpec((tm, k), lambda i, j: (i, 0)),
                  pl.BlockSpec((k, tn), lambda i, j: (0, j)),
                  pl.BlockSpec((1, tn), lambda i, j: (0, j))],
        out_specs=pl.BlockSpec((tm, tn), lambda i, j: (i, j)),
        out_shape=jax.ShapeDtypeStruct((m, n), out_dtype),
        compiler_params=_params("parallel", "parallel"),
        name="matmul_scaled",
    )(x, w, col_scale)


def _layer_norm_rows(h, g, b):
    mu = jnp.mean(h, -1, keepdims=True)
    d = h - mu
    var = jnp.mean(d * d, -1, keepdims=True)
    return d * lax.rsqrt(var + LN_EPS) * g + b


def _mm_ln_kernel(x_ref, w_ref, r_ref, g_ref, b_ref, o_ref):
    y = _dot(_mx(x_ref[...]), w_ref[...])
    o_ref[...] = _layer_norm_rows(DN_ALPHA * r_ref[...] + y, g_ref[...], b_ref[...])


def _matmul_res_ln(x, w, res, g, b, tm):
    m, k = x.shape
    n = w.shape[1]
    return pl.pallas_call(
        _mm_ln_kernel,
        grid=(m // tm,),
        in_specs=[pl.BlockSpec((tm, k), lambda i: (i, 0)),
                  pl.BlockSpec((k, n), lambda i: (0, 0)),
                  pl.BlockSpec((tm, n), lambda i: (i, 0)),
                  pl.BlockSpec((1, n), lambda i: (0, 0)),
                  pl.BlockSpec((1, n), lambda i: (0, 0))],
        out_specs=pl.BlockSpec((tm, n), lambda i: (i, 0)),
        out_shape=jax.ShapeDtypeStruct((m, n), F32),
        compiler_params=_params("parallel"),
        name="matmul_res_ln",
    )(x, w, res, g.reshape(1, n), b.reshape(1, n))


def _swiglu_cols(x, wg_ref, wu_ref, o_ref):
    n = o_ref.shape[1]
    step = SWIGLU_CHUNK if n % SWIGLU_CHUNK == 0 else LANE
    for j in range(0, n, step):
        a = _dot(x, wg_ref[:, j:j + step])
        b = _dot(x, wu_ref[:, j:j + step])
        o_ref[:, j:j + step] = (a * jax.nn.sigmoid(a) * b).astype(o_ref.dtype)


def _swiglu_up_kernel(x_ref, wg_ref, wu_ref, o_ref):
    _swiglu_cols(_mx(x_ref[...]), wg_ref, wu_ref, o_ref)


def _swiglu_up(x, w_gu, tm, tn):
    m, k = x.shape
    f = w_gu.shape[1] // 2
    nj = f // tn
    return pl.pallas_call(
        _swiglu_up_kernel,
        grid=(m // tm, nj),
        in_specs=[pl.BlockSpec((tm, k), lambda i, j: (i, 0)),
                  pl.BlockSpec((k, tn), lambda i, j: (0, j)),
                  pl.BlockSpec((k, tn), lambda i, j: (0, j + nj))],
        out_specs=pl.BlockSpec((tm, tn), lambda i, j: (i, j)),
        out_shape=jax.ShapeDtypeStruct((m, f), MXU_DTYPE),
        compiler_params=_params("parallel", "parallel"),
        name="swiglu_up",
    )(x, w_gu, w_gu)


def _cmp_up_kernel(x0_ref, x1_ref, pea_ref, peb_ref, wa_ref, wb_ref, p_ref, q_ref):
    tm = p_ref.shape[0]
    half_out = p_ref.shape[1] // 2
    for hi, x_ref in enumerate((x0_ref, x1_ref)):
        cs = slice(hi * LANE, (hi + 1) * LANE)
        os_ = slice(hi * half_out, (hi + 1) * half_out)
        p = q = None
        for l in range(CMP_STRIDE):
            xl = x_ref[pl.ds(l, tm, stride=CMP_STRIDE), :]
            da = _dot(_mx(xl + pea_ref[l:l + 1, cs]), wa_ref[l, cs, os_])
            db = _dot(_mx(xl + peb_ref[l:l + 1, cs]), wb_ref[l, cs, os_])
            p = da if p is None else p + da
            q = db if q is None else q + db
        p_ref[:, os_] = p
        q_ref[:, os_] = q


def _cmp_down_kernel(p_ref, q_ref, w_ref, o_ref):
    for gi in range(NSA_KV_GROUPS):
        cs = slice(gi * CMP_HIDDEN, (gi + 1) * CMP_HIDDEN)
        h = jax.nn.gelu(p_ref[:, cs] + q_ref[:, cs])
        o_ref[:, gi * LANE:(gi + 1) * LANE] = _dot(_mx(h), w_ref[...])


def _compress(proj_f, col_blk, bsz, pe, w1, w2):
    g, dk = NSA_KV_GROUPS, NSA_HEAD_DIM
    nch_all = proj_f.shape[0] // CMP_STRIDE
    nch = nch_all // bsz
    tm = min(256, nch_all)
    eye = jnp.eye(g, dtype=F32)

    def block_diag(w_half):
        w3 = w_half.reshape(CMP_STRIDE, dk, CMP_HIDDEN)
        return _mx(jnp.einsum("ldc,gh->lgdhc", w3, eye).reshape(CMP_STRIDE, g * dk, g * CMP_HIDDEN))

    half = CMP_STRIDE * dk
    pe_a = jnp.tile(pe[:CMP_STRIDE], (1, g))
    pe_b = jnp.tile(pe[CMP_STRIDE:], (1, g))
    wide = g * CMP_HIDDEN
    p, q = pl.pallas_call(
        _cmp_up_kernel,
        grid=(nch_all // tm,),
        in_specs=[pl.BlockSpec((tm * CMP_STRIDE, LANE), lambda i: (i, 2 * col_blk)),
                  pl.BlockSpec((tm * CMP_STRIDE, LANE), lambda i: (i, 2 * col_blk + 1)),
                  pl.BlockSpec((CMP_STRIDE, g * dk), lambda i: (0, 0)),
                  pl.BlockSpec((CMP_STRIDE, g * dk), lambda i: (0, 0)),
                  pl.BlockSpec((CMP_STRIDE, g * dk, wide), lambda i: (0, 0, 0)),
                  pl.BlockSpec((CMP_STRIDE, g * dk, wide), lambda i: (0, 0, 0))],
        out_specs=[pl.BlockSpec((tm, wide), lambda i: (i, 0)),
                   pl.BlockSpec((tm, wide), lambda i: (i, 0))],
        out_shape=[jax.ShapeDtypeStruct((nch_all, wide), F32)] * 2,
        compiler_params=_params("parallel"),
        name="cmp_up",
    )(proj_f, proj_f, pe_a, pe_b, block_diag(w1[:half]), block_diag(w1[half:]))
    q = q.reshape(bsz, nch, wide)
    q = jnp.concatenate([q[:, 1:], jnp.zeros_like(q[:, :1])], axis=1).reshape(nch_all, wide)
    w2p = jnp.pad(_mx(w2), ((0, 0), (0, LANE - dk)))
    out = pl.pallas_call(
        _cmp_down_kernel,
        grid=(nch_all // tm,),
        in_specs=[pl.BlockSpec((tm, wide), lambda i: (i, 0)),
                  pl.BlockSpec((tm, wide), lambda i: (i, 0)),
                  pl.BlockSpec((CMP_HIDDEN, LANE), lambda i: (0, 0))],
        out_specs=pl.BlockSpec((tm, g * LANE), lambda i: (i, 0)),
        out_shape=jax.ShapeDtypeStruct((nch_all, g * LANE), F32),
        compiler_params=_params("parallel"),
        name="cmp_down",
    )(p, q, w2p)
    out = out.reshape(bsz, nch, g, LANE)[..., :dk]
    valid = (jnp.arange(nch) < nch - 1)[None, :, None, None]
    return jnp.where(valid, out, 0.0)


def _queries_t(q_blk):
    qt = q_blk.astype(F32).T
    dk = NSA_HEAD_DIM
    return _mx(jnp.concatenate([qt[h * dk:(h + 1) * dk, :] for h in range(NSA_HPG)], axis=1))


def _nsa_cmp_kernel(q_ref, qf_ref, kc_ref, vct_ref, ovl_ref, ocmp_ref, sel_ref, flag_ref,
                    imp_ref, key_ref, cnt_ref):
    c = pl.program_id(2)
    ncp = kc_ref.shape[2]
    ns = ovl_ref.shape[0]
    dk = NSA_HEAD_DIM
    rhs = jnp.concatenate([_queries_t(q_ref[...]), qf_ref[0]], axis=0)
    chunk = min(CMP_CHUNK, ncp)
    n_chunks = (c * (QTILE // CMP_STRIDE) + (QTILE - CMP_LEN) // CMP_STRIDE + chunk) // chunk

    def cmp_branch(rows):
        n_i = lax.broadcasted_iota(jnp.int32, (rows, QTILE), 0)
        tq = c * QTILE + lax.broadcasted_iota(jnp.int32, (rows, QTILE), 1)
        bias = _mx(jnp.where(n_i * CMP_STRIDE + (CMP_LEN - 1) <= tq, 0.0, NEG))
        s = _dot(jnp.concatenate([kc_ref[0, 0, :rows, :], bias], axis=1), rhs)
        e = jnp.exp2(s - jnp.max(s, axis=0, keepdims=True))
        seen = jnp.concatenate([tq[0:1, :]] * NSA_HPG, axis=1) >= CMP_LEN - 1
        r = jnp.where(seen, 1.0 / jnp.maximum(jnp.sum(e, axis=0, keepdims=True), 1e-30), 0.0)
        ocmp_ref[0, 0, 0] = _dot(vct_ref[0, 0, :, :rows], _mx(e)) * r
        psum = e[:, 0:QTILE] * r[:, 0:QTILE]
        for h in range(1, NSA_HPG):
            psum = psum + e[:, h * QTILE:(h + 1) * QTILE] * r[:, h * QTILE:(h + 1) * QTILE]
        hi = _mx(psum)
        lo = _mx(psum - hi.astype(F32))
        ovl = ovl_ref[:, :rows]
        imp_ref[...] = _dot(ovl, hi) + _dot(ovl, lo)

    for k in range(1, ncp // chunk + 1):
        pl.when(n_chunks == k)(functools.partial(cmp_branch, k * chunk))

    j_i = lax.broadcasted_iota(jnp.int32, (ns, QTILE), 0)
    tq2 = c * QTILE + lax.broadcasted_iota(jnp.int32, (ns, QTILE), 1)
    valid = j_i * SLC_LEN <= tq2
    cur = tq2 >> (SLC_LEN.bit_length() - 1)
    forced = valid & ((j_i == 0) | (j_i == cur) | (j_i == cur - 1))
    score = jnp.where(forced, 1e9, jnp.where(valid, imp_ref[...], -1e9))
    bits = lax.bitcast_convert_type(score, jnp.int32)
    key_ref[...] = jnp.where(bits < 0, bits ^ 0x7FFFFFFF, bits)
    cnt_ref[...] = jnp.zeros(cnt_ref.shape, jnp.int32)

    sub = lax.broadcasted_iota(jnp.int32, (8, QTILE), 0)
    n_it = (c * (QTILE // SLC_LEN) + QTILE // SLC_LEN + 7) // 8

    def rank_blocks(n_jt):
        def it_body(it, carry):
            base = pl.multiple_of(it * 8, 8)
            rows8 = key_ref[pl.ds(base, 8), :]
            rb = [jnp.broadcast_to(rows8[il:il + 1, :], (8, QTILE)) for il in range(8)]
            for jt in range(n_jt):
                js = slice(jt * 8, (jt + 1) * 8)
                thr = key_ref[js, :] - jnp.where(jt > it, 1, 0)
                acc = cnt_ref[js, :]
                for il in range(8):
                    acc = acc + jnp.where(rb[il] > thr, 1, 0)
                cnt_ref[js, :] = acc
            corr = jnp.zeros((8, QTILE), jnp.int32)
            for il in range(8):
                corr = corr + jnp.where((rb[il] == rows8) & (sub > il), 1, 0)
            cnt_ref[pl.ds(base, 8), :] += corr
            return carry

        lax.fori_loop(0, n_it, it_body, 0)

    sizes = sorted({min(ns // 8, max(1, ns // 32) * k) for k in range(1, 5)})
    for lo, hi in zip([0] + sizes[:-1], sizes):
        pl.when((n_it > lo) & (n_it <= hi))(functools.partial(rank_blocks, hi))
    sel = jnp.where((cnt_ref[...] < min(N_SEL, ns)) & valid, 1.0, 0.0)
    sel_ref[0, 0, 0] = sel
    flag_ref[0, 0, 0] = _dot_nt(jnp.ones((8, QTILE), MXU_DTYPE), _mx(sel))


def _nsa_cmp_select(proj, q_feat, kcmp, vcmp_t, ovl_t):
    bsz, g, ncp, _ = kcmp.shape
    dk = NSA_HEAD_DIM
    nt = proj.shape[0] // (bsz * QTILE)
    ns = ovl_t.shape[0]
    return pl.pallas_call(
        _nsa_cmp_kernel,
        grid=(bsz, g, nt),
        in_specs=[pl.BlockSpec((QTILE, NSA_HPG * dk), lambda b, gg, c: (b * nt + c, gg)),
                  pl.BlockSpec((1, dk + QTILE, NCOL), lambda b, gg, c: (gg, 0, 0)),
                  pl.BlockSpec((1, 1, ncp, LANE), lambda b, gg, c: (b, gg, 0, 0)),
                  pl.BlockSpec((1, 1, dk, ncp), lambda b, gg, c: (b, gg, 0, 0)),
                  pl.BlockSpec((ns, ncp), lambda b, gg, c: (0, 0))],
        out_specs=[pl.BlockSpec((1, 1, 1, dk, NCOL), lambda b, gg, c: (b, gg, c, 0, 0)),
                   pl.BlockSpec((1, 1, 1, ns, QTILE), lambda b, gg, c: (b, gg, c, 0, 0)),
                   pl.BlockSpec((1, 1, 1, 8, ns), lambda b, gg, c: (b, gg, c, 0, 0))],
        out_shape=[jax.ShapeDtypeStruct((bsz, g, nt, dk, NCOL), F32),
                   jax.ShapeDtypeStruct((bsz, g, nt, ns, QTILE), F32),
                   jax.ShapeDtypeStruct((bsz, g, nt, 8, ns), F32)],
        scratch_shapes=[pltpu.VMEM((ns, QTILE), F32), pltpu.VMEM((ns, QTILE), jnp.int32),
                        pltpu.VMEM((ns, QTILE), jnp.int32)],
        compiler_params=_params("parallel", "parallel", "parallel"),
        name="nsa_cmp_select",
    )(proj, q_feat, kcmp, vcmp_t, ovl_t)


def _list_stride(npair):
    trip = 2 * SEL_UNROLL
    return -(-npair // trip) * trip + SEL_UNROLL


def _values_t(v_blk):
    vt = v_blk.astype(F32).T[:VT_ROWS, :]
    ones_row = lax.broadcasted_iota(jnp.int32, vt.shape, 0) == NSA_HEAD_DIM
    return _mx(jnp.where(ones_row, 1.0, vt))


def _nsa_attn_kernel(lst_ref, cnt_ref, q_ref, qf_ref, ks_ref, vs_ref, kw_ref, vw_ref, pos_ref, sel_ref,
                     ocmp_ref, gl_ref, o_ref, ksa_sc, kwa_sc, vst_sc, vwt_sc, gate_sc, sa_sc, ow_sc, m_sc, acc_sc):
    b = pl.program_id(0)
    g = pl.program_id(1)
    c = pl.program_id(2)
    nt = pl.num_programs(2)
    npair = ks_ref.shape[1] // KBLK
    dk = NSA_HEAD_DIM

    @pl.when(c == 0)
    def _():
        def prep(j, carry):
            rows = pl.ds(pl.multiple_of(j * KBLK, KBLK), KBLK)
            feats = pos_ref[rows, :]
            ksa_sc[rows, :] = ks_ref[0, rows, :] + feats
            kwa_sc[rows, :] = kw_ref[0, rows, :] + feats
            vst_sc[j] = _values_t(vs_ref[0, rows, :])
            vwt_sc[j] = _values_t(vw_ref[0, rows, :])
            return carry

        lax.fori_loop(0, npair, prep, 0)

    rhs = jnp.concatenate([_queries_t(q_ref[...]), qf_ref[0]], axis=0)
    key_i = lax.broadcasted_iota(jnp.int32, (KBLK, QTILE), 0)
    q_i = lax.broadcasted_iota(jnp.int32, (KBLK, QTILE), 1)
    heads = [slice(h * QTILE, (h + 1) * QTILE) for h in range(NSA_HPG)]

    m_sc[...] = jnp.full(m_sc.shape, NEG, F32)
    acc_sc[...] = jnp.zeros(acc_sc.shape, F32)
    step = (b * NSA_KV_GROUPS + g) * nt + c

    def group_blocks(gi):
        return [lst_ref[step * _list_stride(npair) + gi * SEL_UNROLL + u] for u in range(SEL_UNROLL)]

    def group_scores(gi, own_first):
        scores = []
        for u, jj in enumerate(group_blocks(gi)):
            js = jnp.maximum(jj, 0)
            rows = sel_ref[0, 0, 0, pl.ds(2 * js, 2), :]
            if own_first and u == 0:
                picked = jnp.where(key_i < SLC_LEN, rows[0:1, :], rows[1:2, :]) > 0.5
                bias = _mx(jnp.where(picked & (key_i <= q_i), 0.0, NEG))
            else:
                half = _mx(jnp.where(jj >= 0, (rows - 1.0) * -NEG, NEG))
                bias = jnp.concatenate([jnp.broadcast_to(half[0:1, :], (SLC_LEN, QTILE)),
                                        jnp.broadcast_to(half[1:2, :], (SLC_LEN, QTILE))], axis=0)
            keys = ksa_sc[pl.ds(pl.multiple_of(js * KBLK, KBLK), KBLK), :]
            scores.append(_dot(jnp.concatenate([keys, bias], axis=1), rhs))
        return scores

    def softmax_update(gi, scores):
        blocks = [jnp.maximum(jj, 0) for jj in group_blocks(gi)]
        m_new = []
        for cs in heads:
            m_old = m_sc[:, cs]
            mx_ = m_old
            for u in range(SEL_UNROLL):
                mx_ = jnp.maximum(mx_, jnp.max(scores[u][:, cs], axis=0, keepdims=True))
            acc_sc[:, cs] = jnp.exp2(m_old - mx_) * acc_sc[:, cs]
            m_sc[:, cs] = mx_
            m_new.append(mx_)
        upd = None
        for u in range(SEL_UNROLL):
            e = jnp.concatenate([_mx(jnp.exp2(scores[u][:, cs] - m_new[hi])) for hi, cs in enumerate(heads)],
                                axis=1)
            d = _dot(vst_sc[blocks[u]], e)
            upd = d if upd is None else upd + d
        acc_sc[...] += upd

    for u, s in enumerate(group_scores(0, True)):
        sa_sc[u] = s

    nwin = WINDOW // KBLK + 1
    blk0 = jnp.maximum(c - (nwin - 1), 0)
    row = lax.broadcasted_iota(jnp.int32, (nwin * KBLK, QTILE), 0)
    dist = lax.broadcasted_iota(jnp.int32, (nwin * KBLK, QTILE), 1) + (c - blk0) * KBLK - row
    bias = _mx(jnp.where((dist >= 0) & (dist < WINDOW), 0.0, NEG))
    keys = kwa_sc[pl.ds(pl.multiple_of(blk0 * KBLK, KBLK), nwin * KBLK), :]
    s = _dot(jnp.concatenate([keys, bias], axis=1), rhs)
    e_all = jnp.concatenate([_mx(jnp.exp2(s[:, cs] - jnp.max(s[:, cs], axis=0, keepdims=True)))
                             for cs in heads], axis=1)
    acc = None
    for t in range(nwin):
        d = _dot(vwt_sc[blk0 + t], e_all[t * KBLK:(t + 1) * KBLK, :])
        acc = d if acc is None else acc + d
    ow_sc[...] = acc[:dk, :] * (1.0 / jnp.maximum(acc[dk:dk + 1, :], 1e-30))

    def trip(r, carry):
        scores_b = group_scores(2 * r + 1, False)
        softmax_update(2 * r, [sa_sc[u] for u in range(SEL_UNROLL)])
        for u, s in enumerate(group_scores(2 * r + 2, False)):
            sa_sc[u] = s
        softmax_update(2 * r + 1, scores_b)
        return carry

    trip(0, 0)
    lax.fori_loop(1, (cnt_ref[step] + 2 * SEL_UNROLL - 1) // (2 * SEL_UNROLL), trip, 0)
    o_sel = acc_sc[:dk, :] * (1.0 / jnp.maximum(acc_sc[dk:dk + 1, :], 1e-30))
    o_win = ow_sc[...]

    gate_sc[...] = gl_ref[...].T
    ocmp = ocmp_ref[0, 0, 0]
    outs = []
    for h, cs in enumerate(heads):
        gt = jax.nn.sigmoid(gate_sc[pl.ds((g * NSA_HPG + h) * 3, 3), :])
        outs.append(gt[0:1] * ocmp[:, cs] + gt[1:2] * o_sel[:, cs] + gt[2:3] * o_win[:, cs])
    o_ref[...] = jnp.concatenate(outs, axis=0).T.astype(o_ref.dtype)


def _nsa_attend(blk_list, blk_count, proj, proj_f, q_feat, pos_feat, sel_t, ocmp_t, bsz):
    g, dk = NSA_KV_GROUPS, NSA_HEAD_DIM
    t = proj.shape[0] // bsz
    nt = t // QTILE
    npair = t // KBLK
    ns = sel_t.shape[3]
    proj3 = proj.reshape(bsz, t, proj.shape[1])
    seg = D_MODEL // LANE
    gate_blk = proj_f.shape[1] // LANE - 1
    tile = lambda b, gg, c, lst, cnt: (b, gg, c, 0, 0)
    rows = lambda b, gg, c, lst, cnt: (b * nt + c, gg)
    kv = lambda k: (lambda b, gg, c, lst, cnt: (b, 0, seg + k * g + gg))
    grid_spec = pltpu.PrefetchScalarGridSpec(
        num_scalar_prefetch=2,
        grid=(bsz, g, nt),
        in_specs=[pl.BlockSpec((QTILE, NSA_HPG * dk), rows),
                  pl.BlockSpec((1, dk + QTILE, NCOL), lambda b, gg, c, lst, cnt: (gg, 0, 0)),
                  pl.BlockSpec((1, t, LANE), kv(0)),
                  pl.BlockSpec((1, t, LANE), kv(1)),
                  pl.BlockSpec((1, t, LANE), kv(2)),
                  pl.BlockSpec((1, t, LANE), kv(3)),
                  pl.BlockSpec((t, LANE), lambda b, gg, c, lst, cnt: (0, 0)),
                  pl.BlockSpec((1, 1, 1, ns, QTILE), tile),
                  pl.BlockSpec((1, 1, 1, dk, NCOL), tile),
                  pl.BlockSpec((QTILE, LANE), lambda b, gg, c, lst, cnt: (b * nt + c, gate_blk))],
        out_specs=pl.BlockSpec((QTILE, NSA_HPG * dk), rows),
        scratch_shapes=[pltpu.VMEM((t, LANE), MXU_DTYPE), pltpu.VMEM((t, LANE), MXU_DTYPE),
                        pltpu.VMEM((npair, VT_ROWS, KBLK), MXU_DTYPE),
                        pltpu.VMEM((npair, VT_ROWS, KBLK), MXU_DTYPE),
                        pltpu.VMEM((LANE, QTILE), F32),
                        pltpu.VMEM((SEL_UNROLL, KBLK, NCOL), F32),
                        pltpu.VMEM((dk, NCOL), F32),
                        pltpu.VMEM((1, NCOL), F32), pltpu.VMEM((VT_ROWS, NCOL), F32)],
    )
    return pl.pallas_call(
        _nsa_attn_kernel,
        grid_spec=grid_spec,
        out_shape=jax.ShapeDtypeStruct((bsz * t, NSA_HEADS * dk), MXU_DTYPE),
        compiler_params=_params("parallel", "parallel", "arbitrary"),
        name="nsa_attend",
    )(blk_list, blk_count, proj, q_feat, proj3, proj3, proj3, proj3, pos_feat, sel_t, ocmp_t, proj_f)


def _nsa_mixer(x2, bsz, t, w_in, pe_k, pe_v, wk1, wk2, wv1, wv2):
    g, hpg, dk = NSA_KV_GROUPS, NSA_HPG, NSA_HEAD_DIM
    nt = t // QTILE
    d = w_in.shape[0]
    splits = np.cumsum([D_MODEL] + [NSA_KV_DIM] * 6).tolist()
    wq, wkc, wvc, wks, wvs, wkw, wvw, wgl = jnp.split(w_in, splits, axis=1)

    def lane_groups(w):
        return jnp.pad(w.reshape(d, g, dk), ((0, 0), (0, 0), (0, LANE - dk))).reshape(d, g * LANE)

    w_a = _mx(jnp.concatenate([wq] + [lane_groups(w) for w in (wks, wvs, wkw, wvw)], axis=1))
    scale = jnp.concatenate([jnp.full((1, D_MODEL), dk ** -0.5 * LOG2E, F32),
                             jnp.ones((1, w_a.shape[1] - D_MODEL), F32)], axis=1)
    w_b = _mx(jnp.concatenate([wkc, wvc, jnp.pad(wgl, ((0, 0), (0, LANE - wgl.shape[1])))], axis=1))
    proj = _matmul_scaled(x2, w_a, scale, 512, w_a.shape[1], MXU_DTYPE)
    proj_f = _matmul(x2, w_b, 512, w_b.shape[1])

    slopes = 2.0 ** (-8.0 * (jnp.arange(NSA_HEADS, dtype=F32) + 1.0) / NSA_HEADS) * LOG2E
    slopes = jnp.broadcast_to(slopes.reshape(g, 1, hpg, 1), (g, 1, hpg, QTILE)).reshape(g, 1, NCOL)
    s1 = slopes.astype(jnp.bfloat16)
    s2 = (slopes - s1.astype(F32)).astype(jnp.bfloat16)
    s3 = (slopes - s1.astype(F32) - s2.astype(F32)).astype(jnp.bfloat16)
    tail = np.zeros((g, dk - 6 + QTILE, NCOL), np.float32)
    tail[:, dk - 6:, :] = np.tile(np.eye(QTILE, dtype=np.float32), (1, hpg))
    q_feat = jnp.concatenate([s1 * SLC_LEN, s2 * SLC_LEN, s3 * SLC_LEN, s1, s2, s3, jnp.asarray(tail, jnp.bfloat16)],
                             axis=1).astype(MXU_DTYPE)

    def pos_features(kpos):
        feats = np.zeros((kpos.shape[0], LANE - dk), np.float32)
        feats[:, 0:3] = (kpos // SLC_LEN)[:, None]
        feats[:, 3:6] = (kpos % SLC_LEN)[:, None]
        return feats

    pos_feat = jnp.asarray(np.pad(pos_features(np.arange(t)), ((0, 0), (dk, 0))), MXU_DTYPE)

    ncp = t // CMP_STRIDE
    k_cmp = _compress(proj_f, 0, bsz, pe_k, wk1, wk2)
    v_cmp = _compress(proj_f, 1, bsz, pe_v, wv1, wv2)
    cmp_feat = jnp.asarray(pos_features(np.arange(ncp) * CMP_STRIDE + (CMP_LEN - 1)), MXU_DTYPE)
    kcmp = jnp.concatenate([_mx(k_cmp).transpose(0, 2, 1, 3),
                            jnp.broadcast_to(cmp_feat[None, None], (bsz, g, ncp, LANE - dk))], axis=-1)
    vcmp_t = _mx(v_cmp).transpose(0, 2, 3, 1)

    ns = t // SLC_LEN
    cs = np.arange(ncp) * CMP_STRIDE
    ce = cs + CMP_LEN - 1
    bs = np.arange(ns) * SLC_LEN
    be = bs + SLC_LEN - 1
    ovl_t = ((cs[None, :] <= be[:, None]) & (ce[None, :] >= bs[:, None])).astype(np.float32)
    ovl_t[:, ncp - 1] = 0.0
    ovl_t = jnp.asarray(ovl_t, MXU_DTYPE)

    ocmp_t, sel_t, counts = _nsa_cmp_select(proj, q_feat, kcmp, vcmp_t, ovl_t)
    npair = t // KBLK
    picked = counts[:, :, :, 0, :] > 0.5
    jj = jnp.arange(npair)
    tile_i = jnp.arange(nt)[:, None]
    active = (picked[..., 0::2] | picked[..., 1::2]) & (jj[None, :] < tile_i)
    blk_count = 1 + jnp.sum(active, axis=-1).astype(jnp.int32)
    slot = jnp.cumsum(active, axis=-1)
    hit = active[..., None, :] & (slot[..., None, :] == jj[:, None])
    blk_list = jnp.sum(jnp.where(hit, jj, 0), axis=-1)
    blk_list = jnp.where(jj == 0, tile_i, blk_list)
    blk_list = jnp.where(jj < blk_count[..., None], blk_list, -1).astype(jnp.int32)

    blk_list = jnp.pad(blk_list, ((0, 0), (0, 0), (0, 0), (0, _list_stride(npair) - npair)),
                       constant_values=-1).reshape(-1)
    return _nsa_attend(blk_list, blk_count.reshape(-1), proj, proj_f, q_feat, pos_feat,
                       sel_t, ocmp_t, bsz)


def _log_sigmoid(z):
    return jnp.minimum(z, 0.0) - jnp.log1p(jnp.exp(-jnp.abs(z)))


def _gla_kernel(q_ref, k_ref, v_ref, g_ref, al_ref, w2_ref, b2_ref, hn_ref, o_ref, st_sc):
    @pl.when(pl.program_id(0) == 0)
    def _():
        st_sc[...] = jnp.zeros(st_sc.shape, F32)

    cc = GLA_CHUNK
    r_i = lax.broadcasted_iota(jnp.int32, (cc, cc), 0)
    c_i = lax.broadcasted_iota(jnp.int32, (cc, cc), 1)
    causal = r_i >= c_i
    tri = jnp.where(causal, 1.0, 0.0).astype(MXU_DTYPE)
    ref = cc // 2

    def chunk(ci, carry):
        rows = pl.ds(pl.multiple_of(ci * cc, cc), cc)
        for b in range(q_ref.shape[0]):
            z = _dot(_mx(al_ref[b, rows, :]), w2_ref[...]) + b2_ref[...]
            log_a = _log_sigmoid(z) / GLA_GATE_NORM
            a1 = _mx(log_a)
            r1 = log_a - a1.astype(F32)
            a2 = _mx(r1)
            a3 = _mx(r1 - a2.astype(F32))
            bc_all = _dot(tri, a1) + _dot(tri, a2) + _dot(tri, a3)
            for h in range(GLA_HEADS):
                ks = slice(h * GLA_DK, (h + 1) * GLA_DK)
                vs = slice(h * GLA_DV, (h + 1) * GLA_DV)
                bc = bc_all[:, ks]
                q = q_ref[b, rows, ks] * GLA_DK ** -0.5
                k = k_ref[b, rows, ks]
                v = v_ref[b, rows, vs]
                b_ref = bc[ref:ref + 1, :]
                b_last = bc[cc - 1:cc, :]
                a = _dot_nt(_mx(q * jnp.exp(bc - b_ref)), _mx(k * jnp.exp(b_ref - bc)))
                a = jnp.where(causal, a, 0.0)
                st = st_sc[b * GLA_HEADS + h]
                o = _dot(_mx(a), _mx(v)) + _dot_nt(_mx(q * jnp.exp(bc)), _mx(st))
                kl = k * jnp.exp(b_last - bc)
                st_sc[b * GLA_HEADS + h] = st * jnp.exp(b_last) + _dot(_mx(v.T), _mx(kl))
                o = o * lax.rsqrt(jnp.mean(o * o, -1, keepdims=True) + RMS_EPS)
                gg = g_ref[b, rows, vs]
                o_ref[b, rows, vs] = (o * hn_ref[:, vs] * (gg * jax.nn.sigmoid(gg))).astype(o_ref.dtype)
        return carry

    lax.fori_loop(0, q_ref.shape[1] // cc, chunk, 0)


def _gla_mixer(x2, bsz, t, w_in, w_gate2, b_gate2, head_norm_g):
    n_main = 2 * GLA_KEY_DIM + 2 * GLA_VAL_DIM
    w_all = jnp.pad(_mx(w_in), ((0, 0), (0, LANE - GLA_GATE_RANK)))
    n_all = w_all.shape[1]
    proj = _matmul(x2, w_all, 512, n_all).reshape(bsz, t, n_all)
    w2p = jnp.pad(_mx(w_gate2), ((0, LANE - GLA_GATE_RANK), (0, 0)))
    ts = min(GLA_TSTEP, t)
    out = pl.pallas_call(
        _gla_kernel,
        grid=(t // ts,),
        in_specs=[pl.BlockSpec((bsz, ts, GLA_KEY_DIM), lambda s: (0, s, 0)),
                  pl.BlockSpec((bsz, ts, GLA_KEY_DIM), lambda s: (0, s, 1)),
                  pl.BlockSpec((bsz, ts, GLA_VAL_DIM), lambda s: (0, s, 1)),
                  pl.BlockSpec((bsz, ts, GLA_VAL_DIM), lambda s: (0, s, 2)),
                  pl.BlockSpec((bsz, ts, LANE), lambda s: (0, s, n_main // LANE)),
                  pl.BlockSpec((LANE, GLA_KEY_DIM), lambda s: (0, 0)),
                  pl.BlockSpec((1, GLA_KEY_DIM), lambda s: (0, 0)),
                  pl.BlockSpec((1, GLA_VAL_DIM), lambda s: (0, 0))],
        out_specs=pl.BlockSpec((bsz, ts, GLA_VAL_DIM), lambda s: (0, s, 0)),
        out_shape=jax.ShapeDtypeStruct((bsz, t, GLA_VAL_DIM), MXU_DTYPE),
        scratch_shapes=[pltpu.VMEM((bsz * GLA_HEADS, GLA_DV, GLA_DK), F32)],
        compiler_params=_params("arbitrary"),
        name="gla",
    )(proj, proj, proj, proj, proj, w2p, b_gate2.reshape(1, -1), head_norm_g.reshape(1, -1))
    return out.reshape(bsz * t, GLA_VAL_DIM)


def _router_kernel(x_ref, wh_ref, wl_ref, o_ref):
    x = x_ref[...]
    xh = _mx(x)
    xl = _mx(x - xh.astype(F32))
    wh = wh_ref[...]
    logits = _dot(xh, wh) + _dot(xl, wh) + _dot(xh, wl_ref[...])
    lane = lax.broadcasted_iota(jnp.int32, logits.shape, 1)
    lg = jnp.where(lane < N_EXPERTS, logits, -jnp.inf)
    m1 = jnp.max(lg, axis=1, keepdims=True)
    i1 = jnp.min(jnp.where(lg == m1, lane, LANE), axis=1, keepdims=True)
    lg2 = jnp.where(lane == i1, -jnp.inf, lg)
    m2 = jnp.max(lg2, axis=1, keepdims=True)
    i2 = jnp.min(jnp.where(lg2 == m2, lane, LANE), axis=1, keepdims=True)
    e2 = jnp.exp(m2 - m1)
    den = 1.0 + e2
    w1 = 1.0 / den
    w2 = e2 / den
    out = jnp.where(lane == N_EXPERTS, i1.astype(F32), 0.0)
    out = jnp.where(lane == N_EXPERTS + 1, i2.astype(F32), out)
    out = jnp.where(lane == N_EXPERTS + 2, w1, out)
    out = jnp.where(lane == N_EXPERTS + 3, w2, out)
    o_ref[...] = out


def _moe_up_kernel(te_ref, x_ref, wg_ref, wu_ref, o_ref, wg_sc, wu_sc):
    i = pl.program_id(1)

    @pl.when((i == 0) | (te_ref[i] != te_ref[jnp.maximum(i - 1, 0)]))
    def _():
        wg_sc[...] = _mx(wg_ref[0])
        wu_sc[...] = _mx(wu_ref[0])

    _swiglu_cols(_mx(x_ref[...]), wg_sc, wu_sc, o_ref)


def _moe_down_kernel(te_ref, h_ref, w_ref, o_ref, w_sc):
    i = pl.program_id(0)

    @pl.when((i == 0) | (te_ref[i] != te_ref[jnp.maximum(i - 1, 0)]))
    def _():
        w_sc[...] = _mx(w_ref[0])

    h = h_ref[...]
    for cs in _col_chunks(o_ref.shape[1]):
        o_ref[:, cs] = _dot(h, w_sc[:, cs])


def _add_ln_kernel(x_ref, rt_ref, y0_ref, y1_ref, g_ref, b_ref, o_ref):
    rt = rt_ref[...]
    w0 = rt[:, N_EXPERTS + 2:N_EXPERTS + 3]
    w1 = rt[:, N_EXPERTS + 3:N_EXPERTS + 4]
    h = DN_ALPHA * x_ref[...] + (w0 * y0_ref[...] + w1 * y1_ref[...])
    o_ref[...] = _layer_norm_rows(h, g_ref[...], b_ref[...])


def _moe_layer(x2, w_router, w_gu, w_down, ln_g, ln_b):
    m, d = x2.shape
    tm = MOE_TM
    wr = jnp.pad(w_router, ((0, 0), (0, LANE - N_EXPERTS)))
    wr_hi = _mx(wr)
    wr_lo = _mx(wr - wr_hi.astype(F32))
    rt = pl.pallas_call(
        _router_kernel,
        grid=(m // 512,),
        in_specs=[pl.BlockSpec((512, d), lambda i: (i, 0)),
                  pl.BlockSpec((d, LANE), lambda i: (0, 0)),
                  pl.BlockSpec((d, LANE), lambda i: (0, 0))],
        out_specs=pl.BlockSpec((512, LANE), lambda i: (i, 0)),
        out_shape=jax.ShapeDtypeStruct((m, LANE), F32),
        compiler_params=_params("parallel"),
        name="moe_router",
    )(x2, wr_hi, wr_lo)
    top_idx = rt[:, N_EXPERTS:N_EXPERTS + 2].astype(jnp.int32)

    n_asg = m * TOP_K
    n_rows = n_asg + N_EXPERTS * tm
    n_tiles = n_rows // tm
    e_flat = top_idx.reshape(-1)
    order = jnp.argsort(e_flat, stable=True).astype(jnp.int32)
    slot = jnp.argsort(order).astype(jnp.int32)
    counts = jnp.sum((e_flat[:, None] == jnp.arange(N_EXPERTS)[None, :]).astype(jnp.int32), axis=0)
    tiles_per = (counts + tm - 1) // tm
    tile_end = jnp.cumsum(tiles_per)
    row_start = (tile_end - tiles_per) * tm
    grp_start = jnp.cumsum(counts) - counts
    tile_expert = jnp.minimum(jnp.sum((jnp.arange(n_tiles)[:, None] >= tile_end[None, :]).astype(jnp.int32),
                                      axis=1), N_EXPERTS - 1).astype(jnp.int32)
    pos = (row_start[e_flat] + slot - grp_start[e_flat]).astype(jnp.int32).reshape(m, TOP_K)
    row_e = jnp.repeat(tile_expert, tm)
    row_src = (grp_start - row_start)[row_e].astype(jnp.int32) + jnp.arange(n_rows, dtype=jnp.int32)
    row_token = order[jnp.clip(row_src, 0, n_asg - 1)] // TOP_K

    xs = x2[row_token]
    f = w_gu.shape[2] // 2
    tf = MOE_TF
    nf = f // tf
    h = pl.pallas_call(
        _moe_up_kernel,
        grid_spec=pltpu.PrefetchScalarGridSpec(
            num_scalar_prefetch=1,
            grid=(nf, n_tiles),
            in_specs=[pl.BlockSpec((tm, d), lambda j, i, te: (i, 0)),
                      pl.BlockSpec((1, d, tf), lambda j, i, te: (te[i], 0, j)),
                      pl.BlockSpec((1, d, tf), lambda j, i, te: (te[i], 0, j + nf))],
            out_specs=pl.BlockSpec((tm, tf), lambda j, i, te: (i, j)),
            scratch_shapes=[pltpu.VMEM((d, tf), MXU_DTYPE), pltpu.VMEM((d, tf), MXU_DTYPE)],
        ),
        out_shape=jax.ShapeDtypeStruct((n_rows, f), MXU_DTYPE),
        compiler_params=_params("arbitrary", "arbitrary"),
        name="moe_up",
    )(tile_expert, xs, w_gu, w_gu)
    ys = pl.pallas_call(
        _moe_down_kernel,
        grid_spec=pltpu.PrefetchScalarGridSpec(
            num_scalar_prefetch=1,
            grid=(n_tiles,),
            in_specs=[pl.BlockSpec((tm, f), lambda i, te: (i, 0)),
                      pl.BlockSpec((1, f, d), lambda i, te: (te[i], 0, 0))],
            out_specs=pl.BlockSpec((tm, d), lambda i, te: (i, 0)),
            scratch_shapes=[pltpu.VMEM((f, d), MXU_DTYPE)],
        ),
        out_shape=jax.ShapeDtypeStruct((n_rows, d), F32),
        compiler_params=_params("arbitrary"),
        name="moe_down",
    )(tile_expert, h, w_down)

    y0 = ys[pos[:, 0]]
    y1 = ys[pos[:, 1]]
    return pl.pallas_call(
        _add_ln_kernel,
        grid=(m // 512,),
        in_specs=[pl.BlockSpec((512, d), lambda i: (i, 0)), pl.BlockSpec((512, LANE), lambda i: (i, 0)),
                  pl.BlockSpec((512, d), lambda i: (i, 0)), pl.BlockSpec((512, d), lambda i: (i, 0))]
        + [pl.BlockSpec((1, d), lambda i: (0, 0))] * 2,
        out_specs=pl.BlockSpec((512, d), lambda i: (i, 0)),
        out_shape=jax.ShapeDtypeStruct((m, d), F32),
        compiler_params=_params("parallel"),
        name="moe_add_ln",
    )(x2, rt, y0, y1, ln_g.reshape(1, d), ln_b.reshape(1, d))


def kernel(x, l0_w_in, l0_cmp_pe_k, l0_cmp_pe_v, l0_cmp_wk1, l0_cmp_wk2, l0_cmp_wv1, l0_cmp_wv2, l0_w_o, l0_ln1_g, l0_ln1_b, l0_ffn_w_gu, l0_ffn_w_down, l0_ln2_g, l0_ln2_b, l1_w_in, l1_w_gate2, l1_b_gate2, l1_head_norm_g, l1_w_o, l1_ln1_g, l1_ln1_b, l1_router, l1_moe_w_gu, l1_moe_w_down, l1_ln2_g, l1_ln2_b):
    bsz, t, d = x.shape
    x2 = x.reshape(bsz * t, d)

    o = _nsa_mixer(x2, bsz, t, l0_w_in, l0_cmp_pe_k, l0_cmp_pe_v, l0_cmp_wk1, l0_cmp_wk2,
                   l0_cmp_wv1, l0_cmp_wv2)
    x2 = _matmul_res_ln(o, _mx(l0_w_o), x2, l0_ln1_g, l0_ln1_b, 512)
    hmid = _swiglu_up(x2, _mx(l0_ffn_w_gu), 512, FFN_DENSE)
    x2 = _matmul_res_ln(hmid, _mx(l0_ffn_w_down), x2, l0_ln2_g, l0_ln2_b, 512)

    o = _gla_mixer(x2, bsz, t, l1_w_in, l1_w_gate2, l1_b_gate2, l1_head_norm_g)
    x2 = _matmul_res_ln(o, _mx(l1_w_o), x2, l1_ln1_g, l1_ln1_b, 512)
    x2 = _moe_layer(x2, l1_router, l1_moe_w_gu, l1_moe_w_down, l1_ln2_g, l1_ln2_b)
    return x2.reshape(bsz, t, d)
```

```python
import functools

import numpy as np
import jax
import jax.numpy as jnp
from jax import lax
from jax.experimental import pallas as pl
from jax.experimental.pallas import tpu as pltpu

F32 = jnp.float32
MXU_DTYPE = jnp.bfloat16

D_MODEL = 1024
DEPTH = 2
DN_ALPHA = (2 * DEPTH) ** 0.25
LN_EPS = 1e-5
RMS_EPS = 1e-6

NSA_HEADS = 16
NSA_HEAD_DIM = 64
NSA_KV_GROUPS = 4
NSA_HPG = NSA_HEADS // NSA_KV_GROUPS
NSA_KV_DIM = NSA_KV_GROUPS * NSA_HEAD_DIM
CMP_LEN = 32
CMP_STRIDE = 16
CMP_HIDDEN = 256
SLC_LEN = 64
N_SEL = 16
WINDOW = 512
QTILE = 128
KBLK = 128
NCOL = NSA_HPG * QTILE
CMP_CHUNK = 128
SEL_UNROLL = 6
VT_ROWS = 80
LOG2E = 1.4426950408889634

GLA_HEADS = 4
GLA_KEY_DIM = D_MODEL // 2
GLA_VAL_DIM = D_MODEL
GLA_DK = GLA_KEY_DIM // GLA_HEADS
GLA_DV = GLA_VAL_DIM // GLA_HEADS
GLA_GATE_RANK = 16
GLA_GATE_NORM = 16.0
GLA_CHUNK = 64
GLA_TSTEP = 256

FFN_DENSE = 2816
N_EXPERTS = 8
TOP_K = 2
FFN_EXPERT = 3584
MOE_TM = 512
MOE_TF = 1792
SWIGLU_CHUNK = 256
MM_CHUNK = 512

LANE = 128
NEG = -1e30
VMEM_LIMIT = 56 * 1024 * 1024


def _params(*sem):
    return pltpu.CompilerParams(dimension_semantics=sem, vmem_limit_bytes=VMEM_LIMIT)


def _mx(a):
    return a.astype(MXU_DTYPE)


def _dot(a, b):
    return jnp.dot(a, b, preferred_element_type=F32)


def _dot_nt(a, b):
    return lax.dot_general(a, b, (((1,), (1,)), ((), ())), preferred_element_type=F32)


def _col_chunks(n):
    step = next((s for s in (MM_CHUNK, MM_CHUNK + LANE, MM_CHUNK - LANE) if n % s == 0), n)
    return [slice(j, j + step) for j in range(0, n, step)]


def _mm_kernel(x_ref, w_ref, o_ref):
    x = _mx(x_ref[...])
    for cs in _col_chunks(o_ref.shape[1]):
        o_ref[:, cs] = _dot(x, w_ref[:, cs]).astype(o_ref.dtype)


def _matmul(x, w, tm, tn, out_dtype=F32):
    m, k = x.shape
    n = w.shape[1]
    return pl.pallas_call(
        _mm_kernel,
        grid=(m // tm, n // tn),
        in_specs=[pl.BlockSpec((tm, k), lambda i, j: (i, 0)),
                  pl.BlockSpec((k, tn), lambda i, j: (0, j))],
        out_specs=pl.BlockSpec((tm, tn), lambda i, j: (i, j)),
        out_shape=jax.ShapeDtypeStruct((m, n), out_dtype),
        compiler_params=_params("parallel", "parallel"),
        name="matmul",
    )(x, w)


def _mm_scaled_kernel(x_ref, w_ref, s_ref, o_ref):
    x = _mx(x_ref[...])
    for cs in _col_chunks(o_ref.shape[1]):
        o_ref[:, cs] = (_dot(x, w_ref[:, cs]) * s_ref[:, cs]).astype(o_ref.dtype)


def _matmul_scaled(x, w, col_scale, tm, tn, out_dtype):
    m, k = x.shape
    n = w.shape[1]
    return pl.pallas_call(
        _mm_scaled_kernel,
        grid=(m // tm, n // tn),
        in_specs=[pl.BlockSpec((tm, k), lambda i, j: (i, 0)),
                  pl.BlockSpec((k, tn), lambda i, j: (0, j)),
                  pl.BlockSpec((1, tn), lambda i, j: (0, j))],
        out_specs=pl.BlockSpec((tm, tn), lambda i, j: (i, j)),
        out_shape=jax.ShapeDtypeStruct((m, n), out_dtype),
        compiler_params=_params("parallel", "parallel"),
        name="matmul_scaled",
    )(x, w, col_scale)


def _layer_norm_rows(h, g, b):
    mu = jnp.mean(h, -1, keepdims=True)
    d = h - mu
    var = jnp.mean(d * d, -1, keepdims=True)
    return d * lax.rsqrt(var + LN_EPS) * g + b


def _mm_ln_kernel(x_ref, w_ref, r_ref, g_ref, b_ref, o_ref):
    y = _dot(_mx(x_ref[...]), w_ref[...])
    o_ref[...] = _layer_norm_rows(DN_ALPHA * r_ref[...] + y, g_ref[...], b_ref[...])


def _matmul_res_ln(x, w, res, g, b, tm):
    m, k = x.shape
    n = w.shape[1]
    return pl.pallas_call(
        _mm_ln_kernel,
        grid=(m // tm,),
        in_specs=[pl.BlockSpec((tm, k), lambda i: (i, 0)),
                  pl.BlockSpec((k, n), lambda i: (0, 0)),
                  pl.BlockSpec((tm, n), lambda i: (i, 0)),
                  pl.BlockSpec((1, n), lambda i: (0, 0)),
                  pl.BlockSpec((1, n), lambda i: (0, 0))],
        out_specs=pl.BlockSpec((tm, n), lambda i: (i, 0)),
        out_shape=jax.ShapeDtypeStruct((m, n), F32),
        compiler_params=_params("parallel"),
        name="matmul_res_ln",
    )(x, w, res, g.reshape(1, n), b.reshape(1, n))


def _swiglu_cols(x, wg_ref, wu_ref, o_ref):
    n = o_ref.shape[1]
    step = SWIGLU_CHUNK if n % SWIGLU_CHUNK == 0 else LANE
    for j in range(0, n, step):
        a = _dot(x, wg_ref[:, j:j + step])
        b = _dot(x, wu_ref[:, j:j + step])
        o_ref[:, j:j + step] = (a * jax.nn.sigmoid(a) * b).astype(o_ref.dtype)


def _swiglu_up_kernel(x_ref, wg_ref, wu_ref, o_ref):
    _swiglu_cols(_mx(x_ref[...]), wg_ref, wu_ref, o_ref)


def _swiglu_up(x, w_gu, tm, tn):
    m, k = x.shape
    f = w_gu.shape[1] // 2
    nj = f // tn
    return pl.pallas_call(
        _swiglu_up_kernel,
        grid=(m // tm, nj),
        in_specs=[pl.BlockSpec((tm, k), lambda i, j: (i, 0)),
                  pl.BlockSpec((k, tn), lambda i, j: (0, j)),
                  pl.BlockSpec((k, tn), lambda i, j: (0, j + nj))],
        out_specs=pl.BlockSpec((tm, tn), lambda i, j: (i, j)),
        out_shape=jax.ShapeDtypeStruct((m, f), MXU_DTYPE),
        compiler_params=_params("parallel", "parallel"),
        name="swiglu_up",
    )(x, w_gu, w_gu)


def _cmp_up_kernel(x0_ref, x1_ref, pea_ref, peb_ref, wa_ref, wb_ref, p_ref, q_ref):
    tm = p_ref.shape[0]
    half_out = p_ref.shape[1] // 2
    for hi, x_ref in enumerate((x0_ref, x1_ref)):
        cs = slice(hi * LANE, (hi + 1) * LANE)
        os_ = slice(hi * half_out, (hi + 1) * half_out)
        p = q = None
        for l in range(CMP_STRIDE):
            xl = x_ref[pl.ds(l, tm, stride=CMP_STRIDE), :]
            da = _dot(_mx(xl + pea_ref[l:l + 1, cs]), wa_ref[l, cs, os_])
            db = _dot(_mx(xl + peb_ref[l:l + 1, cs]), wb_ref[l, cs, os_])
            p = da if p is None else p + da
            q = db if q is None else q + db
        p_ref[:, os_] = p
        q_ref[:, os_] = q


def _cmp_down_kernel(p_ref, q_ref, w_ref, o_ref):
    for gi in range(NSA_KV_GROUPS):
        cs = slice(gi * CMP_HIDDEN, (gi + 1) * CMP_HIDDEN)
        h = jax.nn.gelu(p_ref[:, cs] + q_ref[:, cs])
        o_ref[:, gi * LANE:(gi + 1) * LANE] = _dot(_mx(h), w_ref[...])


def _compress(proj_f, col_blk, bsz, pe, w1, w2):
    g, dk = NSA_KV_GROUPS, NSA_HEAD_DIM
    nch_all = proj_f.shape[0] // CMP_STRIDE
    nch = nch_all // bsz
    tm = min(256, nch_all)
    eye = jnp.eye(g, dtype=F32)

    def block_diag(w_half):
        w3 = w_half.reshape(CMP_STRIDE, dk, CMP_HIDDEN)
        return _mx(jnp.einsum("ldc,gh->lgdhc", w3, eye).reshape(CMP_STRIDE, g * dk, g * CMP_HIDDEN))

    half = CMP_STRIDE * dk
    pe_a = jnp.tile(pe[:CMP_STRIDE], (1, g))
    pe_b = jnp.tile(pe[CMP_STRIDE:], (1, g))
    wide = g * CMP_HIDDEN
    p, q = pl.pallas_call(
        _cmp_up_kernel,
        grid=(nch_all // tm,),
        in_specs=[pl.BlockSpec((tm * CMP_STRIDE, LANE), lambda i: (i, 2 * col_blk)),
                  pl.BlockSpec((tm * CMP_STRIDE, LANE), lambda i: (i, 2 * col_blk + 1)),
                  pl.BlockSpec((CMP_STRIDE, g * dk), lambda i: (0, 0)),
                  pl.BlockSpec((CMP_STRIDE, g * dk), lambda i: (0, 0)),
                  pl.BlockSpec((CMP_STRIDE, g * dk, wide), lambda i: (0, 0, 0)),
                  pl.BlockSpec((CMP_STRIDE, g * dk, wide), lambda i: (0, 0, 0))],
        out_specs=[pl.BlockSpec((tm, wide), lambda i: (i, 0)),
                   pl.BlockSpec((tm, wide), lambda i: (i, 0))],
        out_shape=[jax.ShapeDtypeStruct((nch_all, wide), F32)] * 2,
        compiler_params=_params("parallel"),
        name="cmp_up",
    )(proj_f, proj_f, pe_a, pe_b, block_diag(w1[:half]), block_diag(w1[half:]))
    q = q.reshape(bsz, nch, wide)
    q = jnp.concatenate([q[:, 1:], jnp.zeros_like(q[:, :1])], axis=1).reshape(nch_all, wide)
    w2p = jnp.pad(_mx(w2), ((0, 0), (0, LANE - dk)))
    out = pl.pallas_call(
        _cmp_down_kernel,
        grid=(nch_all // tm,),
        in_specs=[pl.BlockSpec((tm, wide), lambda i: (i, 0)),
                  pl.BlockSpec((tm, wide), lambda i: (i, 0)),
                  pl.BlockSpec((CMP_HIDDEN, LANE), lambda i: (0, 0))],
        out_specs=pl.BlockSpec((tm, g * LANE), lambda i: (i, 0)),
        out_shape=jax.ShapeDtypeStruct((nch_all, g * LANE), F32),
        compiler_params=_params("parallel"),
        name="cmp_down",
    )(p, q, w2p)
    out = out.reshape(bsz, nch, g, LANE)[..., :dk]
    valid = (jnp.arange(nch) < nch - 1)[None, :, None, None]
    return jnp.where(valid, out, 0.0)


def _queries_t(q_blk):
    qt = q_blk.astype(F32).T
    dk = NSA_HEAD_DIM
    return _mx(jnp.concatenate([qt[h * dk:(h + 1) * dk, :] for h in range(NSA_HPG)], axis=1))


def _nsa_cmp_kernel(q_ref, qf_ref, kc_ref, vct_ref, ovl_ref, ocmp_ref, sel_ref, flag_ref,
                    imp_ref, key_ref, cnt_ref):
    c = pl.program_id(2)
    ncp = kc_ref.shape[2]
    ns = ovl_ref.shape[0]
    dk = NSA_HEAD_DIM
    rhs = jnp.concatenate([_queries_t(q_ref[...]), qf_ref[0]], axis=0)
    chunk = min(CMP_CHUNK, ncp)
    n_chunks = (c * (QTILE // CMP_STRIDE) + (QTILE - CMP_LEN) // CMP_STRIDE + chunk) // chunk

    def cmp_branch(rows):
        n_i = lax.broadcasted_iota(jnp.int32, (rows, QTILE), 0)
        tq = c * QTILE + lax.broadcasted_iota(jnp.int32, (rows, QTILE), 1)
        bias = _mx(jnp.where(n_i * CMP_STRIDE + (CMP_LEN - 1) <= tq, 0.0, NEG))
        s = _dot(jnp.concatenate([kc_ref[0, 0, :rows, :], bias], axis=1), rhs)
        e = jnp.exp2(s - jnp.max(s, axis=0, keepdims=True))
        seen = jnp.concatenate([tq[0:1, :]] * NSA_HPG, axis=1) >= CMP_LEN - 1
        r = jnp.where(seen, 1.0 / jnp.maximum(jnp.sum(e, axis=0, keepdims=True), 1e-30), 0.0)
        ocmp_ref[0, 0, 0] = _dot(vct_ref[0, 0, :, :rows], _mx(e)) * r
        psum = e[:, 0:QTILE] * r[:, 0:QTILE]
        for h in range(1, NSA_HPG):
            psum = psum + e[:, h * QTILE:(h + 1) * QTILE] * r[:, h * QTILE:(h + 1) * QTILE]
        hi = _mx(psum)
        lo = _mx(psum - hi.astype(F32))
        ovl = ovl_ref[:, :rows]
        imp_ref[...] = _dot(ovl, hi) + _dot(ovl, lo)

    for k in range(1, ncp // chunk + 1):
        pl.when(n_chunks == k)(functools.partial(cmp_branch, k * chunk))

    j_i = lax.broadcasted_iota(jnp.int32, (ns, QTILE), 0)
    tq2 = c * QTILE + lax.broadcasted_iota(jnp.int32, (ns, QTILE), 1)
    valid = j_i * SLC_LEN <= tq2
    cur = tq2 >> (SLC_LEN.bit_length() - 1)
    forced = valid & ((j_i == 0) | (j_i == cur) | (j_i == cur - 1))
    score = jnp.where(forced, 1e9, jnp.where(valid, imp_ref[...], -1e9))
    bits = lax.bitcast_convert_type(score, jnp.int32)
    key_ref[...] = jnp.where(bits < 0, bits ^ 0x7FFFFFFF, bits)
    cnt_ref[...] = jnp.zeros(cnt_ref.shape, jnp.int32)

    sub = lax.broadcasted_iota(jnp.int32, (8, QTILE), 0)
    n_it = (c * (QTILE // SLC_LEN) + QTILE // SLC_LEN + 7) // 8

    def rank_blocks(n_jt):
        def it_body(it, carry):
            base = pl.multiple_of(it * 8, 8)
            rows8 = key_ref[pl.ds(base, 8), :]
            rb = [jnp.broadcast_to(rows8[il:il + 1, :], (8, QTILE)) for il in range(8)]
            for jt in range(n_jt):
                js = slice(jt * 8, (jt + 1) * 8)
                thr = key_ref[js, :] - jnp.where(jt > it, 1, 0)
                acc = cnt_ref[js, :]
                for il in range(8):
                    acc = acc + jnp.where(rb[il] > thr, 1, 0)
                cnt_ref[js, :] = acc
            corr = jnp.zeros((8, QTILE), jnp.int32)
            for il in range(8):
                corr = corr + jnp.where((rb[il] == rows8) & (sub > il), 1, 0)
            cnt_ref[pl.ds(base, 8), :] += corr
            return carry

        lax.fori_loop(0, n_it, it_body, 0)

    sizes = sorted({min(ns // 8, max(1, ns // 32) * k) for k in range(1, 5)})
    for lo, hi in zip([0] + sizes[:-1], sizes):
        pl.when((n_it > lo) & (n_it <= hi))(functools.partial(rank_blocks, hi))
    sel = jnp.where((cnt_ref[...] < min(N_SEL, ns)) & valid, 1.0, 0.0)
    sel_ref[0, 0, 0] = sel
    flag_ref[0, 0, 0] = _dot_nt(jnp.ones((8, QTILE), MXU_DTYPE), _mx(sel))


def _nsa_cmp_select(proj, q_feat, kcmp, vcmp_t, ovl_t):
    bsz, g, ncp, _ = kcmp.shape
    dk = NSA_HEAD_DIM
    nt = proj.shape[0] // (bsz * QTILE)
    ns = ovl_t.shape[0]
    return pl.pallas_call(
        _nsa_cmp_kernel,
        grid=(bsz, g, nt),
        in_specs=[pl.BlockSpec((QTILE, NSA_HPG * dk), lambda b, gg, c: (b * nt + c, gg)),
                  pl.BlockSpec((1, dk + QTILE, NCOL), lambda b, gg, c: (gg, 0, 0)),
                  pl.BlockSpec((1, 1, ncp, LANE), lambda b, gg, c: (b, gg, 0, 0)),
                  pl.BlockSpec((1, 1, dk, ncp), lambda b, gg, c: (b, gg, 0, 0)),
                  pl.BlockSpec((ns, ncp), lambda b, gg, c: (0, 0))],
        out_specs=[pl.BlockSpec((1, 1, 1, dk, NCOL), lambda b, gg, c: (b, gg, c, 0, 0)),
                   pl.BlockSpec((1, 1, 1, ns, QTILE), lambda b, gg, c: (b, gg, c, 0, 0)),
                   pl.BlockSpec((1, 1, 1, 8, ns), lambda b, gg, c: (b, gg, c, 0, 0))],
        out_shape=[jax.ShapeDtypeStruct((bsz, g, nt, dk, NCOL), F32),
                   jax.ShapeDtypeStruct((bsz, g, nt, ns, QTILE), F32),
                   jax.ShapeDtypeStruct((bsz, g, nt, 8, ns), F32)],
        scratch_shapes=[pltpu.VMEM((ns, QTILE), F32), pltpu.VMEM((ns, QTILE), jnp.int32),
                        pltpu.VMEM((ns, QTILE), jnp.int32)],
        compiler_params=_params("parallel", "parallel", "parallel"),
        name="nsa_cmp_select",
    )(proj, q_feat, kcmp, vcmp_t, ovl_t)


def _list_stride(npair):
    trip = 2 * SEL_UNROLL
    return -(-npair // trip) * trip


def _values_t(v_blk):
    vt = v_blk.astype(F32).T[:VT_ROWS, :]
    ones_row = lax.broadcasted_iota(jnp.int32, vt.shape, 0) == NSA_HEAD_DIM
    return _mx(jnp.where(ones_row, 1.0, vt))


def _nsa_attn_kernel(lst_ref, cnt_ref, q_ref, qf_ref, ks_ref, vs_ref, kw_ref, vw_ref, pos_ref, sel_ref,
                     ocmp_ref, gl_ref, o_ref, ksa_sc, kwa_sc, vst_sc, vwt_sc, gate_sc, ow_sc, m_sc, acc_sc):
    b = pl.program_id(0)
    g = pl.program_id(1)
    c = pl.program_id(2)
    nt = pl.num_programs(2)
    npair = ks_ref.shape[1] // KBLK
    dk = NSA_HEAD_DIM

    @pl.when(c == 0)
    def _():
        def prep(j, carry):
            rows = pl.ds(pl.multiple_of(j * KBLK, KBLK), KBLK)
            feats = pos_ref[rows, :]
            ksa_sc[rows, :] = ks_ref[0, rows, :] + feats
            kwa_sc[rows, :] = kw_ref[0, rows, :] + feats
            vst_sc[j] = _values_t(vs_ref[0, rows, :])
            vwt_sc[j] = _values_t(vw_ref[0, rows, :])
            return carry

        lax.fori_loop(0, npair, prep, 0)

    rhs = jnp.concatenate([_queries_t(q_ref[...]), qf_ref[0]], axis=0)
    key_i = lax.broadcasted_iota(jnp.int32, (KBLK, QTILE), 0)
    q_i = lax.broadcasted_iota(jnp.int32, (KBLK, QTILE), 1)
    heads = [slice(h * QTILE, (h + 1) * QTILE) for h in range(NSA_HPG)]

    m_sc[...] = jnp.full(m_sc.shape, NEG, F32)
    acc_sc[...] = jnp.zeros(acc_sc.shape, F32)
    step = (b * NSA_KV_GROUPS + g) * nt + c

    def group_blocks(gi):
        return [lst_ref[step * _list_stride(npair) + gi * SEL_UNROLL + u] for u in range(SEL_UNROLL)]

    def group_scores(gi, own_first):
        scores = []
        for u, jj in enumerate(group_blocks(gi)):
            js = jnp.maximum(jj, 0)
            rows = sel_ref[0, 0, 0, pl.ds(2 * js, 2), :]
            if own_first and u == 0:
                picked = jnp.where(key_i < SLC_LEN, rows[0:1, :], rows[1:2, :]) > 0.5
                bias = _mx(jnp.where(picked & (key_i <= q_i), 0.0, NEG))
            else:
                half = _mx(jnp.where(jj >= 0, (rows - 1.0) * -NEG, NEG))
                bias = jnp.concatenate([jnp.broadcast_to(half[0:1, :], (SLC_LEN, QTILE)),
                                        jnp.broadcast_to(half[1:2, :], (SLC_LEN, QTILE))], axis=0)
            keys = ksa_sc[pl.ds(pl.multiple_of(js * KBLK, KBLK), KBLK), :]
            scores.append(_dot(jnp.concatenate([keys, bias], axis=1), rhs))
        return scores

    def softmax_update(gi, scores):
        blocks = [jnp.maximum(jj, 0) for jj in group_blocks(gi)]
        m_new = []
        for cs in heads:
            m_old = m_sc[:, cs]
            mx_ = m_old
            for u in range(SEL_UNROLL):
                mx_ = jnp.maximum(mx_, jnp.max(scores[u][:, cs], axis=0, keepdims=True))
            acc_sc[:, cs] = jnp.exp2(m_old - mx_) * acc_sc[:, cs]
            m_sc[:, cs] = mx_
            m_new.append(mx_)
        upd = None
        for u in range(SEL_UNROLL):
            e = jnp.concatenate([_mx(jnp.exp2(scores[u][:, cs] - m_new[hi])) for hi, cs in enumerate(heads)],
                                axis=1)
            d = _dot(vst_sc[blocks[u]], e)
            upd = d if upd is None else upd + d
        acc_sc[...] += upd

    scores_first = group_scores(0, True)

    nwin = WINDOW // KBLK + 1
    blk0 = jnp.maximum(c - (nwin - 1), 0)
    row = lax.broadcasted_iota(jnp.int32, (nwin * KBLK, QTILE), 0)
    dist = lax.broadcasted_iota(jnp.int32, (nwin * KBLK, QTILE), 1) + (c - blk0) * KBLK - row
    bias = _mx(jnp.where((dist >= 0) & (dist < WINDOW), 0.0, NEG))
    keys = kwa_sc[pl.ds(pl.multiple_of(blk0 * KBLK, KBLK), nwin * KBLK), :]
    s = _dot(jnp.concatenate([keys, bias], axis=1), rhs)
    e_all = jnp.concatenate([_mx(jnp.exp2(s[:, cs] - jnp.max(s[:, cs], axis=0, keepdims=True)))
                             for cs in heads], axis=1)
    acc = None
    for t in range(nwin):
        d = _dot(vwt_sc[blk0 + t], e_all[t * KBLK:(t + 1) * KBLK, :])
        acc = d if acc is None else acc + d
    ow_sc[...] = acc[:dk, :] * (1.0 / jnp.maximum(acc[dk:dk + 1, :], 1e-30))

    def trip(r, carry, scores_a=None):
        scores_a = group_scores(2 * r, False) if scores_a is None else scores_a
        scores_b = group_scores(2 * r + 1, False)
        softmax_update(2 * r, scores_a)
        softmax_update(2 * r + 1, scores_b)
        return carry

    trip(0, 0, scores_first)
    lax.fori_loop(1, (cnt_ref[step] + 2 * SEL_UNROLL - 1) // (2 * SEL_UNROLL), trip, 0)
    o_sel = acc_sc[:dk, :] * (1.0 / jnp.maximum(acc_sc[dk:dk + 1, :], 1e-30))
    o_win = ow_sc[...]

    gate_sc[...] = gl_ref[...].T
    ocmp = ocmp_ref[0, 0, 0]
    outs = []
    for h, cs in enumerate(heads):
        gt = jax.nn.sigmoid(gate_sc[pl.ds((g * NSA_HPG + h) * 3, 3), :])
        outs.append(gt[0:1] * ocmp[:, cs] + gt[1:2] * o_sel[:, cs] + gt[2:3] * o_win[:, cs])
    o_ref[...] = jnp.concatenate(outs, axis=0).T.astype(o_ref.dtype)


def _nsa_attend(blk_list, blk_count, proj, proj_f, q_feat, pos_feat, sel_t, ocmp_t, bsz):
    g, dk = NSA_KV_GROUPS, NSA_HEAD_DIM
    t = proj.shape[0] // bsz
    nt = t // QTILE
    npair = t // KBLK
    ns = sel_t.shape[3]
    proj3 = proj.reshape(bsz, t, proj.shape[1])
    seg = D_MODEL // LANE
    gate_blk = proj_f.shape[1] // LANE - 1
    tile = lambda b, gg, c, lst, cnt: (b, gg, c, 0, 0)
    rows = lambda b, gg, c, lst, cnt: (b * nt + c, gg)
    kv = lambda k: (lambda b, gg, c, lst, cnt: (b, 0, seg + k * g + gg))
    grid_spec = pltpu.PrefetchScalarGridSpec(
        num_scalar_prefetch=2,
        grid=(bsz, g, nt),
        in_specs=[pl.BlockSpec((QTILE, NSA_HPG * dk), rows),
                  pl.BlockSpec((1, dk + QTILE, NCOL), lambda b, gg, c, lst, cnt: (gg, 0, 0)),
                  pl.BlockSpec((1, t, LANE), kv(0)),
                  pl.BlockSpec((1, t, LANE), kv(1)),
                  pl.BlockSpec((1, t, LANE), kv(2)),
                  pl.BlockSpec((1, t, LANE), kv(3)),
                  pl.BlockSpec((t, LANE), lambda b, gg, c, lst, cnt: (0, 0)),
                  pl.BlockSpec((1, 1, 1, ns, QTILE), tile),
                  pl.BlockSpec((1, 1, 1, dk, NCOL), tile),
                  pl.BlockSpec((QTILE, LANE), lambda b, gg, c, lst, cnt: (b * nt + c, gate_blk))],
        out_specs=pl.BlockSpec((QTILE, NSA_HPG * dk), rows),
        scratch_shapes=[pltpu.VMEM((t, LANE), MXU_DTYPE), pltpu.VMEM((t, LANE), MXU_DTYPE),
                        pltpu.VMEM((npair, VT_ROWS, KBLK), MXU_DTYPE),
                        pltpu.VMEM((npair, VT_ROWS, KBLK), MXU_DTYPE),
                        pltpu.VMEM((LANE, QTILE), F32),
                        pltpu.VMEM((dk, NCOL), F32),
                        pltpu.VMEM((1, NCOL), F32), pltpu.VMEM((VT_ROWS, NCOL), F32)],
    )
    return pl.pallas_call(
        _nsa_attn_kernel,
        grid_spec=grid_spec,
        out_shape=jax.ShapeDtypeStruct((bsz * t, NSA_HEADS * dk), MXU_DTYPE),
        compiler_params=_params("parallel", "parallel", "arbitrary"),
        name="nsa_attend",
    )(blk_list, blk_count, proj, q_feat, proj3, proj3, proj3, proj3, pos_feat, sel_t, ocmp_t, proj_f)


def _nsa_mixer(x2, bsz, t, w_in, pe_k, pe_v, wk1, wk2, wv1, wv2):
    g, hpg, dk = NSA_KV_GROUPS, NSA_HPG, NSA_HEAD_DIM
    nt = t // QTILE
    d = w_in.shape[0]
    splits = np.cumsum([D_MODEL] + [NSA_KV_DIM] * 6).tolist()
    wq, wkc, wvc, wks, wvs, wkw, wvw, wgl = jnp.split(w_in, splits, axis=1)

    def lane_groups(w):
        return jnp.pad(w.reshape(d, g, dk), ((0, 0), (0, 0), (0, LANE - dk))).reshape(d, g * LANE)

    w_a = _mx(jnp.concatenate([wq] + [lane_groups(w) for w in (wks, wvs, wkw, wvw)], axis=1))
    scale = jnp.concatenate([jnp.full((1, D_MODEL), dk ** -0.5 * LOG2E, F32),
                             jnp.ones((1, w_a.shape[1] - D_MODEL), F32)], axis=1)
    w_b = _mx(jnp.concatenate([wkc, wvc, jnp.pad(wgl, ((0, 0), (0, LANE - wgl.shape[1])))], axis=1))
    proj = _matmul_scaled(x2, w_a, scale, 512, w_a.shape[1], MXU_DTYPE)
    proj_f = _matmul(x2, w_b, 512, w_b.shape[1])

    slopes = 2.0 ** (-8.0 * (jnp.arange(NSA_HEADS, dtype=F32) + 1.0) / NSA_HEADS) * LOG2E
    slopes = jnp.broadcast_to(slopes.reshape(g, 1, hpg, 1), (g, 1, hpg, QTILE)).reshape(g, 1, NCOL)
    s1 = slopes.astype(jnp.bfloat16)
    s2 = (slopes - s1.astype(F32)).astype(jnp.bfloat16)
    s3 = (slopes - s1.astype(F32) - s2.astype(F32)).astype(jnp.bfloat16)
    tail = np.zeros((g, dk - 6 + QTILE, NCOL), np.float32)
    tail[:, dk - 6:, :] = np.tile(np.eye(QTILE, dtype=np.float32), (1, hpg))
    q_feat = jnp.concatenate([s1 * SLC_LEN, s2 * SLC_LEN, s3 * SLC_LEN, s1, s2, s3, jnp.asarray(tail, jnp.bfloat16)],
                             axis=1).astype(MXU_DTYPE)

    def pos_features(kpos):
        feats = np.zeros((kpos.shape[0], LANE - dk), np.float32)
        feats[:, 0:3] = (kpos // SLC_LEN)[:, None]
        feats[:, 3:6] = (kpos % SLC_LEN)[:, None]
        return feats

    pos_feat = jnp.asarray(np.pad(pos_features(np.arange(t)), ((0, 0), (dk, 0))), MXU_DTYPE)

    ncp = t // CMP_STRIDE
    k_cmp = _compress(proj_f, 0, bsz, pe_k, wk1, wk2)
    v_cmp = _compress(proj_f, 1, bsz, pe_v, wv1, wv2)
    cmp_feat = jnp.asarray(pos_features(np.arange(ncp) * CMP_STRIDE + (CMP_LEN - 1)), MXU_DTYPE)
    kcmp = jnp.concatenate([_mx(k_cmp).transpose(0, 2, 1, 3),
                            jnp.broadcast_to(cmp_feat[None, None], (bsz, g, ncp, LANE - dk))], axis=-1)
    vcmp_t = _mx(v_cmp).transpose(0, 2, 3, 1)

    ns = t // SLC_LEN
    cs = np.arange(ncp) * CMP_STRIDE
    ce = cs + CMP_LEN - 1
    bs = np.arange(ns) * SLC_LEN
    be = bs + SLC_LEN - 1
    ovl_t = ((cs[None, :] <= be[:, None]) & (ce[None, :] >= bs[:, None])).astype(np.float32)
    ovl_t[:, ncp - 1] = 0.0
    ovl_t = jnp.asarray(ovl_t, MXU_DTYPE)

    ocmp_t, sel_t, counts = _nsa_cmp_select(proj, q_feat, kcmp, vcmp_t, ovl_t)
    npair = t // KBLK
    picked = counts[:, :, :, 0, :] > 0.5
    jj = jnp.arange(npair)
    tile_i = jnp.arange(nt)[:, None]
    active = (picked[..., 0::2] | picked[..., 1::2]) & (jj[None, :] < tile_i)
    blk_count = 1 + jnp.sum(active, axis=-1).astype(jnp.int32)
    slot = jnp.cumsum(active, axis=-1)
    hit = active[..., None, :] & (slot[..., None, :] == jj[:, None])
    blk_list = jnp.sum(jnp.where(hit, jj, 0), axis=-1)
    blk_list = jnp.where(jj == 0, tile_i, blk_list)
    blk_list = jnp.where(jj < blk_count[..., None], blk_list, -1).astype(jnp.int32)

    blk_list = jnp.pad(blk_list, ((0, 0), (0, 0), (0, 0), (0, _list_stride(npair) - npair)),
                       constant_values=-1).reshape(-1)
    return _nsa_attend(blk_list, blk_count.reshape(-1), proj, proj_f, q_feat, pos_feat,
                       sel_t, ocmp_t, bsz)


def _log_sigmoid(z):
    return jnp.minimum(z, 0.0) - jnp.log1p(jnp.exp(-jnp.abs(z)))


def _gla_kernel(q_ref, k_ref, v_ref, g_ref, al_ref, w2_ref, b2_ref, hn_ref, o_ref, st_sc):
    @pl.when(pl.program_id(0) == 0)
    def _():
        st_sc[...] = jnp.zeros(st_sc.shape, F32)

    cc = GLA_CHUNK
    r_i = lax.broadcasted_iota(jnp.int32, (cc, cc), 0)
    c_i = lax.broadcasted_iota(jnp.int32, (cc, cc), 1)
    causal = r_i >= c_i
    tri = jnp.where(causal, 1.0, 0.0).astype(MXU_DTYPE)
    ref = cc // 2

    def chunk(ci, carry):
        rows = pl.ds(pl.multiple_of(ci * cc, cc), cc)
        for b in range(q_ref.shape[0]):
            z = _dot(_mx(al_ref[b, rows, :]), w2_ref[...]) + b2_ref[...]
            log_a = _log_sigmoid(z) / GLA_GATE_NORM
            a1 = _mx(log_a)
            r1 = log_a - a1.astype(F32)
            a2 = _mx(r1)
            a3 = _mx(r1 - a2.astype(F32))
            bc_all = _dot(tri, a1) + _dot(tri, a2) + _dot(tri, a3)
            for h in range(GLA_HEADS):
                ks = slice(h * GLA_DK, (h + 1) * GLA_DK)
                vs = slice(h * GLA_DV, (h + 1) * GLA_DV)
                bc = bc_all[:, ks]
                q = q_ref[b, rows, ks] * GLA_DK ** -0.5
                k = k_ref[b, rows, ks]
                v = v_ref[b, rows, vs]
                b_ref = bc[ref:ref + 1, :]
                b_last = bc[cc - 1:cc, :]
                a = _dot_nt(_mx(q * jnp.exp(bc - b_ref)), _mx(k * jnp.exp(b_ref - bc)))
                a = jnp.where(causal, a, 0.0)
                st = st_sc[b * GLA_HEADS + h]
                o = _dot(_mx(a), _mx(v)) + _dot_nt(_mx(q * jnp.exp(bc)), _mx(st))
                kl = k * jnp.exp(b_last - bc)
                st_sc[b * GLA_HEADS + h] = st * jnp.exp(b_last) + _dot(_mx(v.T), _mx(kl))
                o = o * lax.rsqrt(jnp.mean(o * o, -1, keepdims=True) + RMS_EPS)
                gg = g_ref[b, rows, vs]
                o_ref[b, rows, vs] = (o * hn_ref[:, vs] * (gg * jax.nn.sigmoid(gg))).astype(o_ref.dtype)
        return carry

    lax.fori_loop(0, q_ref.shape[1] // cc, chunk, 0)


def _gla_mixer(x2, bsz, t, w_in, w_gate2, b_gate2, head_norm_g):
    n_main = 2 * GLA_KEY_DIM + 2 * GLA_VAL_DIM
    w_all = jnp.pad(_mx(w_in), ((0, 0), (0, LANE - GLA_GATE_RANK)))
    n_all = w_all.shape[1]
    proj = _matmul(x2, w_all, 512, n_all).reshape(bsz, t, n_all)
    w2p = jnp.pad(_mx(w_gate2), ((0, LANE - GLA_GATE_RANK), (0, 0)))
    ts = min(GLA_TSTEP, t)
    out = pl.pallas_call(
        _gla_kernel,
        grid=(t // ts,),
        in_specs=[pl.BlockSpec((bsz, ts, GLA_KEY_DIM), lambda s: (0, s, 0)),
                  pl.BlockSpec((bsz, ts, GLA_KEY_DIM), lambda s: (0, s, 1)),
                  pl.BlockSpec((bsz, ts, GLA_VAL_DIM), lambda s: (0, s, 1)),
                  pl.BlockSpec((bsz, ts, GLA_VAL_DIM), lambda s: (0, s, 2)),
                  pl.BlockSpec((bsz, ts, LANE), lambda s: (0, s, n_main // LANE)),
                  pl.BlockSpec((LANE, GLA_KEY_DIM), lambda s: (0, 0)),
                  pl.BlockSpec((1, GLA_KEY_DIM), lambda s: (0, 0)),
                  pl.BlockSpec((1, GLA_VAL_DIM), lambda s: (0, 0))],
        out_specs=pl.BlockSpec((bsz, ts, GLA_VAL_DIM), lambda s: (0, s, 0)),
        out_shape=jax.ShapeDtypeStruct((bsz, t, GLA_VAL_DIM), MXU_DTYPE),
        scratch_shapes=[pltpu.VMEM((bsz * GLA_HEADS, GLA_DV, GLA_DK), F32)],
        compiler_params=_params("arbitrary"),
        name="gla",
    )(proj, proj, proj, proj, proj, w2p, b_gate2.reshape(1, -1), head_norm_g.reshape(1, -1))
    return out.reshape(bsz * t, GLA_VAL_DIM)


def _route_top2(x, wh_ref, wl_ref):
    xh = _mx(x)
    xl = _mx(x - xh.astype(F32))
    wh = wh_ref[...]
    logits = _dot(xh, wh) + _dot(xl, wh) + _dot(xh, wl_ref[...])
    lane = lax.broadcasted_iota(jnp.int32, logits.shape, 1)
    lg = jnp.where(lane < N_EXPERTS, logits, -jnp.inf)
    m1 = jnp.max(lg, axis=1, keepdims=True)
    i1 = jnp.min(jnp.where(lg == m1, lane, LANE), axis=1, keepdims=True)
    lg2 = jnp.where(lane == i1, -jnp.inf, lg)
    m2 = jnp.max(lg2, axis=1, keepdims=True)
    i2 = jnp.min(jnp.where(lg2 == m2, lane, LANE), axis=1, keepdims=True)
    e2 = jnp.exp(m2 - m1)
    den = 1.0 + e2
    w1 = 1.0 / den
    w2 = e2 / den
    out = jnp.where(lane == N_EXPERTS, i1.astype(F32), 0.0)
    out = jnp.where(lane == N_EXPERTS + 1, i2.astype(F32), out)
    out = jnp.where(lane == N_EXPERTS + 2, w1, out)
    return jnp.where(lane == N_EXPERTS + 3, w2, out)


def _mm_ln_route_kernel(x_ref, w_ref, r_ref, g_ref, b_ref, wh_ref, wl_ref, o_ref, rt_ref):
    y = _dot(_mx(x_ref[...]), w_ref[...])
    h = _layer_norm_rows(DN_ALPHA * r_ref[...] + y, g_ref[...], b_ref[...])
    o_ref[...] = h
    rt_ref[...] = _route_top2(h, wh_ref, wl_ref)


def _matmul_res_ln_route(x, w, res, g, b, w_router, tm):
    m, k = x.shape
    n = w.shape[1]
    wr = jnp.pad(w_router, ((0, 0), (0, LANE - N_EXPERTS)))
    wr_hi = _mx(wr)
    wr_lo = _mx(wr - wr_hi.astype(F32))
    row = lambda i: (i, 0)
    fixed = lambda i: (0, 0)
    return pl.pallas_call(
        _mm_ln_route_kernel,
        grid=(m // tm,),
        in_specs=[pl.BlockSpec((tm, k), row), pl.BlockSpec((k, n), fixed), pl.BlockSpec((tm, n), row),
                  pl.BlockSpec((1, n), fixed), pl.BlockSpec((1, n), fixed),
                  pl.BlockSpec((n, LANE), fixed), pl.BlockSpec((n, LANE), fixed)],
        out_specs=[pl.BlockSpec((tm, n), row), pl.BlockSpec((tm, LANE), row)],
        out_shape=[jax.ShapeDtypeStruct((m, n), F32), jax.ShapeDtypeStruct((m, LANE), F32)],
        compiler_params=_params("parallel"),
        name="matmul_res_ln_route",
    )(x, w, res, g.reshape(1, n), b.reshape(1, n), wr_hi, wr_lo)


def _moe_up_kernel(te_ref, x_ref, wg_ref, wu_ref, o_ref, wg_sc, wu_sc):
    i = pl.program_id(1)

    @pl.when((i == 0) | (te_ref[i] != te_ref[jnp.maximum(i - 1, 0)]))
    def _():
        wg_sc[...] = _mx(wg_ref[0])
        wu_sc[...] = _mx(wu_ref[0])

    _swiglu_cols(_mx(x_ref[...]), wg_sc, wu_sc, o_ref)


def _moe_down_kernel(te_ref, h_ref, w_ref, o_ref, w_sc):
    i = pl.program_id(0)

    @pl.when((i == 0) | (te_ref[i] != te_ref[jnp.maximum(i - 1, 0)]))
    def _():
        w_sc[...] = _mx(w_ref[0])

    h = h_ref[...]
    for cs in _col_chunks(o_ref.shape[1]):
        o_ref[:, cs] = _dot(h, w_sc[:, cs])


def _add_ln_kernel(x_ref, rt_ref, y0_ref, y1_ref, g_ref, b_ref, o_ref):
    rt = rt_ref[...]
    w0 = rt[:, N_EXPERTS + 2:N_EXPERTS + 3]
    w1 = rt[:, N_EXPERTS + 3:N_EXPERTS + 4]
    h = DN_ALPHA * x_ref[...] + (w0 * y0_ref[...] + w1 * y1_ref[...])
    o_ref[...] = _layer_norm_rows(h, g_ref[...], b_ref[...])


def _moe_layer(x2, rt, w_gu, w_down, ln_g, ln_b):
    m, d = x2.shape
    tm = MOE_TM
    top_idx = rt[:, N_EXPERTS:N_EXPERTS + 2].astype(jnp.int32)

    n_asg = m * TOP_K
    n_rows = n_asg + N_EXPERTS * tm
    n_tiles = n_rows // tm
    e_flat = top_idx.reshape(-1)
    order = jnp.argsort(e_flat, stable=True).astype(jnp.int32)
    slot = jnp.argsort(order).astype(jnp.int32)
    counts = jnp.sum((e_flat[:, None] == jnp.arange(N_EXPERTS)[None, :]).astype(jnp.int32), axis=0)
    tiles_per = (counts + tm - 1) // tm
    tile_end = jnp.cumsum(tiles_per)
    row_start = (tile_end - tiles_per) * tm
    grp_start = jnp.cumsum(counts) - counts
    tile_expert = jnp.minimum(jnp.sum((jnp.arange(n_tiles)[:, None] >= tile_end[None, :]).astype(jnp.int32),
                                      axis=1), N_EXPERTS - 1).astype(jnp.int32)
    pos = (row_start[e_flat] + slot - grp_start[e_flat]).astype(jnp.int32).reshape(m, TOP_K)
    row_e = jnp.repeat(tile_expert, tm)
    row_src = (grp_start - row_start)[row_e].astype(jnp.int32) + jnp.arange(n_rows, dtype=jnp.int32)
    row_token = order[jnp.clip(row_src, 0, n_asg - 1)] // TOP_K

    xs = x2[row_token]
    f = w_gu.shape[2] // 2
    tf = MOE_TF
    nf = f // tf
    h = pl.pallas_call(
        _moe_up_kernel,
        grid_spec=pltpu.PrefetchScalarGridSpec(
            num_scalar_prefetch=1,
            grid=(nf, n_tiles),
            in_specs=[pl.BlockSpec((tm, d), lambda j, i, te: (i, 0)),
                      pl.BlockSpec((1, d, tf), lambda j, i, te: (te[i], 0, j)),
                      pl.BlockSpec((1, d, tf), lambda j, i, te: (te[i], 0, j + nf))],
            out_specs=pl.BlockSpec((tm, tf), lambda j, i, te: (i, j)),
            scratch_shapes=[pltpu.VMEM((d, tf), MXU_DTYPE), pltpu.VMEM((d, tf), MXU_DTYPE)],
        ),
        out_shape=jax.ShapeDtypeStruct((n_rows, f), MXU_DTYPE),
        compiler_params=_params("arbitrary", "arbitrary"),
        name="moe_up",
    )(tile_expert, xs, w_gu, w_gu)
    ys = pl.pallas_call(
        _moe_down_kernel,
        grid_spec=pltpu.PrefetchScalarGridSpec(
            num_scalar_prefetch=1,
            grid=(n_tiles,),
            in_specs=[pl.BlockSpec((tm, f), lambda i, te: (i, 0)),
                      pl.BlockSpec((1, f, d), lambda i, te: (te[i], 0, 0))],
            out_specs=pl.BlockSpec((tm, d), lambda i, te: (i, 0)),
            scratch_shapes=[pltpu.VMEM((f, d), MXU_DTYPE)],
        ),
        out_shape=jax.ShapeDtypeStruct((n_rows, d), F32),
        compiler_params=_params("arbitrary"),
        name="moe_down",
    )(tile_expert, h, w_down)

    y0 = ys[pos[:, 0]]
    y1 = ys[pos[:, 1]]
    return pl.pallas_call(
        _add_ln_kernel,
        grid=(m // 512,),
        in_specs=[pl.BlockSpec((512, d), lambda i: (i, 0)), pl.BlockSpec((512, LANE), lambda i: (i, 0)),
                  pl.BlockSpec((512, d), lambda i: (i, 0)), pl.BlockSpec((512, d), lambda i: (i, 0))]
        + [pl.BlockSpec((1, d), lambda i: (0, 0))] * 2,
        out_specs=pl.BlockSpec((512, d), lambda i: (i, 0)),
        out_shape=jax.ShapeDtypeStruct((m, d), F32),
        compiler_params=_params("parallel"),
        name="moe_add_ln",
    )(x2, rt, y0, y1, ln_g.reshape(1, d), ln_b.reshape(1, d))


def kernel(x, l0_w_in, l0_cmp_pe_k, l0_cmp_pe_v, l0_cmp_wk1, l0_cmp_wk2, l0_cmp_wv1, l0_cmp_wv2, l0_w_o, l0_ln1_g, l0_ln1_b, l0_ffn_w_gu, l0_ffn_w_down, l0_ln2_g, l0_ln2_b, l1_w_in, l1_w_gate2, l1_b_gate2, l1_head_norm_g, l1_w_o, l1_ln1_g, l1_ln1_b, l1_router, l1_moe_w_gu, l1_moe_w_down, l1_ln2_g, l1_ln2_b):
    bsz, t, d = x.shape
    x2 = x.reshape(bsz * t, d)

    o = _nsa_mixer(x2, bsz, t, l0_w_in, l0_cmp_pe_k, l0_cmp_pe_v, l0_cmp_wk1, l0_cmp_wk2,
                   l0_cmp_wv1, l0_cmp_wv2)
    x2 = _matmul_res_ln(o, _mx(l0_w_o), x2, l0_ln1_g, l0_ln1_b, 512)
    hmid = _swiglu_up(x2, _mx(l0_ffn_w_gu), 512, FFN_DENSE)
    x2 = _matmul_res_ln(hmid, _mx(l0_ffn_w_down), x2, l0_ln2_g, l0_ln2_b, 512)

    o = _gla_mixer(x2, bsz, t, l1_w_in, l1_w_gate2, l1_b_gate2, l1_head_norm_g)
    x2, rt = _matmul_res_ln_route(o, _mx(l1_w_o), x2, l1_ln1_g, l1_ln1_b, l1_router, 512)
    x2 = _moe_layer(x2, rt, l1_moe_w_gu, l1_moe_w_down, l1_ln2_g, l1_ln2_b)
    return x2.reshape(bsz, t, d)
```

```python
import functools

import numpy as np
import jax
import jax.numpy as jnp
from jax import lax
from jax.experimental import pallas as pl
from jax.experimental.pallas import tpu as pltpu

F32 = jnp.float32
MXU_DTYPE = jnp.bfloat16

D_MODEL = 1024
DEPTH = 2
DN_ALPHA = (2 * DEPTH) ** 0.25
LN_EPS = 1e-5
RMS_EPS = 1e-6

NSA_HEADS = 16
NSA_HEAD_DIM = 64
NSA_KV_GROUPS = 4
NSA_HPG = NSA_HEADS // NSA_KV_GROUPS
NSA_KV_DIM = NSA_KV_GROUPS * NSA_HEAD_DIM
CMP_LEN = 32
CMP_STRIDE = 16
CMP_HIDDEN = 256
SLC_LEN = 64
N_SEL = 16
WINDOW = 512
QTILE = 128
KBLK = 128
NCOL = NSA_HPG * QTILE
CMP_CHUNK = 128
SEL_UNROLL = 6
VT_ROWS = 80
LOG2E = 1.4426950408889634

GLA_HEADS = 4
GLA_KEY_DIM = D_MODEL // 2
GLA_VAL_DIM = D_MODEL
GLA_DK = GLA_KEY_DIM // GLA_HEADS
GLA_DV = GLA_VAL_DIM // GLA_HEADS
GLA_GATE_RANK = 16
GLA_GATE_NORM = 16.0
GLA_CHUNK = 64
GLA_TSTEP = 256

FFN_DENSE = 2816
N_EXPERTS = 8
TOP_K = 2
FFN_EXPERT = 3584
MOE_TM = 512
MOE_TF = 1792
SWIGLU_CHUNK = 256
MM_CHUNK = 512

LANE = 128
NEG = -1e30
VMEM_LIMIT = 56 * 1024 * 1024


def _params(*sem):
    return pltpu.CompilerParams(dimension_semantics=sem, vmem_limit_bytes=VMEM_LIMIT)


def _mx(a):
    return a.astype(MXU_DTYPE)


def _dot(a, b):
    return jnp.dot(a, b, preferred_element_type=F32)


def _dot_nt(a, b):
    return lax.dot_general(a, b, (((1,), (1,)), ((), ())), preferred_element_type=F32)


def _col_chunks(n):
    step = next((s for s in (MM_CHUNK, MM_CHUNK + LANE, MM_CHUNK - LANE) if n % s == 0), n)
    return [slice(j, j + step) for j in range(0, n, step)]


def _mm_kernel(x_ref, w_ref, o_ref):
    x = _mx(x_ref[...])
    for cs in _col_chunks(o_ref.shape[1]):
        o_ref[:, cs] = _dot(x, w_ref[:, cs]).astype(o_ref.dtype)


def _matmul(x, w, tm, tn, out_dtype=F32):
    m, k = x.shape
    n = w.shape[1]
    return pl.pallas_call(
        _mm_kernel,
        grid=(m // tm, n // tn),
        in_specs=[pl.BlockSpec((tm, k), lambda i, j: (i, 0)),
                  pl.BlockSpec((k, tn), lambda i, j: (0, j))],
        out_specs=pl.BlockSpec((tm, tn), lambda i, j: (i, j)),
        out_shape=jax.ShapeDtypeStruct((m, n), out_dtype),
        compiler_params=_params("parallel", "parallel"),
        name="matmul",
    )(x, w)


def _mm_scaled_kernel(x_ref, w_ref, s_ref, o_ref):
    x = _mx(x_ref[...])
    for cs in _col_chunks(o_ref.shape[1]):
        o_ref[:, cs] = (_dot(x, w_ref[:, cs]) * s_ref[:, cs]).astype(o_ref.dtype)


def _matmul_scaled(x, w, col_scale, tm, tn, out_dtype):
    m, k = x.shape
    n = w.shape[1]
    return pl.pallas_call(
        _mm_scaled_kernel,
        grid=(m // tm, n // tn),
        in_specs=[pl.BlockSpec((tm, k), lambda i, j: (i, 0)),
                  pl.BlockSpec((k, tn), lambda i, j: (0, j)),
                  pl.BlockSpec((1, tn), lambda i, j: (0, j))],
        out_specs=pl.BlockSpec((tm, tn), lambda i, j: (i, j)),
        out_shape=jax.ShapeDtypeStruct((m, n), out_dtype),
        compiler_params=_params("parallel", "parallel"),
        name="matmul_scaled",
    )(x, w, col_scale)


def _layer_norm_rows(h, g, b):
    mu = jnp.mean(h, -1, keepdims=True)
    d = h - mu
    var = jnp.mean(d * d, -1, keepdims=True)
    return d * lax.rsqrt(var + LN_EPS) * g + b


def _mm_ln_kernel(x_ref, w_ref, r_ref, g_ref, b_ref, o_ref):
    y = _dot(_mx(x_ref[...]), w_ref[...])
    o_ref[...] = _layer_norm_rows(DN_ALPHA * r_ref[...] + y, g_ref[...], b_ref[...])


def _matmul_res_ln(x, w, res, g, b, tm):
    m, k = x.shape
    n = w.shape[1]
    return pl.pallas_call(
        _mm_ln_kernel,
        grid=(m // tm,),
        in_specs=[pl.BlockSpec((tm, k), lambda i: (i, 0)),
                  pl.BlockSpec((k, n), lambda i: (0, 0)),
                  pl.BlockSpec((tm, n), lambda i: (i, 0)),
                  pl.BlockSpec((1, n), lambda i: (0, 0)),
                  pl.BlockSpec((1, n), lambda i: (0, 0))],
        out_specs=pl.BlockSpec((tm, n), lambda i: (i, 0)),
        out_shape=jax.ShapeDtypeStruct((m, n), F32),
        compiler_params=_params("parallel"),
        name="matmul_res_ln",
    )(x, w, res, g.reshape(1, n), b.reshape(1, n))


def _swiglu_cols(x, wg_ref, wu_ref, o_ref):
    n = o_ref.shape[1]
    step = SWIGLU_CHUNK if n % SWIGLU_CHUNK == 0 else LANE
    for j in range(0, n, step):
        a = _dot(x, wg_ref[:, j:j + step])
        b = _dot(x, wu_ref[:, j:j + step])
        o_ref[:, j:j + step] = (a * jax.nn.sigmoid(a) * b).astype(o_ref.dtype)


def _swiglu_up_kernel(x_ref, wg_ref, wu_ref, o_ref):
    _swiglu_cols(_mx(x_ref[...]), wg_ref, wu_ref, o_ref)


def _swiglu_up(x, w_gu, tm, tn):
    m, k = x.shape
    f = w_gu.shape[1] // 2
    nj = f // tn
    return pl.pallas_call(
        _swiglu_up_kernel,
        grid=(m // tm, nj),
        in_specs=[pl.BlockSpec((tm, k), lambda i, j: (i, 0)),
                  pl.BlockSpec((k, tn), lambda i, j: (0, j)),
                  pl.BlockSpec((k, tn), lambda i, j: (0, j + nj))],
        out_specs=pl.BlockSpec((tm, tn), lambda i, j: (i, j)),
        out_shape=jax.ShapeDtypeStruct((m, f), MXU_DTYPE),
        compiler_params=_params("parallel", "parallel"),
        name="swiglu_up",
    )(x, w_gu, w_gu)


def _cmp_up_kernel(x0_ref, x1_ref, pea_ref, peb_ref, wa_ref, wb_ref, p_ref, q_ref):
    tm = p_ref.shape[0]
    half_out = p_ref.shape[1] // 2
    for hi, x_ref in enumerate((x0_ref, x1_ref)):
        cs = slice(hi * LANE, (hi + 1) * LANE)
        os_ = slice(hi * half_out, (hi + 1) * half_out)
        p = q = None
        for l in range(CMP_STRIDE):
            xl = x_ref[pl.ds(l, tm, stride=CMP_STRIDE), :]
            da = _dot(_mx(xl + pea_ref[l:l + 1, cs]), wa_ref[l, cs, os_])
            db = _dot(_mx(xl + peb_ref[l:l + 1, cs]), wb_ref[l, cs, os_])
            p = da if p is None else p + da
            q = db if q is None else q + db
        p_ref[:, os_] = p
        q_ref[:, os_] = q


def _cmp_down_kernel(p_ref, q_ref, w_ref, o_ref):
    for gi in range(NSA_KV_GROUPS):
        cs = slice(gi * CMP_HIDDEN, (gi + 1) * CMP_HIDDEN)
        h = jax.nn.gelu(p_ref[:, cs] + q_ref[:, cs])
        o_ref[:, gi * LANE:(gi + 1) * LANE] = _dot(_mx(h), w_ref[...])


def _compress(proj_f, col_blk, bsz, pe, w1, w2):
    g, dk = NSA_KV_GROUPS, NSA_HEAD_DIM
    nch_all = proj_f.shape[0] // CMP_STRIDE
    nch = nch_all // bsz
    tm = min(256, nch_all)
    eye = jnp.eye(g, dtype=F32)

    def block_diag(w_half):
        w3 = w_half.reshape(CMP_STRIDE, dk, CMP_HIDDEN)
        return _mx(jnp.einsum("ldc,gh->lgdhc", w3, eye).reshape(CMP_STRIDE, g * dk, g * CMP_HIDDEN))

    half = CMP_STRIDE * dk
    pe_a = jnp.tile(pe[:CMP_STRIDE], (1, g))
    pe_b = jnp.tile(pe[CMP_STRIDE:], (1, g))
    wide = g * CMP_HIDDEN
    p, q = pl.pallas_call(
        _cmp_up_kernel,
        grid=(nch_all // tm,),
        in_specs=[pl.BlockSpec((tm * CMP_STRIDE, LANE), lambda i: (i, 2 * col_blk)),
                  pl.BlockSpec((tm * CMP_STRIDE, LANE), lambda i: (i, 2 * col_blk + 1)),
                  pl.BlockSpec((CMP_STRIDE, g * dk), lambda i: (0, 0)),
                  pl.BlockSpec((CMP_STRIDE, g * dk), lambda i: (0, 0)),
                  pl.BlockSpec((CMP_STRIDE, g * dk, wide), lambda i: (0, 0, 0)),
                  pl.BlockSpec((CMP_STRIDE, g * dk, wide), lambda i: (0, 0, 0))],
        out_specs=[pl.BlockSpec((tm, wide), lambda i: (i, 0)),
                   pl.BlockSpec((tm, wide), lambda i: (i, 0))],
        out_shape=[jax.ShapeDtypeStruct((nch_all, wide), F32)] * 2,
        compiler_params=_params("parallel"),
        name="cmp_up",
    )(proj_f, proj_f, pe_a, pe_b, block_diag(w1[:half]), block_diag(w1[half:]))
    q = q.reshape(bsz, nch, wide)
    q = jnp.concatenate([q[:, 1:], jnp.zeros_like(q[:, :1])], axis=1).reshape(nch_all, wide)
    w2p = jnp.pad(_mx(w2), ((0, 0), (0, LANE - dk)))
    out = pl.pallas_call(
        _cmp_down_kernel,
        grid=(nch_all // tm,),
        in_specs=[pl.BlockSpec((tm, wide), lambda i: (i, 0)),
                  pl.BlockSpec((tm, wide), lambda i: (i, 0)),
                  pl.BlockSpec((CMP_HIDDEN, LANE), lambda i: (0, 0))],
        out_specs=pl.BlockSpec((tm, g * LANE), lambda i: (i, 0)),
        out_shape=jax.ShapeDtypeStruct((nch_all, g * LANE), F32),
        compiler_params=_params("parallel"),
        name="cmp_down",
    )(p, q, w2p)
    out = out.reshape(bsz, nch, g, LANE)[..., :dk]
    valid = (jnp.arange(nch) < nch - 1)[None, :, None, None]
    return jnp.where(valid, out, 0.0)


def _queries_t(q_blk):
    qt = q_blk.astype(F32).T
    dk = NSA_HEAD_DIM
    return _mx(jnp.concatenate([qt[h * dk:(h + 1) * dk, :] for h in range(NSA_HPG)], axis=1))


def _nsa_cmp_kernel(q_ref, qf_ref, kc_ref, vct_ref, ovl_ref, ocmp_ref, sel_ref, flag_ref,
                    imp_ref, key_ref, cnt_ref):
    c = pl.program_id(2)
    ncp = kc_ref.shape[2]
    ns = ovl_ref.shape[0]
    dk = NSA_HEAD_DIM
    rhs = jnp.concatenate([_queries_t(q_ref[...]), qf_ref[0]], axis=0)
    chunk = min(CMP_CHUNK, ncp)
    n_chunks = (c * (QTILE // CMP_STRIDE) + (QTILE - CMP_LEN) // CMP_STRIDE + chunk) // chunk

    def cmp_branch(rows):
        n_i = lax.broadcasted_iota(jnp.int32, (rows, QTILE), 0)
        tq = c * QTILE + lax.broadcasted_iota(jnp.int32, (rows, QTILE), 1)
        bias = _mx(jnp.where(n_i * CMP_STRIDE + (CMP_LEN - 1) <= tq, 0.0, NEG))
        s = _dot(jnp.concatenate([kc_ref[0, 0, :rows, :], bias], axis=1), rhs)
        e = jnp.exp2(s - jnp.max(s, axis=0, keepdims=True))
        seen = jnp.concatenate([tq[0:1, :]] * NSA_HPG, axis=1) >= CMP_LEN - 1
        r = jnp.where(seen, 1.0 / jnp.maximum(jnp.sum(e, axis=0, keepdims=True), 1e-30), 0.0)
        ocmp_ref[0, 0, 0] = _dot(vct_ref[0, 0, :, :rows], _mx(e)) * r
        psum = e[:, 0:QTILE] * r[:, 0:QTILE]
        for h in range(1, NSA_HPG):
            psum = psum + e[:, h * QTILE:(h + 1) * QTILE] * r[:, h * QTILE:(h + 1) * QTILE]
        hi = _mx(psum)
        lo = _mx(psum - hi.astype(F32))
        ovl = ovl_ref[:, :rows]
        imp_ref[...] = _dot(ovl, hi) + _dot(ovl, lo)

    for k in range(1, ncp // chunk + 1):
        pl.when(n_chunks == k)(functools.partial(cmp_branch, k * chunk))

    j_i = lax.broadcasted_iota(jnp.int32, (ns, QTILE), 0)
    tq2 = c * QTILE + lax.broadcasted_iota(jnp.int32, (ns, QTILE), 1)
    valid = j_i * SLC_LEN <= tq2
    cur = tq2 >> (SLC_LEN.bit_length() - 1)
    forced = valid & ((j_i == 0) | (j_i == cur) | (j_i == cur - 1))
    score = jnp.where(forced, 1e9, jnp.where(valid, imp_ref[...], -1e9))
    bits = lax.bitcast_convert_type(score, jnp.int32)
    key_ref[...] = jnp.where(bits < 0, bits ^ 0x7FFFFFFF, bits)
    cnt_ref[...] = jnp.zeros(cnt_ref.shape, jnp.int32)

    sub = lax.broadcasted_iota(jnp.int32, (8, QTILE), 0)
    n_it = (c * (QTILE // SLC_LEN) + QTILE // SLC_LEN + 7) // 8

    def rank_blocks(n_jt):
        def it_body(it, carry):
            base = pl.multiple_of(it * 8, 8)
            rows8 = key_ref[pl.ds(base, 8), :]
            rb = [jnp.broadcast_to(rows8[il:il + 1, :], (8, QTILE)) for il in range(8)]
            for jt in range(n_jt):
                js = slice(jt * 8, (jt + 1) * 8)
                thr = key_ref[js, :] - jnp.where(jt > it, 1, 0)
                acc = cnt_ref[js, :]
                for il in range(8):
                    acc = acc + jnp.where(rb[il] > thr, 1, 0)
                cnt_ref[js, :] = acc
            corr = jnp.zeros((8, QTILE), jnp.int32)
            for il in range(8):
                corr = corr + jnp.where((rb[il] == rows8) & (sub > il), 1, 0)
            cnt_ref[pl.ds(base, 8), :] += corr
            return carry

        lax.fori_loop(0, n_it, it_body, 0)

    sizes = sorted({min(ns // 8, max(1, ns // 32) * k) for k in range(1, 5)})
    for lo, hi in zip([0] + sizes[:-1], sizes):
        pl.when((n_it > lo) & (n_it <= hi))(functools.partial(rank_blocks, hi))
    sel = jnp.where((cnt_ref[...] < min(N_SEL, ns)) & valid, 1.0, 0.0)
    sel_ref[0, 0, 0] = sel
    flag_ref[0, 0, 0] = _dot_nt(jnp.ones((8, QTILE), MXU_DTYPE), _mx(sel))


def _nsa_cmp_select(proj, q_feat, kcmp, vcmp_t, ovl_t):
    bsz, g, ncp, _ = kcmp.shape
    dk = NSA_HEAD_DIM
    nt = proj.shape[0] // (bsz * QTILE)
    ns = ovl_t.shape[0]
    return pl.pallas_call(
        _nsa_cmp_kernel,
        grid=(bsz, g, nt),
        in_specs=[pl.BlockSpec((QTILE, NSA_HPG * dk), lambda b, gg, c: (b * nt + c, gg)),
                  pl.BlockSpec((1, dk + QTILE, NCOL), lambda b, gg, c: (gg, 0, 0)),
                  pl.BlockSpec((1, 1, ncp, LANE), lambda b, gg, c: (b, gg, 0, 0)),
                  pl.BlockSpec((1, 1, dk, ncp), lambda b, gg, c: (b, gg, 0, 0)),
                  pl.BlockSpec((ns, ncp), lambda b, gg, c: (0, 0))],
        out_specs=[pl.BlockSpec((1, 1, 1, dk, NCOL), lambda b, gg, c: (b, gg, c, 0, 0)),
                   pl.BlockSpec((1, 1, 1, ns, QTILE), lambda b, gg, c: (b, gg, c, 0, 0)),
                   pl.BlockSpec((1, 1, 1, 8, ns), lambda b, gg, c: (b, gg, c, 0, 0))],
        out_shape=[jax.ShapeDtypeStruct((bsz, g, nt, dk, NCOL), F32),
                   jax.ShapeDtypeStruct((bsz, g, nt, ns, QTILE), F32),
                   jax.ShapeDtypeStruct((bsz, g, nt, 8, ns), F32)],
        scratch_shapes=[pltpu.VMEM((ns, QTILE), F32), pltpu.VMEM((ns, QTILE), jnp.int32),
                        pltpu.VMEM((ns, QTILE), jnp.int32)],
        compiler_params=_params("parallel", "parallel", "parallel"),
        name="nsa_cmp_select",
    )(proj, q_feat, kcmp, vcmp_t, ovl_t)


def _list_stride(npair):
    trip = 2 * SEL_UNROLL
    return -(-npair // trip) * trip


def _values_t(v_blk):
    vt = v_blk.astype(F32).T[:VT_ROWS, :]
    ones_row = lax.broadcasted_iota(jnp.int32, vt.shape, 0) == NSA_HEAD_DIM
    return _mx(jnp.where(ones_row, 1.0, vt))


def _nsa_attn_kernel(lst_ref, cnt_ref, q_ref, qf_ref, ks_ref, vs_ref, kw_ref, vw_ref, pos_ref, sel_ref,
                     ocmp_ref, gl_ref, o_ref, ksa_sc, kwa_sc, vst_sc, vwt_sc, gate_sc, ow_sc, m_sc, acc_sc):
    b = pl.program_id(0)
    g = pl.program_id(1)
    c = pl.program_id(2)
    nt = pl.num_programs(2)
    npair = ks_ref.shape[1] // KBLK
    dk = NSA_HEAD_DIM

    @pl.when(c == 0)
    def _():
        def prep(j, carry):
            rows = pl.ds(pl.multiple_of(j * KBLK, KBLK), KBLK)
            feats = pos_ref[rows, :]
            ksa_sc[rows, :] = ks_ref[0, rows, :] + feats
            kwa_sc[rows, :] = kw_ref[0, rows, :] + feats
            vst_sc[j] = _values_t(vs_ref[0, rows, :])
            vwt_sc[j] = _values_t(vw_ref[0, rows, :])
            return carry

        lax.fori_loop(0, npair, prep, 0)

    rhs = jnp.concatenate([_queries_t(q_ref[...]), qf_ref[0]], axis=0)
    key_i = lax.broadcasted_iota(jnp.int32, (KBLK, QTILE), 0)
    q_i = lax.broadcasted_iota(jnp.int32, (KBLK, QTILE), 1)
    heads = [slice(h * QTILE, (h + 1) * QTILE) for h in range(NSA_HPG)]

    m_sc[...] = jnp.full(m_sc.shape, NEG, F32)
    acc_sc[...] = jnp.zeros(acc_sc.shape, F32)
    step = (b * NSA_KV_GROUPS + g) * nt + c

    def group_blocks(gi):
        return [lst_ref[step * _list_stride(npair) + gi * SEL_UNROLL + u] for u in range(SEL_UNROLL)]

    def group_scores(gi, own_first):
        scores = []
        for u, jj in enumerate(group_blocks(gi)):
            js = jnp.maximum(jj, 0)
            rows = sel_ref[0, 0, 0, pl.ds(2 * js, 2), :]
            if own_first and u == 0:
                picked = jnp.where(key_i < SLC_LEN, rows[0:1, :], rows[1:2, :]) > 0.5
                bias = _mx(jnp.where(picked & (key_i <= q_i), 0.0, NEG))
            else:
                half = _mx(jnp.where(jj >= 0, (rows - 1.0) * -NEG, NEG))
                bias = jnp.concatenate([jnp.broadcast_to(half[0:1, :], (SLC_LEN, QTILE)),
                                        jnp.broadcast_to(half[1:2, :], (SLC_LEN, QTILE))], axis=0)
            keys = ksa_sc[pl.ds(pl.multiple_of(js * KBLK, KBLK), KBLK), :]
            scores.append(_dot(jnp.concatenate([keys, bias], axis=1), rhs))
        return scores

    def softmax_update(gi, scores):
        blocks = [jnp.maximum(jj, 0) for jj in group_blocks(gi)]
        m_new = []
        for cs in heads:
            m_old = m_sc[:, cs]
            mx_ = m_old
            for u in range(SEL_UNROLL):
                mx_ = jnp.maximum(mx_, jnp.max(scores[u][:, cs], axis=0, keepdims=True))
            acc_sc[:, cs] = jnp.exp2(m_old - mx_) * acc_sc[:, cs]
            m_sc[:, cs] = mx_
            m_new.append(mx_)
        upd = None
        for u in range(SEL_UNROLL):
            e = jnp.concatenate([_mx(jnp.exp2(scores[u][:, cs] - m_new[hi])) for hi, cs in enumerate(heads)],
                                axis=1)
            d = _dot(vst_sc[blocks[u]], e)
            upd = d if upd is None else upd + d
        acc_sc[...] += upd

    scores_first = group_scores(0, True)

    nwin = WINDOW // KBLK + 1
    blk0 = jnp.maximum(c - (nwin - 1), 0)
    row = lax.broadcasted_iota(jnp.int32, (nwin * KBLK, QTILE), 0)
    dist = lax.broadcasted_iota(jnp.int32, (nwin * KBLK, QTILE), 1) + (c - blk0) * KBLK - row
    bias = _mx(jnp.where((dist >= 0) & (dist < WINDOW), 0.0, NEG))
    keys = kwa_sc[pl.ds(pl.multiple_of(blk0 * KBLK, KBLK), nwin * KBLK), :]
    s = _dot(jnp.concatenate([keys, bias], axis=1), rhs)
    e_all = jnp.concatenate([_mx(jnp.exp2(s[:, cs] - jnp.max(s[:, cs], axis=0, keepdims=True)))
                             for cs in heads], axis=1)
    acc = None
    for t in range(nwin):
        d = _dot(vwt_sc[blk0 + t], e_all[t * KBLK:(t + 1) * KBLK, :])
        acc = d if acc is None else acc + d
    ow_sc[...] = acc[:dk, :] * (1.0 / jnp.maximum(acc[dk:dk + 1, :], 1e-30))

    def trip(r, carry, scores_a=None):
        scores_a = group_scores(2 * r, False) if scores_a is None else scores_a
        scores_b = group_scores(2 * r + 1, False)
        softmax_update(2 * r, scores_a)
        softmax_update(2 * r + 1, scores_b)
        return carry

    trip(0, 0, scores_first)
    lax.fori_loop(1, (cnt_ref[step] + 2 * SEL_UNROLL - 1) // (2 * SEL_UNROLL), trip, 0)
    o_sel = acc_sc[:dk, :] * (1.0 / jnp.maximum(acc_sc[dk:dk + 1, :], 1e-30))
    o_win = ow_sc[...]

    gate_sc[...] = gl_ref[...].T
    ocmp = ocmp_ref[0, 0, 0]
    outs = []
    for h, cs in enumerate(heads):
        gt = jax.nn.sigmoid(gate_sc[pl.ds((g * NSA_HPG + h) * 3, 3), :])
        outs.append(gt[0:1] * ocmp[:, cs] + gt[1:2] * o_sel[:, cs] + gt[2:3] * o_win[:, cs])
    o_ref[...] = jnp.concatenate(outs, axis=0).T.astype(o_ref.dtype)


def _nsa_attend(blk_list, blk_count, proj, proj_f, q_feat, pos_feat, sel_t, ocmp_t, bsz):
    g, dk = NSA_KV_GROUPS, NSA_HEAD_DIM
    t = proj.shape[0] // bsz
    nt = t // QTILE
    npair = t // KBLK
    ns = sel_t.shape[3]
    proj3 = proj.reshape(bsz, t, proj.shape[1])
    seg = D_MODEL // LANE
    gate_blk = proj_f.shape[1] // LANE - 1
    tile = lambda b, gg, c, lst, cnt: (b, gg, c, 0, 0)
    rows = lambda b, gg, c, lst, cnt: (b * nt + c, gg)
    kv = lambda k: (lambda b, gg, c, lst, cnt: (b, 0, seg + k * g + gg))
    grid_spec = pltpu.PrefetchScalarGridSpec(
        num_scalar_prefetch=2,
        grid=(bsz, g, nt),
        in_specs=[pl.BlockSpec((QTILE, NSA_HPG * dk), rows),
                  pl.BlockSpec((1, dk + QTILE, NCOL), lambda b, gg, c, lst, cnt: (gg, 0, 0)),
                  pl.BlockSpec((1, t, LANE), kv(0)),
                  pl.BlockSpec((1, t, LANE), kv(1)),
                  pl.BlockSpec((1, t, LANE), kv(2)),
                  pl.BlockSpec((1, t, LANE), kv(3)),
                  pl.BlockSpec((t, LANE), lambda b, gg, c, lst, cnt: (0, 0)),
                  pl.BlockSpec((1, 1, 1, ns, QTILE), tile),
                  pl.BlockSpec((1, 1, 1, dk, NCOL), tile),
                  pl.BlockSpec((QTILE, LANE), lambda b, gg, c, lst, cnt: (b * nt + c, gate_blk))],
        out_specs=pl.BlockSpec((QTILE, NSA_HPG * dk), rows),
        scratch_shapes=[pltpu.VMEM((t, LANE), MXU_DTYPE), pltpu.VMEM((t, LANE), MXU_DTYPE),
                        pltpu.VMEM((npair, VT_ROWS, KBLK), MXU_DTYPE),
                        pltpu.VMEM((npair, VT_ROWS, KBLK), MXU_DTYPE),
                        pltpu.VMEM((LANE, QTILE), F32),
                        pltpu.VMEM((dk, NCOL), F32),
                        pltpu.VMEM((1, NCOL), F32), pltpu.VMEM((VT_ROWS, NCOL), F32)],
    )
    return pl.pallas_call(
        _nsa_attn_kernel,
        grid_spec=grid_spec,
        out_shape=jax.ShapeDtypeStruct((bsz * t, NSA_HEADS * dk), MXU_DTYPE),
        compiler_params=_params("parallel", "parallel", "arbitrary"),
        name="nsa_attend",
    )(blk_list, blk_count, proj, q_feat, proj3, proj3, proj3, proj3, pos_feat, sel_t, ocmp_t, proj_f)


def _nsa_mixer(x2, bsz, t, w_in, pe_k, pe_v, wk1, wk2, wv1, wv2):
    g, hpg, dk = NSA_KV_GROUPS, NSA_HPG, NSA_HEAD_DIM
    nt = t // QTILE
    d = w_in.shape[0]
    splits = np.cumsum([D_MODEL] + [NSA_KV_DIM] * 6).tolist()
    wq, wkc, wvc, wks, wvs, wkw, wvw, wgl = jnp.split(w_in, splits, axis=1)

    def lane_groups(w):
        return jnp.pad(w.reshape(d, g, dk), ((0, 0), (0, 0), (0, LANE - dk))).reshape(d, g * LANE)

    w_a = _mx(jnp.concatenate([wq] + [lane_groups(w) for w in (wks, wvs, wkw, wvw)], axis=1))
    scale = jnp.concatenate([jnp.full((1, D_MODEL), dk ** -0.5 * LOG2E, F32),
                             jnp.ones((1, w_a.shape[1] - D_MODEL), F32)], axis=1)
    w_b = _mx(jnp.concatenate([wkc, wvc, jnp.pad(wgl, ((0, 0), (0, LANE - wgl.shape[1])))], axis=1))
    proj = _matmul_scaled(x2, w_a, scale, 512, w_a.shape[1], MXU_DTYPE)
    proj_f = _matmul(x2, w_b, 512, w_b.shape[1])

    slopes = 2.0 ** (-8.0 * (jnp.arange(NSA_HEADS, dtype=F32) + 1.0) / NSA_HEADS) * LOG2E
    slopes = jnp.broadcast_to(slopes.reshape(g, 1, hpg, 1), (g, 1, hpg, QTILE)).reshape(g, 1, NCOL)
    s1 = slopes.astype(jnp.bfloat16)
    s2 = (slopes - s1.astype(F32)).astype(jnp.bfloat16)
    s3 = (slopes - s1.astype(F32) - s2.astype(F32)).astype(jnp.bfloat16)
    tail = np.zeros((g, dk - 6 + QTILE, NCOL), np.float32)
    tail[:, dk - 6:, :] = np.tile(np.eye(QTILE, dtype=np.float32), (1, hpg))
    q_feat = jnp.concatenate([s1 * SLC_LEN, s2 * SLC_LEN, s3 * SLC_LEN, s1, s2, s3, jnp.asarray(tail, jnp.bfloat16)],
                             axis=1).astype(MXU_DTYPE)

    def pos_features(kpos):
        feats = np.zeros((kpos.shape[0], LANE - dk), np.float32)
        feats[:, 0:3] = (kpos // SLC_LEN)[:, None]
        feats[:, 3:6] = (kpos % SLC_LEN)[:, None]
        return feats

    pos_feat = jnp.asarray(np.pad(pos_features(np.arange(t)), ((0, 0), (dk, 0))), MXU_DTYPE)

    ncp = t // CMP_STRIDE
    k_cmp = _compress(proj_f, 0, bsz, pe_k, wk1, wk2)
    v_cmp = _compress(proj_f, 1, bsz, pe_v, wv1, wv2)
    cmp_feat = jnp.asarray(pos_features(np.arange(ncp) * CMP_STRIDE + (CMP_LEN - 1)), MXU_DTYPE)
    kcmp = jnp.concatenate([_mx(k_cmp).transpose(0, 2, 1, 3),
                            jnp.broadcast_to(cmp_feat[None, None], (bsz, g, ncp, LANE - dk))], axis=-1)
    vcmp_t = _mx(v_cmp).transpose(0, 2, 3, 1)

    ns = t // SLC_LEN
    cs = np.arange(ncp) * CMP_STRIDE
    ce = cs + CMP_LEN - 1
    bs = np.arange(ns) * SLC_LEN
    be = bs + SLC_LEN - 1
    ovl_t = ((cs[None, :] <= be[:, None]) & (ce[None, :] >= bs[:, None])).astype(np.float32)
    ovl_t[:, ncp - 1] = 0.0
    ovl_t = jnp.asarray(ovl_t, MXU_DTYPE)

    ocmp_t, sel_t, counts = _nsa_cmp_select(proj, q_feat, kcmp, vcmp_t, ovl_t)
    npair = t // KBLK
    picked = counts[:, :, :, 0, :] > 0.5
    jj = jnp.arange(npair)
    tile_i = jnp.arange(nt)[:, None]
    active = (picked[..., 0::2] | picked[..., 1::2]) & (jj[None, :] < tile_i)
    blk_count = 1 + jnp.sum(active, axis=-1).astype(jnp.int32)
    slot = jnp.cumsum(active, axis=-1)
    hit = active[..., None, :] & (slot[..., None, :] == jj[:, None])
    blk_list = jnp.sum(jnp.where(hit, jj, 0), axis=-1)
    blk_list = jnp.where(jj == 0, tile_i, blk_list)
    blk_list = jnp.where(jj < blk_count[..., None], blk_list, -1).astype(jnp.int32)

    blk_list = jnp.pad(blk_list, ((0, 0), (0, 0), (0, 0), (0, _list_stride(npair) - npair)),
                       constant_values=-1).reshape(-1)
    return _nsa_attend(blk_list, blk_count.reshape(-1), proj, proj_f, q_feat, pos_feat,
                       sel_t, ocmp_t, bsz)


def _log_sigmoid(z):
    return jnp.minimum(z, 0.0) - jnp.log1p(jnp.exp(-jnp.abs(z)))


def _gla_kernel(q_ref, k_ref, v_ref, g_ref, al_ref, w2_ref, b2_ref, hn_ref, o_ref, st_sc):
    @pl.when(pl.program_id(0) == 0)
    def _():
        st_sc[...] = jnp.zeros(st_sc.shape, F32)

    cc = GLA_CHUNK
    r_i = lax.broadcasted_iota(jnp.int32, (cc, cc), 0)
    c_i = lax.broadcasted_iota(jnp.int32, (cc, cc), 1)
    causal = r_i >= c_i
    tri = jnp.where(causal, 1.0, 0.0).astype(MXU_DTYPE)
    ref = cc // 2

    def chunk(ci, carry):
        rows = pl.ds(pl.multiple_of(ci * cc, cc), cc)
        for b in range(q_ref.shape[0]):
            z = _dot(_mx(al_ref[b, rows, :]), w2_ref[...]) + b2_ref[...]
            log_a = _log_sigmoid(z) / GLA_GATE_NORM
            a1 = _mx(log_a)
            r1 = log_a - a1.astype(F32)
            a2 = _mx(r1)
            a3 = _mx(r1 - a2.astype(F32))
            bc_all = _dot(tri, a1) + _dot(tri, a2) + _dot(tri, a3)
            for h in range(GLA_HEADS):
                ks = slice(h * GLA_DK, (h + 1) * GLA_DK)
                vs = slice(h * GLA_DV, (h + 1) * GLA_DV)
                bc = bc_all[:, ks]
                q = q_ref[b, rows, ks] * GLA_DK ** -0.5
                k = k_ref[b, rows, ks]
                v = v_ref[b, rows, vs]
                b_ref = bc[ref:ref + 1, :]
                b_last = bc[cc - 1:cc, :]
                a = _dot_nt(_mx(q * jnp.exp(bc - b_ref)), _mx(k * jnp.exp(b_ref - bc)))
                a = jnp.where(causal, a, 0.0)
                st = st_sc[b * GLA_HEADS + h]
                o = _dot(_mx(a), _mx(v)) + _dot_nt(_mx(q * jnp.exp(bc)), _mx(st))
                kl = k * jnp.exp(b_last - bc)
                st_sc[b * GLA_HEADS + h] = st * jnp.exp(b_last) + _dot(_mx(v.T), _mx(kl))
                o = o * lax.rsqrt(jnp.mean(o * o, -1, keepdims=True) + RMS_EPS)
                gg = g_ref[b, rows, vs]
                o_ref[b, rows, vs] = (o * hn_ref[:, vs] * (gg * jax.nn.sigmoid(gg))).astype(o_ref.dtype)
        return carry

    lax.fori_loop(0, q_ref.shape[1] // cc, chunk, 0)


def _gla_mixer(x2, bsz, t, w_in, w_gate2, b_gate2, head_norm_g):
    n_main = 2 * GLA_KEY_DIM + 2 * GLA_VAL_DIM
    w_all = jnp.pad(_mx(w_in), ((0, 0), (0, LANE - GLA_GATE_RANK)))
    n_all = w_all.shape[1]
    proj = _matmul(x2, w_all, 512, n_all).reshape(bsz, t, n_all)
    w2p = jnp.pad(_mx(w_gate2), ((0, LANE - GLA_GATE_RANK), (0, 0)))
    ts = min(GLA_TSTEP, t)
    out = pl.pallas_call(
        _gla_kernel,
        grid=(t // ts,),
        in_specs=[pl.BlockSpec((bsz, ts, GLA_KEY_DIM), lambda s: (0, s, 0)),
                  pl.BlockSpec((bsz, ts, GLA_KEY_DIM), lambda s: (0, s, 1)),
                  pl.BlockSpec((bsz, ts, GLA_VAL_DIM), lambda s: (0, s, 1)),
                  pl.BlockSpec((bsz, ts, GLA_VAL_DIM), lambda s: (0, s, 2)),
                  pl.BlockSpec((bsz, ts, LANE), lambda s: (0, s, n_main // LANE)),
                  pl.BlockSpec((LANE, GLA_KEY_DIM), lambda s: (0, 0)),
                  pl.BlockSpec((1, GLA_KEY_DIM), lambda s: (0, 0)),
                  pl.BlockSpec((1, GLA_VAL_DIM), lambda s: (0, 0))],
        out_specs=pl.BlockSpec((bsz, ts, GLA_VAL_DIM), lambda s: (0, s, 0)),
        out_shape=jax.ShapeDtypeStruct((bsz, t, GLA_VAL_DIM), MXU_DTYPE),
        scratch_shapes=[pltpu.VMEM((bsz * GLA_HEADS, GLA_DV, GLA_DK), F32)],
        compiler_params=_params("arbitrary"),
        name="gla",
    )(proj, proj, proj, proj, proj, w2p, b_gate2.reshape(1, -1), head_norm_g.reshape(1, -1))
    return out.reshape(bsz * t, GLA_VAL_DIM)


def _route_top2(x, wh_ref, wl_ref):
    xh = _mx(x)
    xl = _mx(x - xh.astype(F32))
    wh = wh_ref[...]
    logits = _dot(xh, wh) + _dot(xl, wh) + _dot(xh, wl_ref[...])
    lane = lax.broadcasted_iota(jnp.int32, logits.shape, 1)
    lg = jnp.where(lane < N_EXPERTS, logits, -jnp.inf)
    m1 = jnp.max(lg, axis=1, keepdims=True)
    i1 = jnp.min(jnp.where(lg == m1, lane, LANE), axis=1, keepdims=True)
    lg2 = jnp.where(lane == i1, -jnp.inf, lg)
    m2 = jnp.max(lg2, axis=1, keepdims=True)
    i2 = jnp.min(jnp.where(lg2 == m2, lane, LANE), axis=1, keepdims=True)
    e2 = jnp.exp(m2 - m1)
    den = 1.0 + e2
    w1 = 1.0 / den
    w2 = e2 / den
    out = jnp.where(lane == N_EXPERTS, i1.astype(F32), 0.0)
    out = jnp.where(lane == N_EXPERTS + 1, i2.astype(F32), out)
    out = jnp.where(lane == N_EXPERTS + 2, w1, out)
    return jnp.where(lane == N_EXPERTS + 3, w2, out)


def _router_kernel(x_ref, wh_ref, wl_ref, o_ref):
    o_ref[...] = _route_top2(x_ref[...], wh_ref, wl_ref)


def _route(x2, w_router, tm):
    m, d = x2.shape
    wr = jnp.pad(w_router, ((0, 0), (0, LANE - N_EXPERTS)))
    wr_hi = _mx(wr)
    wr_lo = _mx(wr - wr_hi.astype(F32))
    return pl.pallas_call(
        _router_kernel,
        grid=(m // tm,),
        in_specs=[pl.BlockSpec((tm, d), lambda i: (i, 0)),
                  pl.BlockSpec((d, LANE), lambda i: (0, 0)),
                  pl.BlockSpec((d, LANE), lambda i: (0, 0))],
        out_specs=pl.BlockSpec((tm, LANE), lambda i: (i, 0)),
        out_shape=jax.ShapeDtypeStruct((m, LANE), F32),
        compiler_params=_params("parallel"),
        name="moe_router",
    )(x2, wr_hi, wr_lo)


def _moe_up_kernel(te_ref, used_ref, x_ref, wg_ref, wu_ref, o_ref, wg_sc, wu_sc):
    i = pl.program_id(1)

    @pl.when((i == 0) | (te_ref[i] != te_ref[jnp.maximum(i - 1, 0)]))
    def _():
        wg_sc[...] = _mx(wg_ref[0])
        wu_sc[...] = _mx(wu_ref[0])

    @pl.when(i < used_ref[0])
    def _():
        _swiglu_cols(_mx(x_ref[...]), wg_sc, wu_sc, o_ref)

    @pl.when(i >= used_ref[0])
    def _():
        o_ref[...] = jnp.zeros(o_ref.shape, o_ref.dtype)


def _moe_down_kernel(te_ref, used_ref, h_ref, w_ref, o_ref, w_sc):
    i = pl.program_id(0)

    @pl.when((i == 0) | (te_ref[i] != te_ref[jnp.maximum(i - 1, 0)]))
    def _():
        w_sc[...] = _mx(w_ref[0])

    @pl.when(i < used_ref[0])
    def _():
        h = h_ref[...]
        for cs in _col_chunks(o_ref.shape[1]):
            o_ref[:, cs] = _dot(h, w_sc[:, cs])

    @pl.when(i >= used_ref[0])
    def _():
        o_ref[...] = jnp.zeros(o_ref.shape, o_ref.dtype)


def _add_ln_kernel(x_ref, rt_ref, y0_ref, y1_ref, g_ref, b_ref, o_ref):
    rt = rt_ref[...]
    w0 = rt[:, N_EXPERTS + 2:N_EXPERTS + 3]
    w1 = rt[:, N_EXPERTS + 3:N_EXPERTS + 4]
    h = DN_ALPHA * x_ref[...] + (w0 * y0_ref[...] + w1 * y1_ref[...])
    o_ref[...] = _layer_norm_rows(h, g_ref[...], b_ref[...])


def _moe_layer(x2, rt, w_gu, w_down, ln_g, ln_b):
    m, d = x2.shape
    tm = MOE_TM
    top_idx = rt[:, N_EXPERTS:N_EXPERTS + 2].astype(jnp.int32)

    n_asg = m * TOP_K
    n_rows = n_asg + N_EXPERTS * tm
    n_tiles = n_rows // tm
    e_flat = top_idx.reshape(-1)
    order = jnp.argsort(e_flat, stable=True).astype(jnp.int32)
    slot = jnp.argsort(order).astype(jnp.int32)
    counts = jnp.sum((e_flat[:, None] == jnp.arange(N_EXPERTS)[None, :]).astype(jnp.int32), axis=0)
    tiles_per = (counts + tm - 1) // tm
    tile_end = jnp.cumsum(tiles_per)
    n_used = tile_end[-1:].astype(jnp.int32)
    row_start = (tile_end - tiles_per) * tm
    grp_start = jnp.cumsum(counts) - counts
    tile_expert = jnp.minimum(jnp.sum((jnp.arange(n_tiles)[:, None] >= tile_end[None, :]).astype(jnp.int32),
                                      axis=1), N_EXPERTS - 1).astype(jnp.int32)
    pos = (row_start[e_flat] + slot - grp_start[e_flat]).astype(jnp.int32).reshape(m, TOP_K)
    row_e = jnp.repeat(tile_expert, tm)
    row_src = (grp_start - row_start)[row_e].astype(jnp.int32) + jnp.arange(n_rows, dtype=jnp.int32)
    row_token = order[jnp.clip(row_src, 0, n_asg - 1)] // TOP_K

    xs = x2[row_token]
    f = w_gu.shape[2] // 2
    tf = MOE_TF
    nf = f // tf
    h = pl.pallas_call(
        _moe_up_kernel,
        grid_spec=pltpu.PrefetchScalarGridSpec(
            num_scalar_prefetch=2,
            grid=(nf, n_tiles),
            in_specs=[pl.BlockSpec((tm, d), lambda j, i, te, nu: (i, 0)),
                      pl.BlockSpec((1, d, tf), lambda j, i, te, nu: (te[i], 0, j)),
                      pl.BlockSpec((1, d, tf), lambda j, i, te, nu: (te[i], 0, j + nf))],
            out_specs=pl.BlockSpec((tm, tf), lambda j, i, te, nu: (i, j)),
            scratch_shapes=[pltpu.VMEM((d, tf), MXU_DTYPE), pltpu.VMEM((d, tf), MXU_DTYPE)],
        ),
        out_shape=jax.ShapeDtypeStruct((n_rows, f), MXU_DTYPE),
        compiler_params=_params("arbitrary", "arbitrary"),
        name="moe_up",
    )(tile_expert, n_used, xs, w_gu, w_gu)
    ys = pl.pallas_call(
        _moe_down_kernel,
        grid_spec=pltpu.PrefetchScalarGridSpec(
            num_scalar_prefetch=2,
            grid=(n_tiles,),
            in_specs=[pl.BlockSpec((tm, f), lambda i, te, nu: (i, 0)),
                      pl.BlockSpec((1, f, d), lambda i, te, nu: (te[i], 0, 0))],
            out_specs=pl.BlockSpec((tm, d), lambda i, te, nu: (i, 0)),
            scratch_shapes=[pltpu.VMEM((f, d), MXU_DTYPE)],
        ),
        out_shape=jax.ShapeDtypeStruct((n_rows, d), F32),
        compiler_params=_params("arbitrary"),
        name="moe_down",
    )(tile_expert, n_used, h, w_down)

    y0 = ys[pos[:, 0]]
    y1 = ys[pos[:, 1]]
    return pl.pallas_call(
        _add_ln_kernel,
        grid=(m // 512,),
        in_specs=[pl.BlockSpec((512, d), lambda i: (i, 0)), pl.BlockSpec((512, LANE), lambda i: (i, 0)),
                  pl.BlockSpec((512, d), lambda i: (i, 0)), pl.BlockSpec((512, d), lambda i: (i, 0))]
        + [pl.BlockSpec((1, d), lambda i: (0, 0))] * 2,
        out_specs=pl.BlockSpec((512, d), lambda i: (i, 0)),
        out_shape=jax.ShapeDtypeStruct((m, d), F32),
        compiler_params=_params("parallel"),
        name="moe_add_ln",
    )(x2, rt, y0, y1, ln_g.reshape(1, d), ln_b.reshape(1, d))


def kernel(x, l0_w_in, l0_cmp_pe_k, l0_cmp_pe_v, l0_cmp_wk1, l0_cmp_wk2, l0_cmp_wv1, l0_cmp_wv2, l0_w_o, l0_ln1_g, l0_ln1_b, l0_ffn_w_gu, l0_ffn_w_down, l0_ln2_g, l0_ln2_b, l1_w_in, l1_w_gate2, l1_b_gate2, l1_head_norm_g, l1_w_o, l1_ln1_g, l1_ln1_b, l1_router, l1_moe_w_gu, l1_moe_w_down, l1_ln2_g, l1_ln2_b):
    bsz, t, d = x.shape
    x2 = x.reshape(bsz * t, d)

    o = _nsa_mixer(x2, bsz, t, l0_w_in, l0_cmp_pe_k, l0_cmp_pe_v, l0_cmp_wk1, l0_cmp_wk2,
                   l0_cmp_wv1, l0_cmp_wv2)
    x2 = _matmul_res_ln(o, _mx(l0_w_o), x2, l0_ln1_g, l0_ln1_b, 512)
    hmid = _swiglu_up(x2, _mx(l0_ffn_w_gu), 512, FFN_DENSE)
    x2 = _matmul_res_ln(hmid, _mx(l0_ffn_w_down), x2, l0_ln2_g, l0_ln2_b, 512)

    o = _gla_mixer(x2, bsz, t, l1_w_in, l1_w_gate2, l1_b_gate2, l1_head_norm_g)
    x2 = _matmul_res_ln(o, _mx(l1_w_o), x2, l1_ln1_g, l1_ln1_b, 512)
    x2 = _moe_layer(x2, _route(x2, l1_router, 512), l1_moe_w_gu, l1_moe_w_down, l1_ln2_g, l1_ln2_b)
    return x2.reshape(bsz, t, d)
```
